```python
import jax, jax.numpy as jnp
from jax import lax
import numpy as np

D_MODEL = 1024
BATCH = 32
SEQ = 2048
DEPTH = 2

N_A_LAYERS = DEPTH // 2
N_B_LAYERS = DEPTH - N_A_LAYERS

NORM_EPS = 1e-6

A_WINDOWS = (128, 512, 2048)
A_DILATIONS = (1, 4, 16)
A_GROUPS = 3
A_HEADS = 8
A_HEAD_DIM = 128
A_WIDTH = A_HEADS * A_HEAD_DIM
A_ROT_DIM = A_HEAD_DIM // 4
A_ROPE_THETA = 500000.0
BAND_BLOCK = 128
A_IN_WIDTH = A_GROUPS * 3 * A_WIDTH + A_WIDTH

B_HEADS = 16
B_NOPE = 64
B_ROPE = 32
B_QK_DIM = B_NOPE + B_ROPE
B_VDIM = 64
B_WIDTH = B_HEADS * B_VDIM
B_Q_LORA = 384
B_KV_LORA = 256
B_ROPE_THETA = 10000.0
B_IN_WIDTH = B_Q_LORA + B_WIDTH
ATTN_BLOCK = 128

kernel_name = "yoco_dilated_swa_mla_hybrid"


def rms_norm(x, g):
    xf = x.astype(jnp.float32)
    y = xf * lax.rsqrt(jnp.mean(xf * xf, axis=-1, keepdims=True) + NORM_EPS)
    return (y * g.astype(jnp.float32)).astype(x.dtype)


def rope(x, positions, theta):
    dim = x.shape[-1]
    half = dim // 2
    inv_freq = 1.0 / (theta ** (jnp.arange(half, dtype=jnp.float32) * (2.0 / dim)))
    ang = positions.astype(jnp.float32)[..., None] * inv_freq
    cos = jnp.cos(ang)[:, :, None, :]
    sin = jnp.sin(ang)[:, :, None, :]
    xf = x.astype(jnp.float32)
    x1, x2 = xf[..., :half], xf[..., half:]
    out = jnp.concatenate([x1 * cos - x2 * sin, x2 * cos + x1 * sin], axis=-1)
    return out.astype(x.dtype)


def partial_rope(x, positions):
    return jnp.concatenate(
        [rope(x[..., :A_ROT_DIM], positions, A_ROPE_THETA), x[..., A_ROT_DIM:]], axis=-1)


def dilated_window_attention(q, k, v, window, dilation):
    B, T, H, Dh = q.shape
    L = T // dilation
    w_sub = window // dilation
    Q = BAND_BLOCK
    nb = -(-L // Q)
    Lp = nb * Q

    def split(a):
        a = a.reshape(B, L, dilation, H, Dh).transpose(0, 2, 1, 3, 4)
        a = jnp.pad(a, ((0, 0), (0, 0), (0, Lp - L), (0, 0), (0, 0)))
        return a.reshape(B, dilation, nb, Q, H, Dh)

    def band(a):
        prev = jnp.pad(a, ((0, 0), (0, 0), (1, 0), (0, 0), (0, 0), (0, 0)))[:, :, :-1]
        return jnp.concatenate([prev, a], axis=3)

    qb = split(q)
    kb = band(split(k))
    vb = band(split(v))
    s = jnp.einsum('brnqhd,brnkhd->brnhqk', qb, kb,
                   preferred_element_type=jnp.float32) * (A_HEAD_DIM ** -0.5)
    qi = jnp.arange(Q)[:, None]
    ki = jnp.arange(2 * Q)[None, :]
    dist = Q + qi - ki
    key_pos = (jnp.arange(nb)[:, None, None] - 1) * Q + ki[None]
    allowed = (dist >= 0)[None] & (dist <= w_sub)[None] & (key_pos >= 0)
    s = jnp.where(allowed[None, None, :, None], s, -jnp.inf)
    m = jnp.max(s, axis=-1, keepdims=True)
    p = jnp.exp(s - m)
    l = jnp.sum(p, axis=-1)
    o = jnp.einsum('brnhqk,brnkhd->brnqhd', p, vb.astype(jnp.float32))
    l_q = l.transpose(0, 1, 2, 4, 3)
    o = o / l_q[..., None]
    lse = (m[..., 0] + jnp.log(l)).transpose(0, 1, 2, 4, 3)
    o = o.reshape(B, dilation, Lp, H, Dh)[:, :, :L].transpose(0, 2, 1, 3, 4).reshape(B, T, H, Dh)
    lse = lse.reshape(B, dilation, Lp, H)[:, :, :L].transpose(0, 2, 1, 3).reshape(B, T, H)
    return o, lse


def mixer_a(h, positions, w_in, w_out):
    B, T, _ = h.shape
    proj = h @ w_in
    qkv = proj[..., :A_GROUPS * 3 * A_WIDTH].reshape(B, T, A_GROUPS, 3, A_HEADS, A_HEAD_DIM)
    z = proj[..., A_GROUPS * 3 * A_WIDTH:]
    outs, lses = [], []
    for g in range(A_GROUPS):
        q = partial_rope(qkv[:, :, g, 0], positions)
        k = partial_rope(qkv[:, :, g, 1], positions)
        v = qkv[:, :, g, 2]
        o, lse = dilated_window_attention(q, k, v, A_WINDOWS[g], A_DILATIONS[g])
        outs.append(o)
        lses.append(lse)
    wts = jax.nn.softmax(jnp.stack(lses, axis=0), axis=0)
    o = jnp.einsum('gbth,gbthd->bthd', wts, jnp.stack(outs, axis=0))
    y = o.reshape(B, T, A_WIDTH).astype(h.dtype) * jax.nn.silu(z)
    return y @ w_out


def shared_latent_kv(h, positions, kv_norm, kv_w_down, kv_latent_norm, kv_w_up):
    B, T, _ = h.shape
    hn = rms_norm(h, kv_norm)
    ckr = hn @ kv_w_down
    c_kv = rms_norm(ckr[..., :B_KV_LORA], kv_latent_norm)
    k_rope = rope(ckr[..., B_KV_LORA:][:, :, None, :], positions, B_ROPE_THETA)
    kv = (c_kv @ kv_w_up).reshape(B, T, B_HEADS, B_NOPE + B_VDIM)
    k = jnp.concatenate(
        [kv[..., :B_NOPE], jnp.broadcast_to(k_rope, (B, T, B_HEADS, B_ROPE))], axis=-1)
    v = kv[..., B_NOPE:]
    return k, v


def causal_block_attention(q, k, v):
    B, T, H, Dq = q.shape
    Dv = v.shape[-1]
    nb = T // ATTN_BLOCK
    q_blocks = q.reshape(B, nb, ATTN_BLOCK, H, Dq).transpose(1, 0, 2, 3, 4)
    k_pos = jnp.arange(T)
    vf = v.astype(jnp.float32)

    def one_block(args):
        qb, idx = args
        s = jnp.einsum('bqhd,bkhd->bhqk', qb, k,
                       preferred_element_type=jnp.float32) * (B_QK_DIM ** -0.5)
        q_pos = idx * ATTN_BLOCK + jnp.arange(ATTN_BLOCK)
        mask = k_pos[None, :] <= q_pos[:, None]
        p = jax.nn.softmax(jnp.where(mask[None, None], s, -jnp.inf), axis=-1)
        return jnp.einsum('bhqk,bkhd->bqhd', p, vf)

    o = lax.map(one_block, (q_blocks, jnp.arange(nb)))
    return o.transpose(1, 0, 2, 3, 4).reshape(B, T, H, Dv)


def mixer_b(h, positions, k, v, w_in, q_norm, w_q_up, w_out):
    B, T, _ = h.shape
    proj = h @ w_in
    c_q = rms_norm(proj[..., :B_Q_LORA], q_norm)
    z = proj[..., B_Q_LORA:]
    q = (c_q @ w_q_up).reshape(B, T, B_HEADS, B_QK_DIM)
    q = jnp.concatenate([q[..., :B_NOPE], rope(q[..., B_NOPE:], positions, B_ROPE_THETA)], axis=-1)
    o = causal_block_attention(q, k, v)
    y = o.reshape(B, T, B_WIDTH).astype(h.dtype) * jax.nn.silu(z)
    return y @ w_out


def setup_inputs(seed: int = 0) -> dict:
    key = jax.random.key(seed)
    ks = jax.random.split(key, 20)
    f32 = jnp.float32

    def w(k, shape, fan_in):
        return jax.random.normal(k, shape, f32) * (fan_in ** -0.5)

    def gain(k, shape):
        return 1.0 + 0.01 * jax.random.normal(k, shape, f32)

    x = jax.random.normal(ks[0], (BATCH, SEQ, D_MODEL), f32)
    offsets = jax.random.randint(ks[1], (BATCH, 1), 0, 4096, dtype=jnp.int32)
    positions = offsets + jnp.arange(SEQ, dtype=jnp.int32)[None, :]
    return {
        "x": x,
        "positions": positions,
        "a_pre_norm": gain(ks[2], (N_A_LAYERS, D_MODEL)),
        "a_w_in": w(ks[3], (N_A_LAYERS, D_MODEL, A_IN_WIDTH), D_MODEL),
        "a_w_out": w(ks[4], (N_A_LAYERS, A_WIDTH, D_MODEL), A_WIDTH),
        "a_post_norm": gain(ks[5], (N_A_LAYERS, D_MODEL)),
        "kv_norm": gain(ks[6], (D_MODEL,)),
        "kv_w_down": w(ks[7], (D_MODEL, B_KV_LORA + B_ROPE), D_MODEL),
        "kv_latent_norm": gain(ks[8], (B_KV_LORA,)),
        "kv_w_up": w(ks[9], (B_KV_LORA, B_HEADS * (B_NOPE + B_VDIM)), B_KV_LORA),
        "b_pre_norm": gain(ks[10], (N_B_LAYERS, D_MODEL)),
        "b_w_in": w(ks[11], (N_B_LAYERS, D_MODEL, B_IN_WIDTH), D_MODEL),
        "b_q_norm": gain(ks[12], (N_B_LAYERS, B_Q_LORA)),
        "b_w_q_up": w(ks[13], (N_B_LAYERS, B_Q_LORA, B_HEADS * B_QK_DIM), B_Q_LORA),
        "b_w_out": w(ks[14], (N_B_LAYERS, B_WIDTH, D_MODEL), B_WIDTH),
        "b_post_norm": gain(ks[15], (N_B_LAYERS, D_MODEL)),
    }


def reference(x, positions, a_pre_norm, a_w_in, a_w_out, a_post_norm,
              kv_norm, kv_w_down, kv_latent_norm, kv_w_up,
              b_pre_norm, b_w_in, b_q_norm, b_w_q_up, b_w_out, b_post_norm):
    h = x
    k_shared = None
    v_shared = None
    for layer in range(DEPTH):
        if layer < N_A_LAYERS:
            i = layer
            y = mixer_a(rms_norm(h, a_pre_norm[i]), positions, a_w_in[i], a_w_out[i])
            h = h + rms_norm(y, a_post_norm[i])
        else:
            if layer == N_A_LAYERS:
                k_shared, v_shared = shared_latent_kv(
                    h, positions, kv_norm, kv_w_down, kv_latent_norm, kv_w_up)
            i = layer - N_A_LAYERS
            y = mixer_b(rms_norm(h, b_pre_norm[i]), positions, k_shared, v_shared,
                        b_w_in[i], b_q_norm[i], b_w_q_up[i], b_w_out[i])
            h = h + rms_norm(y, b_post_norm[i])
    return h
```

```python
import functools
import math

import jax
import jax.numpy as jnp
from jax import lax
from jax.experimental import pallas as pl
from jax.experimental.pallas import tpu as pltpu

F32 = jnp.float32
BF16 = jnp.bfloat16

NORM_EPS = 1e-6
LANES = 128

A_WINDOWS = (128, 512, 2048)
A_DILATIONS = (1, 4, 16)
A_HEADS = 8
A_HEAD_DIM = 128
A_WIDTH = A_HEADS * A_HEAD_DIM
A_ROT_DIM = A_HEAD_DIM // 4
A_ROPE_THETA = 500000.0
BAND = 128

B_HEADS = 16
B_NOPE = 64
B_ROPE = 32
B_QK_DIM = B_NOPE + B_ROPE
B_VDIM = 64
B_WIDTH = B_HEADS * B_VDIM
B_Q_LORA = 384
B_KV_LORA = 256
B_ROPE_THETA = 10000.0
B_PAIRS = B_HEADS // 2

VMEM_LIMIT = 56 * 1024 * 1024


def _cparams(sem):
    return pltpu.CompilerParams(dimension_semantics=sem, vmem_limit_bytes=VMEM_LIMIT)


def _rms(x, g):
    ms = jnp.mean(x * x, axis=-1, keepdims=True)
    return x * lax.rsqrt(ms + NORM_EPS) * g


def _rot_half(a, lo, cos, sin):
    lane = lax.broadcasted_iota(jnp.int32, a.shape, 1)
    partner = jnp.where(lane < lo + 16, pltpu.roll(a, LANES - 16, 1), pltpu.roll(a, 16, 1))
    return a * cos + partner * sin


def _rope_table_kernel(pos_ref, freq_ref, sign_ref, ca_ref, sa_ref, cb_ref, sb_ref):
    pos = pos_ref[0].astype(F32)
    ang_a = pos * freq_ref[0:1, :]
    ang_b = pos * freq_ref[1:2, :]
    ca_ref[0] = jnp.cos(ang_a)
    sa_ref[0] = jnp.sin(ang_a) * sign_ref[0:1, :]
    cb_ref[0] = jnp.cos(ang_b)
    sb_ref[0] = jnp.sin(ang_b) * sign_ref[1:2, :]


def _rope_tables(positions):
    B, T = positions.shape
    half = A_ROT_DIM // 2
    inv_a = 1.0 / (A_ROPE_THETA ** (jnp.arange(half, dtype=F32) * (2.0 / A_ROT_DIM)))
    inv_b = 1.0 / (B_ROPE_THETA ** (jnp.arange(B_ROPE // 2, dtype=F32) * (2.0 / B_ROPE)))
    zeros = lambda n: jnp.zeros((n,), F32)
    ones = jnp.ones((half,), F32)
    freq = jnp.stack([
        jnp.concatenate([inv_a, inv_a, zeros(LANES - 2 * half)]),
        jnp.concatenate([zeros(B_NOPE), inv_b, inv_b, zeros(LANES - B_QK_DIM)]),
    ])
    sign = jnp.stack([
        jnp.concatenate([-ones, ones, zeros(LANES - 2 * half)]),
        jnp.concatenate([zeros(B_NOPE), -ones, ones, zeros(LANES - B_QK_DIM)]),
    ])
    tab = jax.ShapeDtypeStruct((B, T, LANES), F32)
    spec = pl.BlockSpec((1, T, LANES), lambda b: (b, 0, 0))
    const = pl.BlockSpec((2, LANES), lambda b: (0, 0))
    return pl.pallas_call(
        _rope_table_kernel,
        grid=(B,),
        in_specs=[pl.BlockSpec((1, T, 1), lambda b: (b, 0, 0)), const, const],
        out_specs=[spec] * 4,
        out_shape=[tab] * 4,
        compiler_params=_cparams(("parallel",)),
        name="rope_tables",
    )(positions.reshape(B, T, 1), freq, sign)


def _proj_a_kernel(x_ref, c_ref, s_ref, g_ref, w_ref, qkv_ref, *z_ref, kr):
    x = x_ref[0]
    D = x.shape[1] // kr
    hs, cs, ss = [], [], []
    for c in range(kr):
        hs.append(_rms(x[:, c * D:(c + 1) * D], g_ref[...]).astype(BF16))
        cs.append(c_ref[0][:, c * LANES:(c + 1) * LANES])
        ss.append(s_ref[0][:, c * LANES:(c + 1) * LANES])
    h = jnp.concatenate(hs, axis=0) if kr > 1 else hs[0]
    cos = jnp.concatenate(cs, axis=0) if kr > 1 else cs[0]
    sin = jnp.concatenate(ss, axis=0) if kr > 1 else ss[0]
    scale = A_HEAD_DIM ** -0.5
    for s in range(3):
        acc = jnp.dot(h, w_ref[:, s * A_WIDTH:(s + 1) * A_WIDTH],
                      preferred_element_type=F32)
        for hd in range(A_HEADS):
            a = acc[:, hd * LANES:(hd + 1) * LANES]
            if s == 0:
                a = _rot_half(a, 0, cos * scale, sin * scale)
            elif s == 1:
                a = _rot_half(a, 0, cos, sin)
            qkv_ref[0, s * A_HEADS + hd] = a.astype(BF16)
    if z_ref:
        z_ref[0][0] = jnp.dot(h, w_ref[:, 3 * A_WIDTH:4 * A_WIDTH],
                              preferred_element_type=F32).astype(BF16)


def _proj_a(x, cos_a, sin_a, gain, w, dil, with_z, tm=512):
    B, T, D = x.shape
    L = T // dil
    tl = min(L, tm)
    kr = tm // tl
    nlb = L // tl
    assert L % tl == 0 and dil % kr == 0 and T % tm == 0
    ncols = w.shape[1]
    xmap = lambda b, i: (b, i % nlb, i // nlb)
    out_shape = [jax.ShapeDtypeStruct((B, 3 * A_HEADS, T, LANES), BF16)]
    out_specs = [pl.BlockSpec((1, 3 * A_HEADS, tm, LANES), lambda b, i: (b, 0, i, 0))]
    if with_z:
        out_shape.append(jax.ShapeDtypeStruct((B, T, A_WIDTH), BF16))
        out_specs.append(pl.BlockSpec((1, tm, A_WIDTH), lambda b, i: (b, i, 0)))
    return pl.pallas_call(
        functools.partial(_proj_a_kernel, kr=kr),
        grid=(B, T // tm),
        in_specs=[
            pl.BlockSpec((1, tl, kr * D), xmap),
            pl.BlockSpec((1, tl, kr * LANES), xmap),
            pl.BlockSpec((1, tl, kr * LANES), xmap),
            pl.BlockSpec((1, D), lambda b, i: (0, 0)),
            pl.BlockSpec((D, ncols), lambda b, i: (0, 0)),
        ],
        out_specs=out_specs,
        out_shape=out_shape,
        compiler_params=_cparams(("parallel", "parallel")),
        name=f"proj_a_d{dil}",
    )(x.reshape(B, L, dil * D), cos_a.reshape(B, L, dil * LANES),
      sin_a.reshape(B, L, dil * LANES), gain.reshape(1, D), w)


def _band_block(q, k, v, first):
    s = lax.dot_general(q, k, (((1,), (1,)), ((), ())), preferred_element_type=F32)
    qi = lax.broadcasted_iota(jnp.int32, s.shape, 0)
    ki = lax.broadcasted_iota(jnp.int32, s.shape, 1)
    allowed = (ki <= qi) if first else ((ki >= qi) & (ki <= qi + BAND))
    s = jnp.where(allowed, s, -jnp.inf)
    m = jnp.max(s, axis=-1, keepdims=True)
    p = jnp.exp(s - m)
    l = jnp.sum(p, axis=-1, keepdims=True)
    o = jnp.dot(p.astype(BF16), v, preferred_element_type=F32) / l
    return o, m + jnp.log(l)


def _attn_a_kernel(g0_ref, g1_ref, g2_ref, o_ref, onat, lnat):
    T = o_ref.shape[2]
    for g, (ref, dil) in enumerate(zip((g0_ref, g1_ref, g2_ref), A_DILATIONS)):
        L = T // dil
        nb = L // BAND

        def put(r, n, o, lse, g=g, dil=dil):
            start = r + n * (BAND * dil)
            if dil == 1:
                idx = pl.ds(pl.multiple_of(start, BAND), BAND)
            else:
                idx = pl.ds(start, BAND, stride=dil)
            onat[g, idx, :] = o
            lnat[g, idx, :] = jnp.broadcast_to(lse, (BAND, LANES))

        def per_class(r, carry, ref=ref, L=L, nb=nb, put=put):
            base = pl.multiple_of(r * L, BAND)
            q = ref[0, 0, 0, pl.ds(base, BAND), :]
            k = ref[0, 1, 0, pl.ds(base, BAND), :]
            v = ref[0, 2, 0, pl.ds(base, BAND), :]
            o, lse = _band_block(q, k, v, True)
            put(r, 0, o, lse)

            def per_block(n, c):
                qs = pl.multiple_of(base + n * BAND, BAND)
                ks = pl.multiple_of(base + (n - 1) * BAND, BAND)
                q = ref[0, 0, 0, pl.ds(qs, BAND), :]
                k = ref[0, 1, 0, pl.ds(ks, 2 * BAND), :]
                v = ref[0, 2, 0, pl.ds(ks, 2 * BAND), :]
                o, lse = _band_block(q, k, v, False)
                put(r, n, o, lse)
                return c

            if nb > 1:
                lax.fori_loop(1, nb, per_block, 0)
            return carry

        lax.fori_loop(0, dil, per_class, 0)

    rows = 256

    def merge(i, carry):
        sl = pl.ds(pl.multiple_of(i * rows, rows), rows)
        l0, l1, l2 = lnat[0, sl, :], lnat[1, sl, :], lnat[2, sl, :]
        m = jnp.maximum(jnp.maximum(l0, l1), l2)
        e0, e1, e2 = jnp.exp(l0 - m), jnp.exp(l1 - m), jnp.exp(l2 - m)
        num = e0 * onat[0, sl, :] + e1 * onat[1, sl, :] + e2 * onat[2, sl, :]
        o_ref[0, 0, sl, :] = (num / (e0 + e1 + e2)).astype(BF16)
        return carry

    lax.fori_loop(0, T // rows, merge, 0)


def _attn_a(qkv):
    B, _, T, _ = qkv[0].shape
    views = [a.reshape(B, 3, A_HEADS, T, LANES) for a in qkv]
    spec = pl.BlockSpec((1, 3, 1, T, LANES), lambda b, h: (b, 0, h, 0, 0))
    return pl.pallas_call(
        _attn_a_kernel,
        grid=(B, A_HEADS),
        in_specs=[spec] * 3,
        out_specs=pl.BlockSpec((1, 1, T, LANES), lambda b, h: (b, h, 0, 0)),
        out_shape=jax.ShapeDtypeStruct((B, A_HEADS, T, LANES), BF16),
        scratch_shapes=[pltpu.VMEM((3, T, LANES), F32), pltpu.VMEM((3, T, LANES), F32)],
        compiler_params=_cparams(("parallel", "parallel")),
        name="attn_a",
    )(*views)


def _gated(o_ref, z_ref):
    o = jnp.concatenate([o_ref[0, h] for h in range(o_ref.shape[1])], axis=1)
    z = z_ref[0].astype(F32)
    return (o.astype(F32) * (z * jax.nn.sigmoid(z))).astype(BF16)


def _mid_kernel(o_ref, z_ref, x_ref, cb_ref, sb_ref,
                wout_ref, gpost_ref, gkv_ref, wd_ref, glat_ref, wupk_ref, wupv_ref,
                gpre_ref, wcq_ref, wz_ref, gq_ref, wqup_ref,
                h1_ref, k_ref, v_ref, q_ref, zb_ref):
    y = jnp.dot(_gated(o_ref, z_ref), wout_ref[...], preferred_element_type=F32)
    h1 = x_ref[0] + _rms(y, gpost_ref[...])
    h1_ref[0] = h1
    cos, sin = cb_ref[0], sb_ref[0]

    hn = _rms(h1, gkv_ref[...]).astype(BF16)
    ckr = jnp.dot(hn, wd_ref[...], preferred_element_type=F32)
    c_kv = _rms(ckr[:, :B_KV_LORA], glat_ref[...]).astype(BF16)
    k_rope = _rot_half(ckr[:, B_KV_LORA:], B_NOPE, cos, sin)
    kn = jnp.dot(c_kv, wupk_ref[...], preferred_element_type=F32)
    for h in range(B_HEADS):
        k_ref[0, h] = (kn[:, h * LANES:(h + 1) * LANES] + k_rope).astype(BF16)
    vv = jnp.dot(c_kv, wupv_ref[...], preferred_element_type=F32)
    for p in range(B_PAIRS):
        v_ref[0, p] = vv[:, p * LANES:(p + 1) * LANES].astype(BF16)

    hb = _rms(h1, gpre_ref[...]).astype(BF16)
    c_q = jnp.dot(hb, wcq_ref[...], preferred_element_type=F32)
    c_q = _rms(c_q, gq_ref[...]).astype(BF16)
    qq = jnp.dot(c_q, wqup_ref[...], preferred_element_type=F32)
    scale = B_QK_DIM ** -0.5
    cq, sq = cos * scale, sin * scale
    for h in range(B_HEADS):
        q_ref[0, h] = _rot_half(qq[:, h * LANES:(h + 1) * LANES], B_NOPE, cq, sq).astype(BF16)
    zb_ref[0] = jnp.dot(hb, wz_ref[...], preferred_element_type=F32).astype(BF16)


def _mid(o_a, z_a, x, cos_b, sin_b, weights, tm=256):
    B, T, D = x.shape
    tok = lambda w: pl.BlockSpec((1, tm, w), lambda b, i: (b, i, 0))
    heads = lambda n: pl.BlockSpec((1, n, tm, LANES), lambda b, i: (b, 0, i, 0))
    const = lambda a: pl.BlockSpec(a.shape, lambda b, i: (0, 0))
    return pl.pallas_call(
        _mid_kernel,
        grid=(B, T // tm),
        in_specs=[heads(A_HEADS), tok(A_WIDTH), tok(D), tok(LANES), tok(LANES)]
                 + [const(w) for w in weights],
        out_specs=[tok(D), heads(B_HEADS), heads(B_PAIRS), heads(B_HEADS), tok(B_WIDTH)],
        out_shape=[
            jax.ShapeDtypeStruct((B, T, D), F32),
            jax.ShapeDtypeStruct((B, B_HEADS, T, LANES), BF16),
            jax.ShapeDtypeStruct((B, B_PAIRS, T, LANES), BF16),
            jax.ShapeDtypeStruct((B, B_HEADS, T, LANES), BF16),
            jax.ShapeDtypeStruct((B, T, B_WIDTH), BF16),
        ],
        compiler_params=_cparams(("parallel", "parallel")),
        name="mid",
    )(o_a, z_a, x, cos_b, sin_b, *weights)


def _attn_b_kernel(q_ref, k_ref, v_ref, o_ref, *, tq, tk):
    i = pl.program_id(2)
    outs = []
    for hh in range(2):
        q = q_ref[0, hh]

        def step(j, carry, q=q, hh=hh, masked=False):
            m, l, acc = carry
            ks = pl.multiple_of(j * tk, tk)
            k = k_ref[0, hh, pl.ds(ks, tk), :]
            v = v_ref[0, 0, pl.ds(ks, tk), :]
            s = lax.dot_general(q, k, (((1,), (1,)), ((), ())), preferred_element_type=F32)
            if masked:
                qi = lax.broadcasted_iota(jnp.int32, s.shape, 0)
                ki = lax.broadcasted_iota(jnp.int32, s.shape, 1)
                s = jnp.where(ki <= qi, s, -jnp.inf)
            m_new = jnp.maximum(m, jnp.max(s, axis=-1, keepdims=True))
            alpha = jnp.exp(m - m_new)
            p = jnp.exp(s - m_new)
            l = alpha * l + jnp.sum(p, axis=-1, keepdims=True)
            acc = alpha * acc + jnp.dot(p.astype(BF16), v, preferred_element_type=F32)
            return m_new, l, acc

        init = (jnp.full((tq, 1), -jnp.inf, F32), jnp.zeros((tq, 1), F32),
                jnp.zeros((tq, LANES), F32))
        carry = lax.fori_loop(0, i, step, init)
        m, l, acc = step(i, carry, masked=True)
        outs.append(acc / l)
    lane = lax.broadcasted_iota(jnp.int32, (tq, LANES), 1)
    o_ref[0, 0] = jnp.where(lane < B_VDIM, outs[0], outs[1]).astype(BF16)


def _attn_b(q, k, v, tq=256):
    B, _, T, _ = q.shape
    return pl.pallas_call(
        functools.partial(_attn_b_kernel, tq=tq, tk=tq),
        grid=(B, B_PAIRS, T // tq),
        in_specs=[
            pl.BlockSpec((1, 2, tq, LANES), lambda b, p, i: (b, p, i, 0)),
            pl.BlockSpec((1, 2, T, LANES), lambda b, p, i: (b, p, 0, 0)),
            pl.BlockSpec((1, 1, T, LANES), lambda b, p, i: (b, p, 0, 0)),
        ],
        out_specs=pl.BlockSpec((1, 1, tq, LANES), lambda b, p, i: (b, p, i, 0)),
        out_shape=jax.ShapeDtypeStruct((B, B_PAIRS, T, LANES), BF16),
        compiler_params=_cparams(("parallel", "parallel", "arbitrary")),
        name="attn_b",
    )(q, k, v)


def _final_kernel(o_ref, z_ref, h1_ref, w_ref, g_ref, out_ref):
    y = jnp.dot(_gated(o_ref, z_ref), w_ref[...], preferred_element_type=F32)
    out_ref[0] = h1_ref[0] + _rms(y, g_ref[...])


def _final(o_b, z_b, h1, w, gain, tm=512):
    B, T, D = h1.shape
    tok = lambda w_: pl.BlockSpec((1, tm, w_), lambda b, i: (b, i, 0))
    return pl.pallas_call(
        _final_kernel,
        grid=(B, T // tm),
        in_specs=[pl.BlockSpec((1, B_PAIRS, tm, LANES), lambda b, i: (b, 0, i, 0)),
                  tok(B_WIDTH), tok(D),
                  pl.BlockSpec(w.shape, lambda b, i: (0, 0)),
                  pl.BlockSpec((1, D), lambda b, i: (0, 0))],
        out_specs=tok(D),
        out_shape=jax.ShapeDtypeStruct((B, T, D), F32),
        compiler_params=_cparams(("parallel", "parallel")),
        name="final",
    )(o_b, z_b, h1, w, gain.reshape(1, D))


def kernel(x, positions, a_pre_norm, a_w_in, a_w_out, a_post_norm, kv_norm, kv_w_down, kv_latent_norm, kv_w_up, b_pre_norm, b_w_in, b_q_norm, b_w_q_up, b_w_out, b_post_norm):
    D = x.shape[-1]
    cos_a, sin_a, cos_b, sin_b = _rope_tables(positions)

    w_in = a_w_in[0].astype(BF16)
    gw = 3 * A_WIDTH
    qkv = []
    z_a = None
    for g, dil in enumerate(A_DILATIONS):
        w = w_in[:, g * gw:(g + 1) * gw]
        if g == 0:
            w = jnp.concatenate([w, w_in[:, 3 * gw:]], axis=1)
            qkv_g, z_a = _proj_a(x, cos_a, sin_a, a_pre_norm[0], w, dil, True)
        else:
            (qkv_g,) = _proj_a(x, cos_a, sin_a, a_pre_norm[0], w, dil, False)
        qkv.append(qkv_g)
    o_a = _attn_a(qkv)

    row = lambda g: g.reshape(1, -1)
    wd = jnp.zeros((D, B_KV_LORA + LANES), F32)
    wd = wd.at[:, :B_KV_LORA].set(kv_w_down[:, :B_KV_LORA])
    wd = wd.at[:, B_KV_LORA + B_NOPE:B_KV_LORA + B_QK_DIM].set(kv_w_down[:, B_KV_LORA:])
    wup = kv_w_up.reshape(B_KV_LORA, B_HEADS, B_NOPE + B_VDIM)
    wupk = jnp.pad(wup[:, :, :B_NOPE], ((0, 0), (0, 0), (0, LANES - B_NOPE)))
    wupk = wupk.reshape(B_KV_LORA, B_HEADS * LANES)
    wupv = wup[:, :, B_NOPE:].reshape(B_KV_LORA, B_WIDTH)
    wqup = jnp.pad(b_w_q_up[0].reshape(B_Q_LORA, B_HEADS, B_QK_DIM),
                   ((0, 0), (0, 0), (0, LANES - B_QK_DIM))).reshape(B_Q_LORA, B_HEADS * LANES)
    weights = [
        a_w_out[0].astype(BF16), row(a_post_norm[0]), row(kv_norm), wd.astype(BF16),
        row(kv_latent_norm), wupk.astype(BF16), wupv.astype(BF16),
        row(b_pre_norm[0]), b_w_in[0][:, :B_Q_LORA].astype(BF16),
        b_w_in[0][:, B_Q_LORA:].astype(BF16), row(b_q_norm[0]), wqup.astype(BF16),
    ]
    h1, k_b, v_b, q_b, z_b = _mid(o_a, z_a, x, cos_b, sin_b, weights)

    o_b = _attn_b(q_b, k_b, v_b)
    return _final(o_b, z_b, h1, b_w_out[0].astype(BF16), b_post_norm[0])
```

```python
import functools
import math

import jax
import jax.numpy as jnp
from jax import lax
from jax.experimental import pallas as pl
from jax.experimental.pallas import tpu as pltpu

F32 = jnp.float32
BF16 = jnp.bfloat16

NORM_EPS = 1e-6
LANES = 128

A_WINDOWS = (128, 512, 2048)
A_DILATIONS = (1, 4, 16)
A_HEADS = 8
A_HEAD_DIM = 128
A_WIDTH = A_HEADS * A_HEAD_DIM
A_ROT_DIM = A_HEAD_DIM // 4
A_ROPE_THETA = 500000.0
BAND = 128

B_HEADS = 16
B_NOPE = 64
B_ROPE = 32
B_QK_DIM = B_NOPE + B_ROPE
B_VDIM = 64
B_WIDTH = B_HEADS * B_VDIM
B_Q_LORA = 384
B_KV_LORA = 256
B_ROPE_THETA = 10000.0
B_PAIRS = B_HEADS // 2

VMEM_LIMIT = 56 * 1024 * 1024


def _cparams(sem):
    return pltpu.CompilerParams(dimension_semantics=sem, vmem_limit_bytes=VMEM_LIMIT)


def _rms(x, g):
    ms = jnp.mean(x * x, axis=-1, keepdims=True)
    return x * lax.rsqrt(ms + NORM_EPS) * g


def _rot_half(a, lo, cos, sin):
    lane = lax.broadcasted_iota(jnp.int32, a.shape, 1)
    partner = jnp.where(lane < lo + 16, pltpu.roll(a, LANES - 16, 1), pltpu.roll(a, 16, 1))
    return a * cos + partner * sin


def _rope_table_kernel(pos_ref, freq_ref, sign_ref, ca_ref, sa_ref, cb_ref, sb_ref):
    pos = pos_ref[0].astype(F32)
    ang_a = pos * freq_ref[0:1, :]
    ang_b = pos * freq_ref[1:2, :]
    ca_ref[0] = jnp.cos(ang_a)
    sa_ref[0] = jnp.sin(ang_a) * sign_ref[0:1, :]
    cb_ref[0] = jnp.cos(ang_b)
    sb_ref[0] = jnp.sin(ang_b) * sign_ref[1:2, :]


def _rope_tables(positions):
    B, T = positions.shape
    half = A_ROT_DIM // 2
    inv_a = 1.0 / (A_ROPE_THETA ** (jnp.arange(half, dtype=F32) * (2.0 / A_ROT_DIM)))
    inv_b = 1.0 / (B_ROPE_THETA ** (jnp.arange(B_ROPE // 2, dtype=F32) * (2.0 / B_ROPE)))
    zeros = lambda n: jnp.zeros((n,), F32)
    ones = jnp.ones((half,), F32)
    freq = jnp.stack([
        jnp.concatenate([inv_a, inv_a, zeros(LANES - 2 * half)]),
        jnp.concatenate([zeros(B_NOPE), inv_b, inv_b, zeros(LANES - B_QK_DIM)]),
    ])
    sign = jnp.stack([
        jnp.concatenate([-ones, ones, zeros(LANES - 2 * half)]),
        jnp.concatenate([zeros(B_NOPE), -ones, ones, zeros(LANES - B_QK_DIM)]),
    ])
    tab = jax.ShapeDtypeStruct((B, T, LANES), F32)
    spec = pl.BlockSpec((1, T, LANES), lambda b: (b, 0, 0))
    const = pl.BlockSpec((2, LANES), lambda b: (0, 0))
    return pl.pallas_call(
        _rope_table_kernel,
        grid=(B,),
        in_specs=[pl.BlockSpec((1, T, 1), lambda b: (b, 0, 0)), const, const],
        out_specs=[spec] * 4,
        out_shape=[tab] * 4,
        compiler_params=_cparams(("parallel",)),
        name="rope_tables",
    )(positions.reshape(B, T, 1), freq, sign)


def _proj_a_kernel(x_ref, c_ref, s_ref, g_ref, w_ref, qkv_ref, *z_ref, kr):
    x = x_ref[0]
    D = x.shape[1] // kr
    hs, cs, ss = [], [], []
    for c in range(kr):
        hs.append(_rms(x[:, c * D:(c + 1) * D], g_ref[...]).astype(BF16))
        cs.append(c_ref[0][:, c * LANES:(c + 1) * LANES])
        ss.append(s_ref[0][:, c * LANES:(c + 1) * LANES])
    h = jnp.concatenate(hs, axis=0) if kr > 1 else hs[0]
    cos = jnp.concatenate(cs, axis=0) if kr > 1 else cs[0]
    sin = jnp.concatenate(ss, axis=0) if kr > 1 else ss[0]
    scale = A_HEAD_DIM ** -0.5
    for s in range(3):
        acc = jnp.dot(h, w_ref[:, s * A_WIDTH:(s + 1) * A_WIDTH],
                      preferred_element_type=F32)
        for hd in range(A_HEADS):
            a = acc[:, hd * LANES:(hd + 1) * LANES]
            if s == 0:
                a = _rot_half(a, 0, cos * scale, sin * scale)
            elif s == 1:
                a = _rot_half(a, 0, cos, sin)
            qkv_ref[0, s * A_HEADS + hd] = a.astype(BF16)
    if z_ref:
        z_ref[0][0] = jnp.dot(h, w_ref[:, 3 * A_WIDTH:4 * A_WIDTH],
                              preferred_element_type=F32).astype(BF16)


def _proj_a(x, cos_a, sin_a, gain, w, dil, with_z, tm=512):
    B, T, D = x.shape
    L = T // dil
    tl = min(L, tm)
    kr = tm // tl
    nlb = L // tl
    assert L % tl == 0 and dil % kr == 0 and T % tm == 0
    ncols = w.shape[1]
    xmap = lambda b, i: (b, i % nlb, i // nlb)
    out_shape = [jax.ShapeDtypeStruct((B, 3 * A_HEADS, T, LANES), BF16)]
    out_specs = [pl.BlockSpec((1, 3 * A_HEADS, tm, LANES), lambda b, i: (b, 0, i, 0))]
    if with_z:
        out_shape.append(jax.ShapeDtypeStruct((B, T, A_WIDTH), BF16))
        out_specs.append(pl.BlockSpec((1, tm, A_WIDTH), lambda b, i: (b, i, 0)))
    return pl.pallas_call(
        functools.partial(_proj_a_kernel, kr=kr),
        grid=(B, T // tm),
        in_specs=[
            pl.BlockSpec((1, tl, kr * D), xmap),
            pl.BlockSpec((1, tl, kr * LANES), xmap),
            pl.BlockSpec((1, tl, kr * LANES), xmap),
            pl.BlockSpec((1, D), lambda b, i: (0, 0)),
            pl.BlockSpec((D, ncols), lambda b, i: (0, 0)),
        ],
        out_specs=out_specs,
        out_shape=out_shape,
        compiler_params=_cparams(("parallel", "parallel")),
        name=f"proj_a_d{dil}",
    )(x.reshape(B, L, dil * D), cos_a.reshape(B, L, dil * LANES),
      sin_a.reshape(B, L, dil * LANES), gain.reshape(1, D), w)


def _band_block(q, k, v, allowed):
    s = lax.dot_general(q, k, (((1,), (1,)), ((), ())), preferred_element_type=F32)
    s = jnp.where(allowed, s, -jnp.inf)
    m = jnp.max(s, axis=-1, keepdims=True)
    p = jnp.exp(s - m)
    l = jnp.sum(p, axis=-1, keepdims=True)
    o = jnp.dot(p.astype(BF16), v, preferred_element_type=F32) / l
    return o, m + jnp.log(l)


def _attn_a_kernel(g0_ref, g1_ref, g2_ref, o_ref, onat, lnat):
    T = o_ref.shape[2]
    qi = lax.broadcasted_iota(jnp.int32, (BAND, 2 * BAND), 0)
    ki = lax.broadcasted_iota(jnp.int32, (BAND, 2 * BAND), 1)
    band_mask = (ki >= qi) & (ki <= qi + BAND)
    first_mask = (lax.broadcasted_iota(jnp.int32, (BAND, BAND), 1)
                  <= lax.broadcasted_iota(jnp.int32, (BAND, BAND), 0))
    for g, (ref, dil) in enumerate(zip((g0_ref, g1_ref, g2_ref), A_DILATIONS)):
        nb = T // dil // BAND
        for j in range(T // BAND):
            r, n = divmod(j, nb)
            lo = (j - 1) * BAND if n else j * BAND
            q = ref[0, 0, 0, j * BAND:(j + 1) * BAND, :]
            k = ref[0, 1, 0, lo:(j + 1) * BAND, :]
            v = ref[0, 2, 0, lo:(j + 1) * BAND, :]
            o, lse = _band_block(q, k, v, band_mask if n else first_mask)
            start = r + n * BAND * dil
            idx = pl.ds(start, BAND, stride=dil) if dil > 1 else pl.ds(start, BAND)
            onat[g, idx, :] = o
            lnat[g, idx, :] = jnp.broadcast_to(lse, (BAND, LANES))

    rows = 256

    def merge(i, carry):
        sl = pl.ds(pl.multiple_of(i * rows, rows), rows)
        l0, l1, l2 = lnat[0, sl, :], lnat[1, sl, :], lnat[2, sl, :]
        m = jnp.maximum(jnp.maximum(l0, l1), l2)
        e0, e1, e2 = jnp.exp(l0 - m), jnp.exp(l1 - m), jnp.exp(l2 - m)
        num = e0 * onat[0, sl, :] + e1 * onat[1, sl, :] + e2 * onat[2, sl, :]
        o_ref[0, 0, sl, :] = (num / (e0 + e1 + e2)).astype(BF16)
        return carry

    lax.fori_loop(0, T // rows, merge, 0)


def _attn_a(qkv):
    B, _, T, _ = qkv[0].shape
    views = [a.reshape(B, 3, A_HEADS, T, LANES) for a in qkv]
    spec = pl.BlockSpec((1, 3, 1, T, LANES), lambda b, h: (b, 0, h, 0, 0))
    return pl.pallas_call(
        _attn_a_kernel,
        grid=(B, A_HEADS),
        in_specs=[spec] * 3,
        out_specs=pl.BlockSpec((1, 1, T, LANES), lambda b, h: (b, h, 0, 0)),
        out_shape=jax.ShapeDtypeStruct((B, A_HEADS, T, LANES), BF16),
        scratch_shapes=[pltpu.VMEM((3, T, LANES), F32), pltpu.VMEM((3, T, LANES), F32)],
        compiler_params=_cparams(("parallel", "parallel")),
        name="attn_a",
    )(*views)


def _gated(o_ref, z_ref):
    o = jnp.concatenate([o_ref[0, h] for h in range(o_ref.shape[1])], axis=1)
    z = z_ref[0].astype(F32)
    return (o.astype(F32) * (z * jax.nn.sigmoid(z))).astype(BF16)


def _mid_kernel(o_ref, z_ref, x_ref, cb_ref, sb_ref,
                wout_ref, gpost_ref, gkv_ref, wd_ref, glat_ref, wupk_ref, wupv_ref,
                gpre_ref, wcq_ref, wz_ref, gq_ref, wqup_ref,
                h1_ref, k_ref, v_ref, q_ref, zb_ref):
    y = jnp.dot(_gated(o_ref, z_ref), wout_ref[...], preferred_element_type=F32)
    h1 = x_ref[0] + _rms(y, gpost_ref[...])
    h1_ref[0] = h1
    cos, sin = cb_ref[0], sb_ref[0]

    hn = _rms(h1, gkv_ref[...]).astype(BF16)
    ckr = jnp.dot(hn, wd_ref[...], preferred_element_type=F32)
    c_kv = _rms(ckr[:, :B_KV_LORA], glat_ref[...]).astype(BF16)
    k_rope = _rot_half(ckr[:, B_KV_LORA:], B_NOPE, cos, sin)
    kn = jnp.dot(c_kv, wupk_ref[...], preferred_element_type=F32)
    for h in range(B_HEADS):
        k_ref[0, h] = (kn[:, h * LANES:(h + 1) * LANES] + k_rope).astype(BF16)
    vv = jnp.dot(c_kv, wupv_ref[...], preferred_element_type=F32)
    lane = lax.broadcasted_iota(jnp.int32, k_rope.shape, 1)
    one_col = jnp.where(lane == B_VDIM, 1.0, 0.0)
    for h in range(B_HEADS):
        v_ref[0, h] = (vv[:, h * LANES:(h + 1) * LANES] + one_col).astype(BF16)

    hb = _rms(h1, gpre_ref[...]).astype(BF16)
    c_q = jnp.dot(hb, wcq_ref[...], preferred_element_type=F32)
    c_q = _rms(c_q, gq_ref[...]).astype(BF16)
    qq = jnp.dot(c_q, wqup_ref[...], preferred_element_type=F32)
    scale = B_QK_DIM ** -0.5
    cq, sq = cos * scale, sin * scale
    for h in range(B_HEADS):
        q_ref[0, h] = _rot_half(qq[:, h * LANES:(h + 1) * LANES], B_NOPE, cq, sq).astype(BF16)
    zb_ref[0] = jnp.dot(hb, wz_ref[...], preferred_element_type=F32).astype(BF16)


def _mid(o_a, z_a, x, cos_b, sin_b, weights, tm=256):
    B, T, D = x.shape
    tok = lambda w: pl.BlockSpec((1, tm, w), lambda b, i: (b, i, 0))
    heads = lambda n: pl.BlockSpec((1, n, tm, LANES), lambda b, i: (b, 0, i, 0))
    const = lambda a: pl.BlockSpec(a.shape, lambda b, i: (0, 0))
    return pl.pallas_call(
        _mid_kernel,
        grid=(B, T // tm),
        in_specs=[heads(A_HEADS), tok(A_WIDTH), tok(D), tok(LANES), tok(LANES)]
                 + [const(w) for w in weights],
        out_specs=[tok(D), heads(B_HEADS), heads(B_HEADS), heads(B_HEADS), tok(B_WIDTH)],
        out_shape=[
            jax.ShapeDtypeStruct((B, T, D), F32),
            jax.ShapeDtypeStruct((B, B_HEADS, T, LANES), BF16),
            jax.ShapeDtypeStruct((B, B_HEADS, T, LANES), BF16),
            jax.ShapeDtypeStruct((B, B_HEADS, T, LANES), BF16),
            jax.ShapeDtypeStruct((B, T, B_WIDTH), BF16),
        ],
        compiler_params=_cparams(("parallel", "parallel")),
        name="mid",
    )(o_a, z_a, x, cos_b, sin_b, *weights)


def _attn_b_kernel(q_ref, k_ref, v_ref, o_ref, *, tq):
    i = pl.program_id(2)
    T = k_ref.shape[2]
    qi = lax.broadcasted_iota(jnp.int32, (tq, tq), 0)
    ki = lax.broadcasted_iota(jnp.int32, (tq, tq), 1)
    causal = ki <= qi
    lane = lax.broadcasted_iota(jnp.int32, (tq, LANES), 1)
    nt = (((1,), (1,)), ((), ()))

    def block(ii):
        lo = ii * tq
        outs = []
        for hh in range(2):
            q = q_ref[0, hh]
            s_d = lax.dot_general(q, k_ref[0, hh, lo:lo + tq, :], nt, preferred_element_type=F32)
            s_d = jnp.where(causal, s_d, -jnp.inf)
            m = jnp.max(s_d, axis=-1, keepdims=True)
            if ii:
                s_m = lax.dot_general(q, k_ref[0, hh, 0:lo, :], nt, preferred_element_type=F32)
                m = jnp.maximum(m, jnp.max(s_m, axis=-1, keepdims=True))
            o = jnp.dot(jnp.exp(s_d - m).astype(BF16), v_ref[0, hh, lo:lo + tq, :],
                        preferred_element_type=F32)
            if ii:
                o = o + jnp.dot(jnp.exp(s_m - m).astype(BF16), v_ref[0, hh, 0:lo, :],
                                preferred_element_type=F32)
            l = jnp.sum(jnp.where(lane == B_VDIM, o, 0.0), axis=-1, keepdims=True)
            outs.append(o / l)
        o_ref[0, 0] = jnp.where(lane < B_VDIM, outs[0],
                                pltpu.roll(outs[1], B_VDIM, 1)).astype(BF16)

    for ii in range(T // tq):
        pl.when(i == ii)(functools.partial(block, ii))


def _attn_b(q, k, v, tq=512):
    B, _, T, _ = q.shape
    return pl.pallas_call(
        functools.partial(_attn_b_kernel, tq=tq),
        grid=(B, B_PAIRS, T // tq),
        in_specs=[
            pl.BlockSpec((1, 2, tq, LANES), lambda b, p, i: (b, p, i, 0)),
            pl.BlockSpec((1, 2, T, LANES), lambda b, p, i: (b, p, 0, 0)),
            pl.BlockSpec((1, 2, T, LANES), lambda b, p, i: (b, p, 0, 0)),
        ],
        out_specs=pl.BlockSpec((1, 1, tq, LANES), lambda b, p, i: (b, p, i, 0)),
        out_shape=jax.ShapeDtypeStruct((B, B_PAIRS, T, LANES), BF16),
        compiler_params=_cparams(("parallel", "parallel", "arbitrary")),
        name="attn_b",
    )(q, k, v)


def _final_kernel(o_ref, z_ref, h1_ref, w_ref, g_ref, out_ref):
    y = jnp.dot(_gated(o_ref, z_ref), w_ref[...], preferred_element_type=F32)
    out_ref[0] = h1_ref[0] + _rms(y, g_ref[...])


def _final(o_b, z_b, h1, w, gain, tm=512):
    B, T, D = h1.shape
    tok = lambda w_: pl.BlockSpec((1, tm, w_), lambda b, i: (b, i, 0))
    return pl.pallas_call(
        _final_kernel,
        grid=(B, T // tm),
        in_specs=[pl.BlockSpec((1, B_PAIRS, tm, LANES), lambda b, i: (b, 0, i, 0)),
                  tok(B_WIDTH), tok(D),
                  pl.BlockSpec(w.shape, lambda b, i: (0, 0)),
                  pl.BlockSpec((1, D), lambda b, i: (0, 0))],
        out_specs=tok(D),
        out_shape=jax.ShapeDtypeStruct((B, T, D), F32),
        compiler_params=_cparams(("parallel", "parallel")),
        name="final",
    )(o_b, z_b, h1, w, gain.reshape(1, D))


def kernel(x, positions, a_pre_norm, a_w_in, a_w_out, a_post_norm, kv_norm, kv_w_down, kv_latent_norm, kv_w_up, b_pre_norm, b_w_in, b_q_norm, b_w_q_up, b_w_out, b_post_norm):
    D = x.shape[-1]
    cos_a, sin_a, cos_b, sin_b = _rope_tables(positions)

    w_in = a_w_in[0].astype(BF16)
    gw = 3 * A_WIDTH
    qkv = []
    z_a = None
    for g, dil in enumerate(A_DILATIONS):
        w = w_in[:, g * gw:(g + 1) * gw]
        if g == 0:
            w = jnp.concatenate([w, w_in[:, 3 * gw:]], axis=1)
            qkv_g, z_a = _proj_a(x, cos_a, sin_a, a_pre_norm[0], w, dil, True)
        else:
            (qkv_g,) = _proj_a(x, cos_a, sin_a, a_pre_norm[0], w, dil, False)
        qkv.append(qkv_g)
    o_a = _attn_a(qkv)

    row = lambda g: g.reshape(1, -1)
    wd = jnp.zeros((D, B_KV_LORA + LANES), F32)
    wd = wd.at[:, :B_KV_LORA].set(kv_w_down[:, :B_KV_LORA])
    wd = wd.at[:, B_KV_LORA + B_NOPE:B_KV_LORA + B_QK_DIM].set(kv_w_down[:, B_KV_LORA:])
    wup = kv_w_up.reshape(B_KV_LORA, B_HEADS, B_NOPE + B_VDIM)
    wupk = jnp.pad(wup[:, :, :B_NOPE], ((0, 0), (0, 0), (0, LANES - B_NOPE)))
    wupk = wupk.reshape(B_KV_LORA, B_HEADS * LANES)
    wupv = jnp.pad(wup[:, :, B_NOPE:], ((0, 0), (0, 0), (0, LANES - B_VDIM)))
    wupv = wupv.reshape(B_KV_LORA, B_HEADS * LANES)
    wqup = jnp.pad(b_w_q_up[0].reshape(B_Q_LORA, B_HEADS, B_QK_DIM),
                   ((0, 0), (0, 0), (0, LANES - B_QK_DIM))).reshape(B_Q_LORA, B_HEADS * LANES)
    weights = [
        a_w_out[0].astype(BF16), row(a_post_norm[0]), row(kv_norm), wd.astype(BF16),
        row(kv_latent_norm), wupk.astype(BF16), wupv.astype(BF16),
        row(b_pre_norm[0]), b_w_in[0][:, :B_Q_LORA].astype(BF16),
        b_w_in[0][:, B_Q_LORA:].astype(BF16), row(b_q_norm[0]), wqup.astype(BF16),
    ]
    h1, k_b, v_b, q_b, z_b = _mid(o_a, z_a, x, cos_b, sin_b, weights)

    o_b = _attn_b(q_b, k_b, v_b)
    return _final(o_b, z_b, h1, b_w_out[0].astype(BF16), b_post_norm[0])
```

```python
import functools
import math

import jax
import jax.numpy as jnp
from jax import lax
from jax.experimental import pallas as pl
from jax.experimental.pallas import tpu as pltpu

F32 = jnp.float32
BF16 = jnp.bfloat16

NORM_EPS = 1e-6
LANES = 128

A_WINDOWS = (128, 512, 2048)
A_DILATIONS = (1, 4, 16)
A_HEADS = 8
A_HEAD_DIM = 128
A_WIDTH = A_HEADS * A_HEAD_DIM
A_ROT_DIM = A_HEAD_DIM // 4
A_ROPE_THETA = 500000.0
BAND = 128

B_HEADS = 16
B_NOPE = 64
B_ROPE = 32
B_QK_DIM = B_NOPE + B_ROPE
B_VDIM = 64
B_WIDTH = B_HEADS * B_VDIM
B_Q_LORA = 384
B_KV_LORA = 256
B_ROPE_THETA = 10000.0
B_PAIRS = B_HEADS // 2

VMEM_LIMIT = 56 * 1024 * 1024


def _cparams(sem):
    return pltpu.CompilerParams(dimension_semantics=sem, vmem_limit_bytes=VMEM_LIMIT)


def _rms(x, g):
    ms = jnp.mean(x * x, axis=-1, keepdims=True)
    return x * lax.rsqrt(ms + NORM_EPS) * g


def _rot_half(a, lo, cos, sin):
    lane = lax.broadcasted_iota(jnp.int32, a.shape, 1)
    partner = jnp.where(lane < lo + 16, pltpu.roll(a, LANES - 16, 1), pltpu.roll(a, 16, 1))
    return a * cos + partner * sin


def _rope_table_kernel(pos_ref, freq_ref, sign_ref, ca_ref, sa_ref, cb_ref, sb_ref):
    pos = pos_ref[0].astype(F32)
    ang_a = pos * freq_ref[0:1, :]
    ang_b = pos * freq_ref[1:2, :]
    ca_ref[0] = jnp.cos(ang_a)
    sa_ref[0] = jnp.sin(ang_a) * sign_ref[0:1, :]
    cb_ref[0] = jnp.cos(ang_b)
    sb_ref[0] = jnp.sin(ang_b) * sign_ref[1:2, :]


def _rope_tables(positions):
    B, T = positions.shape
    half = A_ROT_DIM // 2
    inv_a = 1.0 / (A_ROPE_THETA ** (jnp.arange(half, dtype=F32) * (2.0 / A_ROT_DIM)))
    inv_b = 1.0 / (B_ROPE_THETA ** (jnp.arange(B_ROPE // 2, dtype=F32) * (2.0 / B_ROPE)))
    zeros = lambda n: jnp.zeros((n,), F32)
    ones = jnp.ones((half,), F32)
    freq = jnp.stack([
        jnp.concatenate([inv_a, inv_a, zeros(LANES - 2 * half)]),
        jnp.concatenate([zeros(B_NOPE), inv_b, inv_b, zeros(LANES - B_QK_DIM)]),
    ])
    sign = jnp.stack([
        jnp.concatenate([-ones, ones, zeros(LANES - 2 * half)]),
        jnp.concatenate([zeros(B_NOPE), -ones, ones, zeros(LANES - B_QK_DIM)]),
    ])
    tab = jax.ShapeDtypeStruct((B, T, LANES), F32)
    spec = pl.BlockSpec((1, T, LANES), lambda b: (b, 0, 0))
    const = pl.BlockSpec((2, LANES), lambda b: (0, 0))
    return pl.pallas_call(
        _rope_table_kernel,
        grid=(B,),
        in_specs=[pl.BlockSpec((1, T, 1), lambda b: (b, 0, 0)), const, const],
        out_specs=[spec] * 4,
        out_shape=[tab] * 4,
        compiler_params=_cparams(("parallel",)),
        name="rope_tables",
    )(positions.reshape(B, T, 1), freq, sign)


def _proj_a_kernel(x_ref, c_ref, s_ref, g_ref, w_ref, qkv_ref, *z_ref, kr):
    x = x_ref[0]
    D = x.shape[1] // kr
    hs, cs, ss = [], [], []
    for c in range(kr):
        hs.append(_rms(x[:, c * D:(c + 1) * D], g_ref[...]).astype(BF16))
        cs.append(c_ref[0][:, c * LANES:(c + 1) * LANES])
        ss.append(s_ref[0][:, c * LANES:(c + 1) * LANES])
    h = jnp.concatenate(hs, axis=0) if kr > 1 else hs[0]
    cos = jnp.concatenate(cs, axis=0) if kr > 1 else cs[0]
    sin = jnp.concatenate(ss, axis=0) if kr > 1 else ss[0]
    scale = A_HEAD_DIM ** -0.5
    for s in range(3):
        acc = jnp.dot(h, w_ref[:, s * A_WIDTH:(s + 1) * A_WIDTH],
                      preferred_element_type=F32)
        for hd in range(A_HEADS):
            a = acc[:, hd * LANES:(hd + 1) * LANES]
            if s == 0:
                a = _rot_half(a, 0, cos * scale, sin * scale)
            elif s == 1:
                a = _rot_half(a, 0, cos, sin)
            qkv_ref[0, s * A_HEADS + hd] = a.astype(BF16)
    if z_ref:
        z_ref[0][0] = jnp.dot(h, w_ref[:, 3 * A_WIDTH:4 * A_WIDTH],
                              preferred_element_type=F32).astype(BF16)


def _proj_a(x, cos_a, sin_a, gain, w, dil, with_z, tm=512):
    B, T, D = x.shape
    L = T // dil
    tl = min(L, tm)
    kr = tm // tl
    nlb = L // tl
    assert L % tl == 0 and dil % kr == 0 and T % tm == 0
    ncols = w.shape[1]
    xmap = lambda b, i: (b, i % nlb, i // nlb)
    out_shape = [jax.ShapeDtypeStruct((B, 3 * A_HEADS, T, LANES), BF16)]
    out_specs = [pl.BlockSpec((1, 3 * A_HEADS, tm, LANES), lambda b, i: (b, 0, i, 0))]
    if with_z:
        out_shape.append(jax.ShapeDtypeStruct((B, T, A_WIDTH), BF16))
        out_specs.append(pl.BlockSpec((1, tm, A_WIDTH), lambda b, i: (b, i, 0)))
    return pl.pallas_call(
        functools.partial(_proj_a_kernel, kr=kr),
        grid=(B, T // tm),
        in_specs=[
            pl.BlockSpec((1, tl, kr * D), xmap),
            pl.BlockSpec((1, tl, kr * LANES), xmap),
            pl.BlockSpec((1, tl, kr * LANES), xmap),
            pl.BlockSpec((1, D), lambda b, i: (0, 0)),
            pl.BlockSpec((D, ncols), lambda b, i: (0, 0)),
        ],
        out_specs=out_specs,
        out_shape=out_shape,
        compiler_params=_cparams(("parallel", "parallel")),
        name=f"proj_a_d{dil}",
    )(x.reshape(B, L, dil * D), cos_a.reshape(B, L, dil * LANES),
      sin_a.reshape(B, L, dil * LANES), gain.reshape(1, D), w)


def _attn_a_kernel(g0_ref, g1_ref, g2_ref, o_ref, onat, mnat, lnat, *, chunk):
    T = o_ref.shape[2]
    nt = (((1,), (1,)), ((), ()))
    qi = lax.broadcasted_iota(jnp.int32, (BAND, 2 * BAND), 0)
    ki = lax.broadcasted_iota(jnp.int32, (BAND, 2 * BAND), 1)
    band_mask = (ki >= qi) & (ki <= qi + BAND)
    first_mask = (lax.broadcasted_iota(jnp.int32, (BAND, BAND), 1)
                  <= lax.broadcasted_iota(jnp.int32, (BAND, BAND), 0))
    nblk = T // BAND
    for g, (ref, dil) in enumerate(zip((g0_ref, g1_ref, g2_ref), A_DILATIONS)):
        nb = T // dil // BAND
        for c0 in range(0, nblk, chunk):
            js = range(c0, c0 + chunk)
            lo = [(j - 1) * BAND if j % nb else j * BAND for j in js]
            s = [lax.dot_general(ref[0, 0, 0, j * BAND:(j + 1) * BAND, :],
                                 ref[0, 1, 0, l:(j + 1) * BAND, :], nt,
                                 preferred_element_type=F32) for j, l in zip(js, lo)]
            s = [jnp.where(band_mask if j % nb else first_mask, x, -jnp.inf)
                 for j, x in zip(js, s)]
            m = [jnp.max(x, axis=-1, keepdims=True) for x in s]
            p = [jnp.exp(x - y) for x, y in zip(s, m)]
            lsum = [jnp.sum(x, axis=-1, keepdims=True) for x in p]
            o = [jnp.dot(x.astype(BF16), ref[0, 2, 0, l:(j + 1) * BAND, :],
                         preferred_element_type=F32) for x, j, l in zip(p, js, lo)]
            for j, oj, mj, lj in zip(js, o, m, lsum):
                r, n = divmod(j, nb)
                start = r + n * BAND * dil
                idx = pl.ds(start, BAND, stride=dil) if dil > 1 else pl.ds(start, BAND)
                onat[g, idx, :] = oj
                mnat[g, idx, :] = jnp.broadcast_to(mj, (BAND, LANES))
                lnat[g, idx, :] = jnp.broadcast_to(lj, (BAND, LANES))

    rows = 256

    def merge(i, carry):
        sl = pl.ds(pl.multiple_of(i * rows, rows), rows)
        m0, m1, m2 = mnat[0, sl, :], mnat[1, sl, :], mnat[2, sl, :]
        m = jnp.maximum(jnp.maximum(m0, m1), m2)
        e0, e1, e2 = jnp.exp(m0 - m), jnp.exp(m1 - m), jnp.exp(m2 - m)
        num = e0 * onat[0, sl, :] + e1 * onat[1, sl, :] + e2 * onat[2, sl, :]
        den = e0 * lnat[0, sl, :] + e1 * lnat[1, sl, :] + e2 * lnat[2, sl, :]
        o_ref[0, 0, sl, :] = (num / den).astype(BF16)
        return carry

    lax.fori_loop(0, T // rows, merge, 0)


def _attn_a(qkv, chunk=8):
    B, _, T, _ = qkv[0].shape
    views = [a.reshape(B, 3, A_HEADS, T, LANES) for a in qkv]
    spec = pl.BlockSpec((1, 3, 1, T, LANES), lambda b, h: (b, 0, h, 0, 0))
    return pl.pallas_call(
        functools.partial(_attn_a_kernel, chunk=chunk),
        grid=(B, A_HEADS),
        in_specs=[spec] * 3,
        out_specs=pl.BlockSpec((1, 1, T, LANES), lambda b, h: (b, h, 0, 0)),
        out_shape=jax.ShapeDtypeStruct((B, A_HEADS, T, LANES), BF16),
        scratch_shapes=[pltpu.VMEM((3, T, LANES), F32)] * 3,
        compiler_params=_cparams(("parallel", "parallel")),
        name="attn_a",
    )(*views)


def _gated(o_ref, z_ref):
    o = jnp.concatenate([o_ref[0, h] for h in range(o_ref.shape[1])], axis=1)
    z = z_ref[0].astype(F32)
    return (o.astype(F32) * (z * jax.nn.sigmoid(z))).astype(BF16)


def _mid_kernel(o_ref, z_ref, x_ref, cb_ref, sb_ref,
                wout_ref, gpost_ref, gkv_ref, wd_ref, glat_ref, wupk_ref, wupv_ref,
                gpre_ref, wcq_ref, wz_ref, gq_ref, wqup_ref,
                h1_ref, k_ref, v_ref, q_ref, zb_ref):
    y = jnp.dot(_gated(o_ref, z_ref), wout_ref[...], preferred_element_type=F32)
    h1 = x_ref[0] + _rms(y, gpost_ref[...])
    h1_ref[0] = h1
    cos, sin = cb_ref[0], sb_ref[0]

    hn = _rms(h1, gkv_ref[...]).astype(BF16)
    ckr = jnp.dot(hn, wd_ref[...], preferred_element_type=F32)
    c_kv = _rms(ckr[:, :B_KV_LORA], glat_ref[...]).astype(BF16)
    k_rope = _rot_half(ckr[:, B_KV_LORA:], B_NOPE, cos, sin)
    kn = jnp.dot(c_kv, wupk_ref[...], preferred_element_type=F32)
    for h in range(B_HEADS):
        k_ref[0, h] = (kn[:, h * LANES:(h + 1) * LANES] + k_rope).astype(BF16)
    vv = jnp.dot(c_kv, wupv_ref[...], preferred_element_type=F32)
    lane = lax.broadcasted_iota(jnp.int32, k_rope.shape, 1)
    one_col = jnp.where(lane == B_VDIM, 1.0, 0.0)
    for h in range(B_HEADS):
        v_ref[0, h] = (vv[:, h * LANES:(h + 1) * LANES] + one_col).astype(BF16)

    hb = _rms(h1, gpre_ref[...]).astype(BF16)
    c_q = jnp.dot(hb, wcq_ref[...], preferred_element_type=F32)
    c_q = _rms(c_q, gq_ref[...]).astype(BF16)
    qq = jnp.dot(c_q, wqup_ref[...], preferred_element_type=F32)
    scale = B_QK_DIM ** -0.5
    cq, sq = cos * scale, sin * scale
    for h in range(B_HEADS):
        q_ref[0, h] = _rot_half(qq[:, h * LANES:(h + 1) * LANES], B_NOPE, cq, sq).astype(BF16)
    zb_ref[0] = jnp.dot(hb, wz_ref[...], preferred_element_type=F32).astype(BF16)


def _mid(o_a, z_a, x, cos_b, sin_b, weights, tm=256):
    B, T, D = x.shape
    tok = lambda w: pl.BlockSpec((1, tm, w), lambda b, i: (b, i, 0))
    heads = lambda n: pl.BlockSpec((1, n, tm, LANES), lambda b, i: (b, 0, i, 0))
    const = lambda a: pl.BlockSpec(a.shape, lambda b, i: (0, 0))
    return pl.pallas_call(
        _mid_kernel,
        grid=(B, T // tm),
        in_specs=[heads(A_HEADS), tok(A_WIDTH), tok(D), tok(LANES), tok(LANES)]
                 + [const(w) for w in weights],
        out_specs=[tok(D), heads(B_HEADS), heads(B_HEADS), heads(B_HEADS), tok(B_WIDTH)],
        out_shape=[
            jax.ShapeDtypeStruct((B, T, D), F32),
            jax.ShapeDtypeStruct((B, B_HEADS, T, LANES), BF16),
            jax.ShapeDtypeStruct((B, B_HEADS, T, LANES), BF16),
            jax.ShapeDtypeStruct((B, B_HEADS, T, LANES), BF16),
            jax.ShapeDtypeStruct((B, T, B_WIDTH), BF16),
        ],
        compiler_params=_cparams(("parallel", "parallel")),
        name="mid",
    )(o_a, z_a, x, cos_b, sin_b, *weights)


def _attn_b_kernel(q_ref, k_ref, v_ref, o_ref, *, tq):
    T = k_ref.shape[2]
    qi = lax.broadcasted_iota(jnp.int32, (tq, tq), 0)
    ki = lax.broadcasted_iota(jnp.int32, (tq, tq), 1)
    causal = ki <= qi
    lane = lax.broadcasted_iota(jnp.int32, (tq, LANES), 1)
    nt = (((1,), (1,)), ((), ()))
    units = [(ii, hh) for ii in range(T // tq) for hh in range(2)]

    def scores(ii, hh):
        lo = ii * tq
        q = q_ref[0, hh, lo:lo + tq, :]
        s_d = lax.dot_general(q, k_ref[0, hh, lo:lo + tq, :], nt, preferred_element_type=F32)
        s_d = jnp.where(causal, s_d, -jnp.inf)
        m = jnp.max(s_d, axis=-1, keepdims=True)
        s_m = None
        if ii:
            s_m = lax.dot_general(q, k_ref[0, hh, 0:lo, :], nt, preferred_element_type=F32)
            m = jnp.maximum(m, jnp.max(s_m, axis=-1, keepdims=True))
        return s_d, s_m, m

    def values(ii, hh, s_d, s_m, m):
        lo = ii * tq
        o = jnp.dot(jnp.exp(s_d - m).astype(BF16), v_ref[0, hh, lo:lo + tq, :],
                    preferred_element_type=F32)
        if ii:
            o = o + jnp.dot(jnp.exp(s_m - m).astype(BF16), v_ref[0, hh, 0:lo, :],
                            preferred_element_type=F32)
        l = jnp.sum(jnp.where(lane == B_VDIM, o, 0.0), axis=-1, keepdims=True)
        return o / l

    nxt = scores(*units[0])
    outs = []
    for u, (ii, hh) in enumerate(units):
        cur = nxt
        if u + 1 < len(units):
            nxt = scores(*units[u + 1])
        outs.append(values(ii, hh, *cur))
        if hh:
            lo = ii * tq
            o_ref[0, 0, lo:lo + tq, :] = jnp.where(
                lane < B_VDIM, outs[0], pltpu.roll(outs[1], B_VDIM, 1)).astype(BF16)
            outs = []


def _attn_b(q, k, v, tq=256):
    B, _, T, _ = q.shape
    heads = pl.BlockSpec((1, 2, T, LANES), lambda b, p: (b, p, 0, 0))
    return pl.pallas_call(
        functools.partial(_attn_b_kernel, tq=tq),
        grid=(B, B_PAIRS),
        in_specs=[heads, heads, heads],
        out_specs=pl.BlockSpec((1, 1, T, LANES), lambda b, p: (b, p, 0, 0)),
        out_shape=jax.ShapeDtypeStruct((B, B_PAIRS, T, LANES), BF16),
        compiler_params=_cparams(("parallel", "parallel")),
        name="attn_b",
    )(q, k, v)


def _final_kernel(o_ref, z_ref, h1_ref, w_ref, g_ref, out_ref):
    y = jnp.dot(_gated(o_ref, z_ref), w_ref[...], preferred_element_type=F32)
    out_ref[0] = h1_ref[0] + _rms(y, g_ref[...])


def _final(o_b, z_b, h1, w, gain, tm=512):
    B, T, D = h1.shape
    tok = lambda w_: pl.BlockSpec((1, tm, w_), lambda b, i: (b, i, 0))
    return pl.pallas_call(
        _final_kernel,
        grid=(B, T // tm),
        in_specs=[pl.BlockSpec((1, B_PAIRS, tm, LANES), lambda b, i: (b, 0, i, 0)),
                  tok(B_WIDTH), tok(D),
                  pl.BlockSpec(w.shape, lambda b, i: (0, 0)),
                  pl.BlockSpec((1, D), lambda b, i: (0, 0))],
        out_specs=tok(D),
        out_shape=jax.ShapeDtypeStruct((B, T, D), F32),
        compiler_params=_cparams(("parallel", "parallel")),
        name="final",
    )(o_b, z_b, h1, w, gain.reshape(1, D))


def kernel(x, positions, a_pre_norm, a_w_in, a_w_out, a_post_norm, kv_norm, kv_w_down, kv_latent_norm, kv_w_up, b_pre_norm, b_w_in, b_q_norm, b_w_q_up, b_w_out, b_post_norm):
    D = x.shape[-1]
    cos_a, sin_a, cos_b, sin_b = _rope_tables(positions)

    w_in = a_w_in[0].astype(BF16)
    gw = 3 * A_WIDTH
    qkv = []
    z_a = None
    for g, dil in enumerate(A_DILATIONS):
        w = w_in[:, g * gw:(g + 1) * gw]
        if g == 0:
            w = jnp.concatenate([w, w_in[:, 3 * gw:]], axis=1)
            qkv_g, z_a = _proj_a(x, cos_a, sin_a, a_pre_norm[0], w, dil, True)
        else:
            (qkv_g,) = _proj_a(x, cos_a, sin_a, a_pre_norm[0], w, dil, False)
        qkv.append(qkv_g)
    o_a = _attn_a(qkv)

    row = lambda g: g.reshape(1, -1)
    wd = jnp.zeros((D, B_KV_LORA + LANES), F32)
    wd = wd.at[:, :B_KV_LORA].set(kv_w_down[:, :B_KV_LORA])
    wd = wd.at[:, B_KV_LORA + B_NOPE:B_KV_LORA + B_QK_DIM].set(kv_w_down[:, B_KV_LORA:])
    wup = kv_w_up.reshape(B_KV_LORA, B_HEADS, B_NOPE + B_VDIM)
    wupk = jnp.pad(wup[:, :, :B_NOPE], ((0, 0), (0, 0), (0, LANES - B_NOPE)))
    wupk = wupk.reshape(B_KV_LORA, B_HEADS * LANES)
    wupv = jnp.pad(wup[:, :, B_NOPE:], ((0, 0), (0, 0), (0, LANES - B_VDIM)))
    wupv = wupv.reshape(B_KV_LORA, B_HEADS * LANES)
    wqup = jnp.pad(b_w_q_up[0].reshape(B_Q_LORA, B_HEADS, B_QK_DIM),
                   ((0, 0), (0, 0), (0, LANES - B_QK_DIM))).reshape(B_Q_LORA, B_HEADS * LANES)
    weights = [
        a_w_out[0].astype(BF16), row(a_post_norm[0]), row(kv_norm), wd.astype(BF16),
        row(kv_latent_norm), wupk.astype(BF16), wupv.astype(BF16),
        row(b_pre_norm[0]), b_w_in[0][:, :B_Q_LORA].astype(BF16),
        b_w_in[0][:, B_Q_LORA:].astype(BF16), row(b_q_norm[0]), wqup.astype(BF16),
    ]
    h1, k_b, v_b, q_b, z_b = _mid(o_a, z_a, x, cos_b, sin_b, weights)

    o_b = _attn_b(q_b, k_b, v_b)
    return _final(o_b, z_b, h1, b_w_out[0].astype(BF16), b_post_norm[0])
```

```python
import functools

import jax
import jax.numpy as jnp
from jax import lax
from jax.experimental import pallas as pl
from jax.experimental.pallas import tpu as pltpu

F32 = jnp.float32
BF16 = jnp.bfloat16

NORM_EPS = 1e-6
LANES = 128

A_WINDOWS = (128, 512, 2048)
A_DILATIONS = (1, 4, 16)
A_HEADS = 8
A_HEAD_DIM = 128
A_WIDTH = A_HEADS * A_HEAD_DIM
A_ROT_DIM = A_HEAD_DIM // 4
A_ROPE_THETA = 500000.0
BAND = 128
MAX_ROW_STRIDE = 4

B_HEADS = 16
B_NOPE = 64
B_ROPE = 32
B_QK_DIM = B_NOPE + B_ROPE
B_VDIM = 64
B_WIDTH = B_HEADS * B_VDIM
B_Q_LORA = 384
B_KV_LORA = 256
B_ROPE_THETA = 10000.0
B_PAIRS = B_HEADS // 2

VMEM_LIMIT = 56 * 1024 * 1024


def _cparams(sem):
    return pltpu.CompilerParams(dimension_semantics=sem, vmem_limit_bytes=VMEM_LIMIT)


def _rms(x, g):
    ms = jnp.mean(x * x, axis=-1, keepdims=True)
    return x * lax.rsqrt(ms + NORM_EPS) * g


def _rot_half(a, lo, cos, sin):
    lane = lax.broadcasted_iota(jnp.int32, a.shape, 1)
    partner = jnp.where(lane < lo + 16, pltpu.roll(a, LANES - 16, 1), pltpu.roll(a, 16, 1))
    return a * cos + partner * sin


def _rope_table_kernel(pos_ref, freq_ref, sign_ref, cos_ref, sin_ref):
    ang = pos_ref[0].astype(F32) * freq_ref[...]
    cos_ref[0] = jnp.cos(ang)
    sin_ref[0] = jnp.sin(ang) * sign_ref[...]


def _rope_tables(positions):
    B, T = positions.shape
    half = A_ROT_DIM // 2
    inv_a = 1.0 / (A_ROPE_THETA ** (jnp.arange(half, dtype=F32) * (2.0 / A_ROT_DIM)))
    inv_b = 1.0 / (B_ROPE_THETA ** (jnp.arange(B_ROPE // 2, dtype=F32) * (2.0 / B_ROPE)))
    zeros = lambda n: jnp.zeros((n,), F32)
    ones = jnp.ones((half,), F32)
    freq = jnp.concatenate([inv_a, inv_a, zeros(B_NOPE - A_ROT_DIM), inv_b, inv_b,
                            zeros(LANES - B_QK_DIM)]).reshape(1, LANES)
    sign = jnp.concatenate([-ones, ones, zeros(B_NOPE - A_ROT_DIM), -ones, ones,
                            zeros(LANES - B_QK_DIM)]).reshape(1, LANES)
    tab = jax.ShapeDtypeStruct((B, T, LANES), F32)
    spec = pl.BlockSpec((1, T, LANES), lambda b: (b, 0, 0))
    const = pl.BlockSpec((1, LANES), lambda b: (0, 0))
    return pl.pallas_call(
        _rope_table_kernel,
        grid=(B,),
        in_specs=[pl.BlockSpec((1, T, 1), lambda b: (b, 0, 0)), const, const],
        out_specs=[spec] * 2,
        out_shape=[tab] * 2,
        compiler_params=_cparams(("parallel",)),
        name="rope_tables",
    )(positions.reshape(B, T, 1), freq, sign)


def _proj_a_kernel(x_ref, c_ref, s_ref, g_ref, w_ref, qkv_ref, *rest, dil, with_z):
    tm = x_ref.shape[1]
    tl = tm // dil
    h = _rms(x_ref[0], g_ref[...]).astype(BF16)
    lane = lax.broadcasted_iota(jnp.int32, (tm, LANES), 1)
    cos = jnp.where(lane < A_ROT_DIM, c_ref[0], 1.0)
    sin = jnp.where(lane < A_ROT_DIM, s_ref[0], 0.0)
    scale = A_HEAD_DIM ** -0.5
    for s in range(3):
        acc = jnp.dot(h, w_ref[:, s * A_WIDTH:(s + 1) * A_WIDTH],
                      preferred_element_type=F32)
        for hd in range(A_HEADS):
            a = acc[:, hd * LANES:(hd + 1) * LANES]
            if s == 0:
                a = _rot_half(a, 0, cos * scale, sin * scale)
            elif s == 1:
                a = _rot_half(a, 0, cos, sin)
            if dil == 1:
                qkv_ref[0, s * A_HEADS + hd, 0] = a.astype(BF16)
            else:
                sh = s * A_HEADS + hd
                rows = rest[-1]
                rows[0, sh] = a
                if dil > MAX_ROW_STRIDE:
                    inner = dil // MAX_ROW_STRIDE
                    tq = tm // MAX_ROW_STRIDE
                    for r0 in range(MAX_ROW_STRIDE):
                        rows[1, sh, r0 * tq:(r0 + 1) * tq] = (
                            rows[0, sh, pl.ds(r0, tq, stride=MAX_ROW_STRIDE), :])
                    for r in range(dil):
                        r1, r0 = divmod(r, MAX_ROW_STRIDE)
                        qkv_ref[0, sh, r] = rows[
                            1, sh, pl.ds(r0 * tq + r1, tl, stride=inner), :].astype(BF16)
                else:
                    for r in range(dil):
                        qkv_ref[0, sh, r] = rows[0, sh, pl.ds(r, tl, stride=dil), :].astype(BF16)
    if with_z:
        rest[0][0] = jnp.dot(h, w_ref[:, 3 * A_WIDTH:4 * A_WIDTH],
                             preferred_element_type=F32).astype(BF16)


def _proj_a(x, cos, sin, gain, w, dil, with_z, tm=512):
    B, T, D = x.shape
    L = T // dil
    tl = tm // dil
    ncols = w.shape[1]
    tok = lambda n: pl.BlockSpec((1, tm, n), lambda b, i: (b, i, 0))
    out_shape = [jax.ShapeDtypeStruct((B, 3 * A_HEADS, dil, L, LANES), BF16)]
    out_specs = [pl.BlockSpec((1, 3 * A_HEADS, dil, tl, LANES), lambda b, i: (b, 0, 0, i, 0))]
    if with_z:
        out_shape.append(jax.ShapeDtypeStruct((B, T, A_WIDTH), BF16))
        out_specs.append(tok(A_WIDTH))
    return pl.pallas_call(
        functools.partial(_proj_a_kernel, dil=dil, with_z=with_z),
        grid=(B, T // tm),
        in_specs=[tok(D), tok(LANES), tok(LANES),
                  pl.BlockSpec((1, D), lambda b, i: (0, 0)),
                  pl.BlockSpec((D, ncols), lambda b, i: (0, 0))],
        out_specs=out_specs,
        out_shape=out_shape,
        scratch_shapes=[pltpu.VMEM((2, 3 * A_HEADS, tm, LANES), F32)] if dil > 1 else [],
        compiler_params=_cparams(("parallel", "parallel")),
        name=f"proj_a_d{dil}",
    )(x, cos, sin, gain.reshape(1, D), w)


def _attn_a_kernel(g0_ref, g1_ref, g2_ref, o_ref, onat, mnat, lnat, *, chunk):
    T = o_ref.shape[2]
    nt = (((1,), (1,)), ((), ()))
    qi = lax.broadcasted_iota(jnp.int32, (BAND, 2 * BAND), 0)
    ki = lax.broadcasted_iota(jnp.int32, (BAND, 2 * BAND), 1)
    band_mask = (ki >= qi) & (ki <= qi + BAND)
    first_mask = (lax.broadcasted_iota(jnp.int32, (BAND, BAND), 1)
                  <= lax.broadcasted_iota(jnp.int32, (BAND, BAND), 0))
    nblk = T // BAND
    for g, (ref, dil) in enumerate(zip((g0_ref, g1_ref, g2_ref), A_DILATIONS)):
        nb = T // dil // BAND
        for c0 in range(0, nblk, chunk):
            js = range(c0, c0 + chunk)
            lo = [(j - 1) * BAND if j % nb else j * BAND for j in js]
            s = [lax.dot_general(ref[0, 0, 0, j * BAND:(j + 1) * BAND, :],
                                 ref[0, 1, 0, l:(j + 1) * BAND, :], nt,
                                 preferred_element_type=F32) for j, l in zip(js, lo)]
            s = [jnp.where(band_mask if j % nb else first_mask, x, -jnp.inf)
                 for j, x in zip(js, s)]
            m = [jnp.max(x, axis=-1, keepdims=True) for x in s]
            p = [jnp.exp(x - y) for x, y in zip(s, m)]
            lsum = [jnp.sum(x, axis=-1, keepdims=True) for x in p]
            o = [jnp.dot(x.astype(BF16), ref[0, 2, 0, l:(j + 1) * BAND, :],
                         preferred_element_type=F32) for x, j, l in zip(p, js, lo)]
            for j, oj, mj, lj in zip(js, o, m, lsum):
                r, n = divmod(j, nb)
                start = r + n * BAND * dil
                idx = pl.ds(start, BAND, stride=dil) if dil > 1 else pl.ds(start, BAND)
                onat[g, idx, :] = oj
                mnat[g, idx, :] = jnp.broadcast_to(mj, (BAND, LANES))
                lnat[g, idx, :] = jnp.broadcast_to(lj, (BAND, LANES))

    rows = 256

    def merge(i, carry):
        sl = pl.ds(pl.multiple_of(i * rows, rows), rows)
        m0, m1, m2 = mnat[0, sl, :], mnat[1, sl, :], mnat[2, sl, :]
        m = jnp.maximum(jnp.maximum(m0, m1), m2)
        e0, e1, e2 = jnp.exp(m0 - m), jnp.exp(m1 - m), jnp.exp(m2 - m)
        num = e0 * onat[0, sl, :] + e1 * onat[1, sl, :] + e2 * onat[2, sl, :]
        den = e0 * lnat[0, sl, :] + e1 * lnat[1, sl, :] + e2 * lnat[2, sl, :]
        o_ref[0, 0, sl, :] = (num / den).astype(BF16)
        return carry

    lax.fori_loop(0, T // rows, merge, 0)


def _attn_a(qkv, chunk=8):
    B, T = qkv[0].shape[0], qkv[0].shape[2] * qkv[0].shape[3]
    views = [a.reshape(B, 3, A_HEADS, T, LANES) for a in qkv]
    spec = pl.BlockSpec((1, 3, 1, T, LANES), lambda b, h: (b, 0, h, 0, 0))
    return pl.pallas_call(
        functools.partial(_attn_a_kernel, chunk=chunk),
        grid=(B, A_HEADS),
        in_specs=[spec] * 3,
        out_specs=pl.BlockSpec((1, 1, T, LANES), lambda b, h: (b, h, 0, 0)),
        out_shape=jax.ShapeDtypeStruct((B, A_HEADS, T, LANES), BF16),
        scratch_shapes=[pltpu.VMEM((3, T, LANES), F32)] * 3,
        compiler_params=_cparams(("parallel", "parallel")),
        name="attn_a",
    )(*views)


def _gated(o_ref, z_ref):
    o = jnp.concatenate([o_ref[0, h] for h in range(o_ref.shape[1])], axis=1)
    z = z_ref[0].astype(F32)
    return (o.astype(F32) * (z * jax.nn.sigmoid(z))).astype(BF16)


def _mid_kernel(o_ref, z_ref, x_ref, cb_ref, sb_ref,
                wout_ref, gpost_ref, gkv_ref, wd_ref, glat_ref, wup_ref,
                gpre_ref, wcq_ref, wz_ref, gq_ref, wqup_ref,
                h1_ref, k_ref, v_ref, q_ref, zb_ref):
    y = jnp.dot(_gated(o_ref, z_ref), wout_ref[...], preferred_element_type=F32)
    h1 = x_ref[0] + _rms(y, gpost_ref[...])
    h1_ref[0] = h1
    lane = lax.broadcasted_iota(jnp.int32, cb_ref.shape[1:], 1)
    cos = jnp.where(lane >= B_NOPE, cb_ref[0], 1.0)
    sin = jnp.where(lane >= B_NOPE, sb_ref[0], 0.0)

    hn = _rms(h1, gkv_ref[...]).astype(BF16)
    ckr = jnp.dot(hn, wd_ref[...], preferred_element_type=F32)
    c_kv = _rms(ckr[:, :B_KV_LORA], glat_ref[...]).astype(BF16)
    k_rope = _rot_half(ckr[:, B_KV_LORA:], B_NOPE, cos, sin)
    kv = jnp.dot(c_kv, wup_ref[...], preferred_element_type=F32)
    one_col = jnp.where(lane == 0, 1.0, 0.0)
    for h in range(B_HEADS):
        blk = kv[:, h * LANES:(h + 1) * LANES]
        k_ref[0, h] = jnp.where(lane < B_NOPE, blk, k_rope).astype(BF16)
        v_ref[0, h] = jnp.where(lane >= B_NOPE, blk, one_col).astype(BF16)

    hb = _rms(h1, gpre_ref[...]).astype(BF16)
    c_q = jnp.dot(hb, wcq_ref[...], preferred_element_type=F32)
    c_q = _rms(c_q, gq_ref[...]).astype(BF16)
    qq = jnp.dot(c_q, wqup_ref[...], preferred_element_type=F32)
    scale = B_QK_DIM ** -0.5
    cq, sq = cos * scale, sin * scale
    for h in range(B_HEADS):
        q_ref[0, h] = _rot_half(qq[:, h * LANES:(h + 1) * LANES], B_NOPE, cq, sq).astype(BF16)
    zb_ref[0] = jnp.dot(hb, wz_ref[...], preferred_element_type=F32).astype(BF16)


def _mid(o_a, z_a, x, cos, sin, weights, tm=256):
    B, T, D = x.shape
    tok = lambda w: pl.BlockSpec((1, tm, w), lambda b, i: (b, i, 0))
    heads = lambda n: pl.BlockSpec((1, n, tm, LANES), lambda b, i: (b, 0, i, 0))
    const = lambda a: pl.BlockSpec(a.shape, lambda b, i: (0, 0))
    return pl.pallas_call(
        _mid_kernel,
        grid=(B, T // tm),
        in_specs=[heads(A_HEADS), tok(A_WIDTH), tok(D), tok(LANES), tok(LANES)]
                 + [const(w) for w in weights],
        out_specs=[tok(D), heads(B_HEADS), heads(B_HEADS), heads(B_HEADS), tok(B_WIDTH)],
        out_shape=[
            jax.ShapeDtypeStruct((B, T, D), F32),
            jax.ShapeDtypeStruct((B, B_HEADS, T, LANES), BF16),
            jax.ShapeDtypeStruct((B, B_HEADS, T, LANES), BF16),
            jax.ShapeDtypeStruct((B, B_HEADS, T, LANES), BF16),
            jax.ShapeDtypeStruct((B, T, B_WIDTH), BF16),
        ],
        compiler_params=_cparams(("parallel", "parallel")),
        name="mid",
    )(o_a, z_a, x, cos, sin, *weights)


def _attn_b_kernel(q_ref, k_ref, v_ref, o_ref, *, tq):
    T = k_ref.shape[2]
    qi = lax.broadcasted_iota(jnp.int32, (tq, tq), 0)
    ki = lax.broadcasted_iota(jnp.int32, (tq, tq), 1)
    causal = ki <= qi
    lane = lax.broadcasted_iota(jnp.int32, (tq, LANES), 1)
    nt = (((1,), (1,)), ((), ()))
    units = [(ii, hh) for ii in range(T // tq) for hh in range(2)]

    def scores(ii, hh):
        lo = ii * tq
        q = q_ref[0, hh, lo:lo + tq, :]
        s_d = lax.dot_general(q, k_ref[0, hh, lo:lo + tq, :], nt, preferred_element_type=F32)
        s_d = jnp.where(causal, s_d, -jnp.inf)
        m = jnp.max(s_d, axis=-1, keepdims=True)
        s_m = None
        if ii:
            s_m = lax.dot_general(q, k_ref[0, hh, 0:lo, :], nt, preferred_element_type=F32)
            m = jnp.maximum(m, jnp.max(s_m, axis=-1, keepdims=True))
        return s_d, s_m, m

    def values(ii, hh, s_d, s_m, m):
        lo = ii * tq
        o = jnp.dot(jnp.exp(s_d - m).astype(BF16), v_ref[0, hh, lo:lo + tq, :],
                    preferred_element_type=F32)
        if ii:
            o = o + jnp.dot(jnp.exp(s_m - m).astype(BF16), v_ref[0, hh, 0:lo, :],
                            preferred_element_type=F32)
        l = jnp.sum(jnp.where(lane == 0, o, 0.0), axis=-1, keepdims=True)
        return o / l

    nxt = scores(*units[0])
    outs = []
    for u, (ii, hh) in enumerate(units):
        cur = nxt
        if u + 1 < len(units):
            nxt = scores(*units[u + 1])
        outs.append(values(ii, hh, *cur))
        if hh:
            lo = ii * tq
            o_ref[0, 0, lo:lo + tq, :] = jnp.where(
                lane < B_VDIM, pltpu.roll(outs[0], B_VDIM, 1), outs[1]).astype(BF16)
            outs = []


def _attn_b(q, k, v, tq=256):
    B, _, T, _ = q.shape
    heads = pl.BlockSpec((1, 2, T, LANES), lambda b, p: (b, p, 0, 0))
    return pl.pallas_call(
        functools.partial(_attn_b_kernel, tq=tq),
        grid=(B, B_PAIRS),
        in_specs=[heads, heads, heads],
        out_specs=pl.BlockSpec((1, 1, T, LANES), lambda b, p: (b, p, 0, 0)),
        out_shape=jax.ShapeDtypeStruct((B, B_PAIRS, T, LANES), BF16),
        compiler_params=_cparams(("parallel", "parallel")),
        name="attn_b",
    )(q, k, v)


def _final_kernel(o_ref, z_ref, h1_ref, w_ref, g_ref, out_ref):
    y = jnp.dot(_gated(o_ref, z_ref), w_ref[...], preferred_element_type=F32)
    out_ref[0] = h1_ref[0] + _rms(y, g_ref[...])


def _final(o_b, z_b, h1, w, gain, tm=512):
    B, T, D = h1.shape
    tok = lambda w_: pl.BlockSpec((1, tm, w_), lambda b, i: (b, i, 0))
    return pl.pallas_call(
        _final_kernel,
        grid=(B, T // tm),
        in_specs=[pl.BlockSpec((1, B_PAIRS, tm, LANES), lambda b, i: (b, 0, i, 0)),
                  tok(B_WIDTH), tok(D),
                  pl.BlockSpec(w.shape, lambda b, i: (0, 0)),
                  pl.BlockSpec((1, D), lambda b, i: (0, 0))],
        out_specs=tok(D),
        out_shape=jax.ShapeDtypeStruct((B, T, D), F32),
        compiler_params=_cparams(("parallel", "parallel")),
        name="final",
    )(o_b, z_b, h1, w, gain.reshape(1, D))


def kernel(x, positions, a_pre_norm, a_w_in, a_w_out, a_post_norm, kv_norm, kv_w_down, kv_latent_norm, kv_w_up, b_pre_norm, b_w_in, b_q_norm, b_w_q_up, b_w_out, b_post_norm):
    D = x.shape[-1]
    cos, sin = _rope_tables(positions)

    w_in = a_w_in[0].astype(BF16)
    gw = 3 * A_WIDTH
    qkv = []
    z_a = None
    for g, dil in enumerate(A_DILATIONS):
        w = w_in[:, g * gw:(g + 1) * gw]
        if g == 0:
            w = jnp.concatenate([w, w_in[:, 3 * gw:]], axis=1)
            qkv_g, z_a = _proj_a(x, cos, sin, a_pre_norm[0], w, dil, True)
        else:
            (qkv_g,) = _proj_a(x, cos, sin, a_pre_norm[0], w, dil, False)
        qkv.append(qkv_g)
    o_a = _attn_a(qkv)

    row = lambda g: g.reshape(1, -1)
    wd = jnp.zeros((D, B_KV_LORA + LANES), F32)
    wd = wd.at[:, :B_KV_LORA].set(kv_w_down[:, :B_KV_LORA])
    wd = wd.at[:, B_KV_LORA + B_NOPE:B_KV_LORA + B_QK_DIM].set(kv_w_down[:, B_KV_LORA:])
    wqup = jnp.pad(b_w_q_up[0].reshape(B_Q_LORA, B_HEADS, B_QK_DIM),
                   ((0, 0), (0, 0), (0, LANES - B_QK_DIM))).reshape(B_Q_LORA, B_HEADS * LANES)
    weights = [
        a_w_out[0].astype(BF16), row(a_post_norm[0]), row(kv_norm), wd.astype(BF16),
        row(kv_latent_norm), kv_w_up.astype(BF16),
        row(b_pre_norm[0]), b_w_in[0][:, :B_Q_LORA].astype(BF16),
        b_w_in[0][:, B_Q_LORA:].astype(BF16), row(b_q_norm[0]), wqup.astype(BF16),
    ]
    h1, k_b, v_b, q_b, z_b = _mid(o_a, z_a, x, cos, sin, weights)

    o_b = _attn_b(q_b, k_b, v_b)
    return _final(o_b, z_b, h1, b_w_out[0].astype(BF16), b_post_norm[0])
```

```python
import functools

import jax
import jax.numpy as jnp
from jax import lax
from jax.experimental import pallas as pl
from jax.experimental.pallas import tpu as pltpu

F32 = jnp.float32
BF16 = jnp.bfloat16

NORM_EPS = 1e-6
LANES = 128

A_WINDOWS = (128, 512, 2048)
A_DILATIONS = (1, 4, 16)
A_HEADS = 8
A_HEAD_DIM = 128
A_WIDTH = A_HEADS * A_HEAD_DIM
A_ROT_DIM = A_HEAD_DIM // 4
A_ROPE_THETA = 500000.0
BAND = 128
MAX_ROW_STRIDE = 4

B_HEADS = 16
B_NOPE = 64
B_ROPE = 32
B_QK_DIM = B_NOPE + B_ROPE
B_VDIM = 64
B_WIDTH = B_HEADS * B_VDIM
B_Q_LORA = 384
B_KV_LORA = 256
B_ROPE_THETA = 10000.0
B_PAIRS = B_HEADS // 2

VMEM_LIMIT = 56 * 1024 * 1024


def _cparams(sem):
    return pltpu.CompilerParams(dimension_semantics=sem, vmem_limit_bytes=VMEM_LIMIT)


def _rms(x, g):
    ms = jnp.mean(x * x, axis=-1, keepdims=True)
    return x * lax.rsqrt(ms + NORM_EPS) * g


def _rot_half(a, lo, cos, sin):
    lane = lax.broadcasted_iota(jnp.int32, a.shape, 1)
    partner = jnp.where(lane < lo + 16, pltpu.roll(a, LANES - 16, 1), pltpu.roll(a, 16, 1))
    return a * cos + partner * sin


def _rope_table_kernel(pos_ref, freq_ref, sign_ref, cos_ref, sin_ref):
    ang = pos_ref[0].astype(F32) * freq_ref[...]
    cos_ref[0] = jnp.cos(ang)
    sin_ref[0] = jnp.sin(ang) * sign_ref[...]


def _rope_tables(positions):
    B, T = positions.shape
    half = A_ROT_DIM // 2
    inv_a = 1.0 / (A_ROPE_THETA ** (jnp.arange(half, dtype=F32) * (2.0 / A_ROT_DIM)))
    inv_b = 1.0 / (B_ROPE_THETA ** (jnp.arange(B_ROPE // 2, dtype=F32) * (2.0 / B_ROPE)))
    zeros = lambda n: jnp.zeros((n,), F32)
    ones = jnp.ones((half,), F32)
    freq = jnp.concatenate([inv_a, inv_a, zeros(B_NOPE - A_ROT_DIM), inv_b, inv_b,
                            zeros(LANES - B_QK_DIM)]).reshape(1, LANES)
    sign = jnp.concatenate([-ones, ones, zeros(B_NOPE - A_ROT_DIM), -ones, ones,
                            zeros(LANES - B_QK_DIM)]).reshape(1, LANES)
    tab = jax.ShapeDtypeStruct((B, T, LANES), F32)
    spec = pl.BlockSpec((1, T, LANES), lambda b: (b, 0, 0))
    const = pl.BlockSpec((1, LANES), lambda b: (0, 0))
    return pl.pallas_call(
        _rope_table_kernel,
        grid=(B,),
        in_specs=[pl.BlockSpec((1, T, 1), lambda b: (b, 0, 0)), const, const],
        out_specs=[spec] * 2,
        out_shape=[tab] * 2,
        compiler_params=_cparams(("parallel",)),
        name="rope_tables",
    )(positions.reshape(B, T, 1), freq, sign)


def _proj_a_kernel(x_ref, c_ref, s_ref, g_ref, w_ref, qkv_ref, *rest, dil, with_z):
    tm = x_ref.shape[1]
    tl = tm // dil
    h = _rms(x_ref[0], g_ref[...]).astype(BF16)
    lane = lax.broadcasted_iota(jnp.int32, (tm, LANES), 1)
    cos = jnp.where(lane < A_ROT_DIM, c_ref[0], 1.0)
    sin = jnp.where(lane < A_ROT_DIM, s_ref[0], 0.0)
    scale = A_HEAD_DIM ** -0.5
    for s in range(3):
        acc = jnp.dot(h, w_ref[:, s * A_WIDTH:(s + 1) * A_WIDTH],
                      preferred_element_type=F32)
        for hd in range(A_HEADS):
            a = acc[:, hd * LANES:(hd + 1) * LANES]
            if s == 0:
                a = _rot_half(a, 0, cos * scale, sin * scale)
            elif s == 1:
                a = _rot_half(a, 0, cos, sin)
            if dil == 1:
                qkv_ref[0, s * A_HEADS + hd, 0] = a.astype(BF16)
            else:
                sh = s * A_HEADS + hd
                rows = rest[-1]
                rows[0, sh] = a
                if dil > MAX_ROW_STRIDE:
                    inner = dil // MAX_ROW_STRIDE
                    tq = tm // MAX_ROW_STRIDE
                    for r0 in range(MAX_ROW_STRIDE):
                        rows[1, sh, r0 * tq:(r0 + 1) * tq] = (
                            rows[0, sh, pl.ds(r0, tq, stride=MAX_ROW_STRIDE), :])
                    for r in range(dil):
                        r1, r0 = divmod(r, MAX_ROW_STRIDE)
                        qkv_ref[0, sh, r] = rows[
                            1, sh, pl.ds(r0 * tq + r1, tl, stride=inner), :].astype(BF16)
                else:
                    for r in range(dil):
                        qkv_ref[0, sh, r] = rows[0, sh, pl.ds(r, tl, stride=dil), :].astype(BF16)
    if with_z:
        rest[0][0] = jnp.dot(h, w_ref[:, 3 * A_WIDTH:4 * A_WIDTH],
                             preferred_element_type=F32).astype(BF16)


def _proj_a(x, cos, sin, gain, w, dil, with_z, tm=512):
    B, T, D = x.shape
    L = T // dil
    tl = tm // dil
    ncols = w.shape[1]
    tok = lambda n: pl.BlockSpec((1, tm, n), lambda b, i: (b, i, 0))
    out_shape = [jax.ShapeDtypeStruct((B, 3 * A_HEADS, dil, L, LANES), BF16)]
    out_specs = [pl.BlockSpec((1, 3 * A_HEADS, dil, tl, LANES), lambda b, i: (b, 0, 0, i, 0))]
    if with_z:
        out_shape.append(jax.ShapeDtypeStruct((B, T, A_WIDTH), BF16))
        out_specs.append(tok(A_WIDTH))
    return pl.pallas_call(
        functools.partial(_proj_a_kernel, dil=dil, with_z=with_z),
        grid=(B, T // tm),
        in_specs=[tok(D), tok(LANES), tok(LANES),
                  pl.BlockSpec((1, D), lambda b, i: (0, 0)),
                  pl.BlockSpec((D, ncols), lambda b, i: (0, 0))],
        out_specs=out_specs,
        out_shape=out_shape,
        scratch_shapes=[pltpu.VMEM((2, 3 * A_HEADS, tm, LANES), F32)] if dil > 1 else [],
        compiler_params=_cparams(("parallel", "parallel")),
        name=f"proj_a_d{dil}",
    )(x, cos, sin, gain.reshape(1, D), w)


def _attn_a_kernel(g0_ref, g1_ref, g2_ref, o_ref, onat, mnat, lnat, stage, *, chunk):
    T = o_ref.shape[2]
    nt = (((1,), (1,)), ((), ()))
    qi = lax.broadcasted_iota(jnp.int32, (BAND, 2 * BAND), 0)
    ki = lax.broadcasted_iota(jnp.int32, (BAND, 2 * BAND), 1)
    band_mask = (ki >= qi) & (ki <= qi + BAND)
    first_mask = (lax.broadcasted_iota(jnp.int32, (BAND, BAND), 1)
                  <= lax.broadcasted_iota(jnp.int32, (BAND, BAND), 0))
    nblk = T // BAND
    refs = (g0_ref, g1_ref, g2_ref)

    def key_lo(g, j):
        nb = T // A_DILATIONS[g] // BAND
        return (j - 1) * BAND if j % nb else j * BAND

    def scores(g, c0):
        ref, nb = refs[g], T // A_DILATIONS[g] // BAND
        js = range(c0, c0 + chunk)
        s = [lax.dot_general(ref[0, 0, 0, j * BAND:(j + 1) * BAND, :],
                             ref[0, 1, 0, key_lo(g, j):(j + 1) * BAND, :], nt,
                             preferred_element_type=F32) for j in js]
        s = [jnp.where(band_mask if j % nb else first_mask, x, -jnp.inf) for j, x in zip(js, s)]
        m = [jnp.max(x, axis=-1, keepdims=True) for x in s]
        return s, m

    def values(g, c0, s, m):
        ref, dil = refs[g], A_DILATIONS[g]
        nb = T // dil // BAND
        js = range(c0, c0 + chunk)
        p = [jnp.exp(x - y).astype(BF16) for x, y in zip(s, m)]
        o = [jnp.dot(x, jnp.concatenate([ref[0, 2, 0, key_lo(g, j):(j + 1) * BAND, :],
                                         jnp.ones(((j + 1) * BAND - key_lo(g, j), LANES), BF16)],
                                        axis=1),
                     preferred_element_type=F32) for x, j in zip(p, js)]
        for j, oj, mj in zip(js, o, m):
            r, n = divmod(j, nb)
            mj = jnp.broadcast_to(mj, (BAND, LANES))
            oj, lj = oj[:, :LANES], oj[:, LANES:]
            if dil > MAX_ROW_STRIDE:
                inner = dil // MAX_ROW_STRIDE
                r1, r0 = divmod(r, MAX_ROW_STRIDE)
                idx = pl.ds(r1 + n * BAND * inner, BAND, stride=inner)
                stage[0, r0, idx, :] = oj
                stage[1, r0, idx, :] = mj
                stage[2, r0, idx, :] = lj
            else:
                start = r + n * BAND * dil
                idx = pl.ds(start, BAND, stride=dil) if dil > 1 else pl.ds(start, BAND)
                onat[g, idx, :] = oj
                mnat[g, idx, :] = mj
                lnat[g, idx, :] = lj

    work = [(g, c0) for g in range(len(refs)) for c0 in range(0, nblk, chunk)]
    nxt = scores(*work[0])
    for w, (g, c0) in enumerate(work):
        cur = nxt
        if w + 1 < len(work):
            nxt = scores(*work[w + 1])
        values(g, c0, *cur)
    for g, dil in enumerate(A_DILATIONS):
        if dil > MAX_ROW_STRIDE:
            for a, nat in enumerate((onat, mnat, lnat)):
                for r0 in range(MAX_ROW_STRIDE):
                    nat[g, pl.ds(r0, T // MAX_ROW_STRIDE, stride=MAX_ROW_STRIDE), :] = stage[a, r0]

    rows = 256

    def merge(i, carry):
        sl = pl.ds(pl.multiple_of(i * rows, rows), rows)
        m0, m1, m2 = mnat[0, sl, :], mnat[1, sl, :], mnat[2, sl, :]
        m = jnp.maximum(jnp.maximum(m0, m1), m2)
        e0, e1, e2 = jnp.exp(m0 - m), jnp.exp(m1 - m), jnp.exp(m2 - m)
        num = e0 * onat[0, sl, :] + e1 * onat[1, sl, :] + e2 * onat[2, sl, :]
        den = e0 * lnat[0, sl, :] + e1 * lnat[1, sl, :] + e2 * lnat[2, sl, :]
        o_ref[0, 0, sl, :] = (num / den).astype(BF16)
        return carry

    lax.fori_loop(0, T // rows, merge, 0)


def _attn_a(qkv, chunk=8):
    B, T = qkv[0].shape[0], qkv[0].shape[2] * qkv[0].shape[3]
    views = [a.reshape(B, 3, A_HEADS, T, LANES) for a in qkv]
    spec = pl.BlockSpec((1, 3, 1, T, LANES), lambda b, h: (b, 0, h, 0, 0))
    return pl.pallas_call(
        functools.partial(_attn_a_kernel, chunk=chunk),
        grid=(B, A_HEADS),
        in_specs=[spec] * 3,
        out_specs=pl.BlockSpec((1, 1, T, LANES), lambda b, h: (b, h, 0, 0)),
        out_shape=jax.ShapeDtypeStruct((B, A_HEADS, T, LANES), BF16),
        scratch_shapes=[pltpu.VMEM((3, T, LANES), F32)] * 3
                       + [pltpu.VMEM((3, MAX_ROW_STRIDE, T // MAX_ROW_STRIDE, LANES), F32)],
        compiler_params=_cparams(("parallel", "parallel")),
        name="attn_a",
    )(*views)


def _gated(o_ref, z_ref):
    o = jnp.concatenate([o_ref[0, h] for h in range(o_ref.shape[1])], axis=1)
    z = z_ref[0].astype(F32)
    return (o.astype(F32) * (z * jax.nn.sigmoid(z))).astype(BF16)


def _mid_kernel(o_ref, z_ref, x_ref, cb_ref, sb_ref,
                wout_ref, gpost_ref, gkv_ref, wd_ref, glat_ref, wup_ref,
                gpre_ref, wcq_ref, wz_ref, gq_ref, wqup_ref,
                h1_ref, k_ref, v_ref, q_ref, zb_ref):
    y = jnp.dot(_gated(o_ref, z_ref), wout_ref[...], preferred_element_type=F32)
    h1 = x_ref[0] + _rms(y, gpost_ref[...])
    h1_ref[0] = h1
    lane = lax.broadcasted_iota(jnp.int32, cb_ref.shape[1:], 1)
    cos = jnp.where(lane >= B_NOPE, cb_ref[0], 1.0)
    sin = jnp.where(lane >= B_NOPE, sb_ref[0], 0.0)

    hn = _rms(h1, gkv_ref[...]).astype(BF16)
    ckr = jnp.dot(hn, wd_ref[...], preferred_element_type=F32)
    c_kv = _rms(ckr[:, :B_KV_LORA], glat_ref[...]).astype(BF16)
    k_rope = _rot_half(ckr[:, B_KV_LORA:], B_NOPE, cos, sin)
    kv = jnp.dot(c_kv, wup_ref[...], preferred_element_type=F32)
    one_col = jnp.where(lane == 0, 1.0, 0.0)
    for h in range(B_HEADS):
        blk = kv[:, h * LANES:(h + 1) * LANES]
        k_ref[0, h] = jnp.where(lane < B_NOPE, blk, k_rope).astype(BF16)
        v_ref[0, h] = jnp.where(lane >= B_NOPE, blk, one_col).astype(BF16)

    hb = _rms(h1, gpre_ref[...]).astype(BF16)
    c_q = jnp.dot(hb, wcq_ref[...], preferred_element_type=F32)
    c_q = _rms(c_q, gq_ref[...]).astype(BF16)
    qq = jnp.dot(c_q, wqup_ref[...], preferred_element_type=F32)
    scale = B_QK_DIM ** -0.5
    cq, sq = cos * scale, sin * scale
    for h in range(B_HEADS):
        q_ref[0, h] = _rot_half(qq[:, h * LANES:(h + 1) * LANES], B_NOPE, cq, sq).astype(BF16)
    zb_ref[0] = jnp.dot(hb, wz_ref[...], preferred_element_type=F32).astype(BF16)


def _mid(o_a, z_a, x, cos, sin, weights, tm=512):
    B, T, D = x.shape
    tok = lambda w: pl.BlockSpec((1, tm, w), lambda b, i: (b, i, 0))
    heads = lambda n: pl.BlockSpec((1, n, tm, LANES), lambda b, i: (b, 0, i, 0))
    const = lambda a: pl.BlockSpec(a.shape, lambda b, i: (0, 0), pipeline_mode=pl.Buffered(1))
    return pl.pallas_call(
        _mid_kernel,
        grid=(B, T // tm),
        in_specs=[heads(A_HEADS), tok(A_WIDTH), tok(D), tok(LANES), tok(LANES)]
                 + [const(w) for w in weights],
        out_specs=[tok(D), heads(B_HEADS), heads(B_HEADS), heads(B_HEADS), tok(B_WIDTH)],
        out_shape=[
            jax.ShapeDtypeStruct((B, T, D), F32),
            jax.ShapeDtypeStruct((B, B_HEADS, T, LANES), BF16),
            jax.ShapeDtypeStruct((B, B_HEADS, T, LANES), BF16),
            jax.ShapeDtypeStruct((B, B_HEADS, T, LANES), BF16),
            jax.ShapeDtypeStruct((B, T, B_WIDTH), BF16),
        ],
        compiler_params=_cparams(("parallel", "parallel")),
        name="mid",
    )(o_a, z_a, x, cos, sin, *weights)


def _attn_b_kernel(q_ref, k_ref, v_ref, o_ref, *, tq):
    T = k_ref.shape[2]
    qi = lax.broadcasted_iota(jnp.int32, (tq, tq), 0)
    ki = lax.broadcasted_iota(jnp.int32, (tq, tq), 1)
    causal = ki <= qi
    lane = lax.broadcasted_iota(jnp.int32, (tq, LANES), 1)
    nt = (((1,), (1,)), ((), ()))
    units = [(ii, hh) for ii in range(T // tq) for hh in range(2)]

    def scores(ii, hh):
        lo = ii * tq
        q = q_ref[0, hh, lo:lo + tq, :]
        s_d = lax.dot_general(q, k_ref[0, hh, lo:lo + tq, :], nt, preferred_element_type=F32)
        s_d = jnp.where(causal, s_d, -jnp.inf)
        m = jnp.max(s_d, axis=-1, keepdims=True)
        s_m = None
        if ii:
            s_m = lax.dot_general(q, k_ref[0, hh, 0:lo, :], nt, preferred_element_type=F32)
            m = jnp.maximum(m, jnp.max(s_m, axis=-1, keepdims=True))
        return s_d, s_m, m

    def values(ii, hh, s_d, s_m, m):
        lo = ii * tq
        o = jnp.dot(jnp.exp(s_d - m).astype(BF16), v_ref[0, hh, lo:lo + tq, :],
                    preferred_element_type=F32)
        if ii:
            o = o + jnp.dot(jnp.exp(s_m - m).astype(BF16), v_ref[0, hh, 0:lo, :],
                            preferred_element_type=F32)
        l = jnp.sum(jnp.where(lane == 0, o, 0.0), axis=-1, keepdims=True)
        return o / l

    nxt = scores(*units[0])
    outs = []
    for u, (ii, hh) in enumerate(units):
        cur = nxt
        if u + 1 < len(units):
            nxt = scores(*units[u + 1])
        outs.append(values(ii, hh, *cur))
        if hh:
            lo = ii * tq
            o_ref[0, 0, lo:lo + tq, :] = jnp.where(
                lane < B_VDIM, pltpu.roll(outs[0], B_VDIM, 1), outs[1]).astype(BF16)
            outs = []


def _attn_b(q, k, v, tq=256):
    B, _, T, _ = q.shape
    heads = pl.BlockSpec((1, 2, T, LANES), lambda b, p: (b, p, 0, 0))
    return pl.pallas_call(
        functools.partial(_attn_b_kernel, tq=tq),
        grid=(B, B_PAIRS),
        in_specs=[heads, heads, heads],
        out_specs=pl.BlockSpec((1, 1, T, LANES), lambda b, p: (b, p, 0, 0)),
        out_shape=jax.ShapeDtypeStruct((B, B_PAIRS, T, LANES), BF16),
        compiler_params=_cparams(("parallel", "parallel")),
        name="attn_b",
    )(q, k, v)


def _final_kernel(o_ref, z_ref, h1_ref, w_ref, g_ref, out_ref):
    y = jnp.dot(_gated(o_ref, z_ref), w_ref[...], preferred_element_type=F32)
    out_ref[0] = h1_ref[0] + _rms(y, g_ref[...])


def _final(o_b, z_b, h1, w, gain, tm=512):
    B, T, D = h1.shape
    tok = lambda w_: pl.BlockSpec((1, tm, w_), lambda b, i: (b, i, 0))
    return pl.pallas_call(
        _final_kernel,
        grid=(B, T // tm),
        in_specs=[pl.BlockSpec((1, B_PAIRS, tm, LANES), lambda b, i: (b, 0, i, 0)),
                  tok(B_WIDTH), tok(D),
                  pl.BlockSpec(w.shape, lambda b, i: (0, 0)),
                  pl.BlockSpec((1, D), lambda b, i: (0, 0))],
        out_specs=tok(D),
        out_shape=jax.ShapeDtypeStruct((B, T, D), F32),
        compiler_params=_cparams(("parallel", "parallel")),
        name="final",
    )(o_b, z_b, h1, w, gain.reshape(1, D))


def kernel(x, positions, a_pre_norm, a_w_in, a_w_out, a_post_norm, kv_norm, kv_w_down, kv_latent_norm, kv_w_up, b_pre_norm, b_w_in, b_q_norm, b_w_q_up, b_w_out, b_post_norm):
    D = x.shape[-1]
    cos, sin = _rope_tables(positions)

    w_in = a_w_in[0].astype(BF16)
    gw = 3 * A_WIDTH
    qkv = []
    z_a = None
    for g, dil in enumerate(A_DILATIONS):
        w = w_in[:, g * gw:(g + 1) * gw]
        if g == 0:
            w = jnp.concatenate([w, w_in[:, 3 * gw:]], axis=1)
            qkv_g, z_a = _proj_a(x, cos, sin, a_pre_norm[0], w, dil, True)
        else:
            (qkv_g,) = _proj_a(x, cos, sin, a_pre_norm[0], w, dil, False)
        qkv.append(qkv_g)
    o_a = _attn_a(qkv)

    row = lambda g: g.reshape(1, -1)
    wd = jnp.zeros((D, B_KV_LORA + LANES), F32)
    wd = wd.at[:, :B_KV_LORA].set(kv_w_down[:, :B_KV_LORA])
    wd = wd.at[:, B_KV_LORA + B_NOPE:B_KV_LORA + B_QK_DIM].set(kv_w_down[:, B_KV_LORA:])
    wqup = jnp.pad(b_w_q_up[0].reshape(B_Q_LORA, B_HEADS, B_QK_DIM),
                   ((0, 0), (0, 0), (0, LANES - B_QK_DIM))).reshape(B_Q_LORA, B_HEADS * LANES)
    weights = [
        a_w_out[0].astype(BF16), row(a_post_norm[0]), row(kv_norm), wd.astype(BF16),
        row(kv_latent_norm), kv_w_up.astype(BF16),
        row(b_pre_norm[0]), b_w_in[0][:, :B_Q_LORA].astype(BF16),
        b_w_in[0][:, B_Q_LORA:].astype(BF16), row(b_q_norm[0]), wqup.astype(BF16),
    ]
    h1, k_b, v_b, q_b, z_b = _mid(o_a, z_a, x, cos, sin, weights)

    o_b = _attn_b(q_b, k_b, v_b)
    return _final(o_b, z_b, h1, b_w_out[0].astype(BF16), b_post_norm[0])
```

```python
import functools

import jax
import jax.numpy as jnp
from jax import lax
from jax.experimental import pallas as pl
from jax.experimental.pallas import tpu as pltpu

F32 = jnp.float32
BF16 = jnp.bfloat16

NORM_EPS = 1e-6
LOG2E = 1.4426950408889634
LANES = 128

A_WINDOWS = (128, 512, 2048)
A_DILATIONS = (1, 4, 16)
A_HEADS = 8
A_HEAD_DIM = 128
A_WIDTH = A_HEADS * A_HEAD_DIM
A_ROT_DIM = A_HEAD_DIM // 4
A_ROPE_THETA = 500000.0
BAND = 128
MAX_ROW_STRIDE = 4

B_HEADS = 16
B_NOPE = 64
B_ROPE = 32
B_QK_DIM = B_NOPE + B_ROPE
B_VDIM = 64
B_WIDTH = B_HEADS * B_VDIM
B_Q_LORA = 384
B_KV_LORA = 256
B_ROPE_THETA = 10000.0
B_PAIRS = B_HEADS // 2

VMEM_LIMIT = 56 * 1024 * 1024


def _cparams(sem):
    return pltpu.CompilerParams(dimension_semantics=sem, vmem_limit_bytes=VMEM_LIMIT)


def _rms(x, g):
    ms = jnp.mean(x * x, axis=-1, keepdims=True)
    return x * lax.rsqrt(ms + NORM_EPS) * g


def _rot_half(a, lo, cos, sin):
    lane = lax.broadcasted_iota(jnp.int32, a.shape, 1)
    partner = jnp.where(lane < lo + 16, pltpu.roll(a, LANES - 16, 1), pltpu.roll(a, 16, 1))
    return a * cos + partner * sin


def _rope_table_kernel(pos_ref, freq_ref, sign_ref, cos_ref, sin_ref):
    ang = pos_ref[0].astype(F32) * freq_ref[...]
    cos_ref[0] = jnp.cos(ang)
    sin_ref[0] = jnp.sin(ang) * sign_ref[...]


def _rope_tables(positions):
    B, T = positions.shape
    half = A_ROT_DIM // 2
    inv_a = 1.0 / (A_ROPE_THETA ** (jnp.arange(half, dtype=F32) * (2.0 / A_ROT_DIM)))
    inv_b = 1.0 / (B_ROPE_THETA ** (jnp.arange(B_ROPE // 2, dtype=F32) * (2.0 / B_ROPE)))
    zeros = lambda n: jnp.zeros((n,), F32)
    ones = jnp.ones((half,), F32)
    freq = jnp.concatenate([inv_a, inv_a, zeros(B_NOPE - A_ROT_DIM), inv_b, inv_b,
                            zeros(LANES - B_QK_DIM)]).reshape(1, LANES)
    sign = jnp.concatenate([-ones, ones, zeros(B_NOPE - A_ROT_DIM), -ones, ones,
                            zeros(LANES - B_QK_DIM)]).reshape(1, LANES)
    tab = jax.ShapeDtypeStruct((B, T, LANES), F32)
    spec = pl.BlockSpec((1, T, LANES), lambda b: (b, 0, 0))
    const = pl.BlockSpec((1, LANES), lambda b: (0, 0))
    return pl.pallas_call(
        _rope_table_kernel,
        grid=(B,),
        in_specs=[pl.BlockSpec((1, T, 1), lambda b: (b, 0, 0)), const, const],
        out_specs=[spec] * 2,
        out_shape=[tab] * 2,
        compiler_params=_cparams(("parallel",)),
        name="rope_tables",
    )(positions.reshape(B, T, 1), freq, sign)


def _proj_a_kernel(x_ref, c_ref, s_ref, g_ref, w_ref, qkv_ref, *rest, dil, with_z):
    tm = x_ref.shape[1]
    tl = tm // dil
    h = _rms(x_ref[0], g_ref[...]).astype(BF16)
    lane = lax.broadcasted_iota(jnp.int32, (tm, LANES), 1)
    cos = jnp.where(lane < A_ROT_DIM, c_ref[0], 1.0)
    sin = jnp.where(lane < A_ROT_DIM, s_ref[0], 0.0)
    scale = A_HEAD_DIM ** -0.5 * LOG2E
    for s in range(3):
        acc = jnp.dot(h, w_ref[:, s * A_WIDTH:(s + 1) * A_WIDTH],
                      preferred_element_type=F32)
        for hd in range(A_HEADS):
            a = acc[:, hd * LANES:(hd + 1) * LANES]
            if s == 0:
                a = _rot_half(a, 0, cos * scale, sin * scale)
            elif s == 1:
                a = _rot_half(a, 0, cos, sin)
            if dil == 1:
                qkv_ref[0, s * A_HEADS + hd, 0] = a.astype(BF16)
            else:
                sh = s * A_HEADS + hd
                rows = rest[-1]
                rows[0, sh] = a
                if dil > MAX_ROW_STRIDE:
                    inner = dil // MAX_ROW_STRIDE
                    tq = tm // MAX_ROW_STRIDE
                    for r0 in range(MAX_ROW_STRIDE):
                        rows[1, sh, r0 * tq:(r0 + 1) * tq] = (
                            rows[0, sh, pl.ds(r0, tq, stride=MAX_ROW_STRIDE), :])
                    for r in range(dil):
                        r1, r0 = divmod(r, MAX_ROW_STRIDE)
                        qkv_ref[0, sh, r] = rows[
                            1, sh, pl.ds(r0 * tq + r1, tl, stride=inner), :].astype(BF16)
                else:
                    for r in range(dil):
                        qkv_ref[0, sh, r] = rows[0, sh, pl.ds(r, tl, stride=dil), :].astype(BF16)
    if with_z:
        rest[0][0] = jnp.dot(h, w_ref[:, 3 * A_WIDTH:4 * A_WIDTH],
                             preferred_element_type=F32).astype(BF16)


def _proj_a(x, cos, sin, gain, w, dil, with_z, tm=512):
    B, T, D = x.shape
    L = T // dil
    tl = tm // dil
    ncols = w.shape[1]
    tok = lambda n: pl.BlockSpec((1, tm, n), lambda b, i: (b, i, 0))
    out_shape = [jax.ShapeDtypeStruct((B, 3 * A_HEADS, dil, L, LANES), BF16)]
    out_specs = [pl.BlockSpec((1, 3 * A_HEADS, dil, tl, LANES), lambda b, i: (b, 0, 0, i, 0))]
    if with_z:
        out_shape.append(jax.ShapeDtypeStruct((B, T, A_WIDTH), BF16))
        out_specs.append(tok(A_WIDTH))
    return pl.pallas_call(
        functools.partial(_proj_a_kernel, dil=dil, with_z=with_z),
        grid=(B, T // tm),
        in_specs=[tok(D), tok(LANES), tok(LANES),
                  pl.BlockSpec((1, D), lambda b, i: (0, 0)),
                  pl.BlockSpec((D, ncols), lambda b, i: (0, 0))],
        out_specs=out_specs,
        out_shape=out_shape,
        scratch_shapes=[pltpu.VMEM((2, 3 * A_HEADS, tm, LANES), F32)] if dil > 1 else [],
        compiler_params=_cparams(("parallel", "parallel")),
        name=f"proj_a_d{dil}",
    )(x, cos, sin, gain.reshape(1, D), w)


def _attn_a_kernel(g0_ref, g1_ref, g2_ref, o_ref, onat, mnat, lnat, stage, *, chunk):
    T = o_ref.shape[2]
    nt = (((1,), (1,)), ((), ()))
    qi = lax.broadcasted_iota(jnp.int32, (BAND, 2 * BAND), 0)
    ki = lax.broadcasted_iota(jnp.int32, (BAND, 2 * BAND), 1)
    band_mask = (ki >= qi) & (ki <= qi + BAND)
    first_mask = (lax.broadcasted_iota(jnp.int32, (BAND, BAND), 1)
                  <= lax.broadcasted_iota(jnp.int32, (BAND, BAND), 0))
    nblk = T // BAND
    refs = (g0_ref, g1_ref, g2_ref)

    def key_lo(g, j):
        nb = T // A_DILATIONS[g] // BAND
        return (j - 1) * BAND if j % nb else j * BAND

    def scores(g, c0):
        ref, nb = refs[g], T // A_DILATIONS[g] // BAND
        js = range(c0, c0 + chunk)
        s = [lax.dot_general(ref[0, 0, 0, j * BAND:(j + 1) * BAND, :],
                             ref[0, 1, 0, key_lo(g, j):(j + 1) * BAND, :], nt,
                             preferred_element_type=F32) for j in js]
        s = [jnp.where(band_mask if j % nb else first_mask, x, -jnp.inf) for j, x in zip(js, s)]
        m = [jnp.max(x, axis=-1, keepdims=True) for x in s]
        return s, m

    def values(g, c0, s, m):
        ref, dil = refs[g], A_DILATIONS[g]
        nb = T // dil // BAND
        js = range(c0, c0 + chunk)
        p = [jnp.exp2(x - y).astype(BF16) for x, y in zip(s, m)]
        o = [jnp.dot(x, jnp.concatenate([ref[0, 2, 0, key_lo(g, j):(j + 1) * BAND, :],
                                         jnp.ones(((j + 1) * BAND - key_lo(g, j), LANES), BF16)],
                                        axis=1),
                     preferred_element_type=F32) for x, j in zip(p, js)]
        for j, oj, mj in zip(js, o, m):
            r, n = divmod(j, nb)
            mj = jnp.broadcast_to(mj, (BAND, LANES))
            oj, lj = oj[:, :LANES], oj[:, LANES:]
            if dil > MAX_ROW_STRIDE:
                inner = dil // MAX_ROW_STRIDE
                r1, r0 = divmod(r, MAX_ROW_STRIDE)
                idx = pl.ds(r1 + n * BAND * inner, BAND, stride=inner)
                stage[0, r0, idx, :] = oj
                stage[1, r0, idx, :] = mj
                stage[2, r0, idx, :] = lj
            else:
                start = r + n * BAND * dil
                idx = pl.ds(start, BAND, stride=dil) if dil > 1 else pl.ds(start, BAND)
                onat[g, idx, :] = oj
                mnat[g, idx, :] = mj
                lnat[g, idx, :] = lj

    work = [(g, c0) for g in range(len(refs)) for c0 in range(0, nblk, chunk)]
    nxt = scores(*work[0])
    for w, (g, c0) in enumerate(work):
        cur = nxt
        if w + 1 < len(work):
            nxt = scores(*work[w + 1])
        values(g, c0, *cur)
    for g, dil in enumerate(A_DILATIONS):
        if dil > MAX_ROW_STRIDE:
            for a, nat in enumerate((onat, mnat, lnat)):
                for r0 in range(MAX_ROW_STRIDE):
                    nat[g, pl.ds(r0, T // MAX_ROW_STRIDE, stride=MAX_ROW_STRIDE), :] = stage[a, r0]

    rows = 256

    def merge(i, carry):
        sl = pl.ds(pl.multiple_of(i * rows, rows), rows)
        m0, m1, m2 = mnat[0, sl, :], mnat[1, sl, :], mnat[2, sl, :]
        m = jnp.maximum(jnp.maximum(m0, m1), m2)
        e0, e1, e2 = jnp.exp2(m0 - m), jnp.exp2(m1 - m), jnp.exp2(m2 - m)
        num = e0 * onat[0, sl, :] + e1 * onat[1, sl, :] + e2 * onat[2, sl, :]
        den = e0 * lnat[0, sl, :] + e1 * lnat[1, sl, :] + e2 * lnat[2, sl, :]
        o_ref[0, 0, sl, :] = (num / den).astype(BF16)
        return carry

    lax.fori_loop(0, T // rows, merge, 0)


def _attn_a(qkv, chunk=8):
    B, T = qkv[0].shape[0], qkv[0].shape[2] * qkv[0].shape[3]
    views = [a.reshape(B, 3, A_HEADS, T, LANES) for a in qkv]
    spec = pl.BlockSpec((1, 3, 1, T, LANES), lambda b, h: (b, 0, h, 0, 0))
    return pl.pallas_call(
        functools.partial(_attn_a_kernel, chunk=chunk),
        grid=(B, A_HEADS),
        in_specs=[spec] * 3,
        out_specs=pl.BlockSpec((1, 1, T, LANES), lambda b, h: (b, h, 0, 0)),
        out_shape=jax.ShapeDtypeStruct((B, A_HEADS, T, LANES), BF16),
        scratch_shapes=[pltpu.VMEM((3, T, LANES), F32)] * 3
                       + [pltpu.VMEM((3, MAX_ROW_STRIDE, T // MAX_ROW_STRIDE, LANES), F32)],
        compiler_params=_cparams(("parallel", "parallel")),
        name="attn_a",
    )(*views)


def _gated(o_ref, z_ref):
    o = jnp.concatenate([o_ref[0, h] for h in range(o_ref.shape[1])], axis=1)
    z = z_ref[0].astype(F32)
    return (o.astype(F32) * (z * jax.nn.sigmoid(z))).astype(BF16)


def _mid_kernel(o_ref, z_ref, x_ref, cb_ref, sb_ref,
                wout_ref, gpost_ref, gkv_ref, wd_ref, glat_ref, wup_ref,
                gpre_ref, wcq_ref, wz_ref, gq_ref, wqup_ref,
                h1_ref, k_ref, v_ref, q_ref, zb_ref):
    y = jnp.dot(_gated(o_ref, z_ref), wout_ref[...], preferred_element_type=F32)
    h1 = x_ref[0] + _rms(y, gpost_ref[...])
    h1_ref[0] = h1
    lane = lax.broadcasted_iota(jnp.int32, cb_ref.shape[1:], 1)
    cos = jnp.where(lane >= B_NOPE, cb_ref[0], 1.0)
    sin = jnp.where(lane >= B_NOPE, sb_ref[0], 0.0)

    hn = _rms(h1, gkv_ref[...]).astype(BF16)
    ckr = jnp.dot(hn, wd_ref[...], preferred_element_type=F32)
    c_kv = _rms(ckr[:, :B_KV_LORA], glat_ref[...]).astype(BF16)
    k_rope = _rot_half(ckr[:, B_KV_LORA:], B_NOPE, cos, sin)
    kv = jnp.dot(c_kv, wup_ref[...], preferred_element_type=F32)
    one_col = jnp.where(lane == 0, 1.0, 0.0)
    for h in range(B_HEADS):
        blk = kv[:, h * LANES:(h + 1) * LANES]
        k_ref[0, h] = jnp.where(lane < B_NOPE, blk, k_rope).astype(BF16)
        v_ref[0, h] = jnp.where(lane >= B_NOPE, blk, one_col).astype(BF16)

    hb = _rms(h1, gpre_ref[...]).astype(BF16)
    c_q = jnp.dot(hb, wcq_ref[...], preferred_element_type=F32)
    c_q = _rms(c_q, gq_ref[...]).astype(BF16)
    qq = jnp.dot(c_q, wqup_ref[...], preferred_element_type=F32)
    scale = B_QK_DIM ** -0.5 * LOG2E
    cq, sq = cos * scale, sin * scale
    for h in range(B_HEADS):
        q_ref[0, h] = _rot_half(qq[:, h * LANES:(h + 1) * LANES], B_NOPE, cq, sq).astype(BF16)
    zb_ref[0] = jnp.dot(hb, wz_ref[...], preferred_element_type=F32).astype(BF16)


def _mid(o_a, z_a, x, cos, sin, weights, tm=512):
    B, T, D = x.shape
    tok = lambda w: pl.BlockSpec((1, tm, w), lambda b, i: (b, i, 0))
    heads = lambda n: pl.BlockSpec((1, n, tm, LANES), lambda b, i: (b, 0, i, 0))
    const = lambda a: pl.BlockSpec(a.shape, lambda b, i: (0, 0), pipeline_mode=pl.Buffered(1))
    return pl.pallas_call(
        _mid_kernel,
        grid=(B, T // tm),
        in_specs=[heads(A_HEADS), tok(A_WIDTH), tok(D), tok(LANES), tok(LANES)]
                 + [const(w) for w in weights],
        out_specs=[tok(D), heads(B_HEADS), heads(B_HEADS), heads(B_HEADS), tok(B_WIDTH)],
        out_shape=[
            jax.ShapeDtypeStruct((B, T, D), F32),
            jax.ShapeDtypeStruct((B, B_HEADS, T, LANES), BF16),
            jax.ShapeDtypeStruct((B, B_HEADS, T, LANES), BF16),
            jax.ShapeDtypeStruct((B, B_HEADS, T, LANES), BF16),
            jax.ShapeDtypeStruct((B, T, B_WIDTH), BF16),
        ],
        compiler_params=_cparams(("parallel", "parallel")),
        name="mid",
    )(o_a, z_a, x, cos, sin, *weights)


def _attn_b_kernel(q_ref, k_ref, v_ref, o_ref, *, tq, ahead):
    T = k_ref.shape[2]
    qi = lax.broadcasted_iota(jnp.int32, (tq, tq), 0)
    ki = lax.broadcasted_iota(jnp.int32, (tq, tq), 1)
    causal = ki <= qi
    lane = lax.broadcasted_iota(jnp.int32, (tq, LANES), 1)
    nt = (((1,), (1,)), ((), ()))
    units = [(ii, hh) for ii in range(T // tq) for hh in range(2)]

    def scores(ii, hh):
        lo = ii * tq
        q = q_ref[0, hh, lo:lo + tq, :]
        s_d = lax.dot_general(q, k_ref[0, hh, lo:lo + tq, :], nt, preferred_element_type=F32)
        s_d = jnp.where(causal, s_d, -jnp.inf)
        m = jnp.max(s_d, axis=-1, keepdims=True)
        s_m = None
        if ii:
            s_m = lax.dot_general(q, k_ref[0, hh, 0:lo, :], nt, preferred_element_type=F32)
            m = jnp.maximum(m, jnp.max(s_m, axis=-1, keepdims=True))
        return s_d, s_m, m

    def values(ii, hh, s_d, s_m, m):
        lo = ii * tq
        o = jnp.dot(jnp.exp2(s_d - m).astype(BF16), v_ref[0, hh, lo:lo + tq, :],
                    preferred_element_type=F32)
        if ii:
            o = o + jnp.dot(jnp.exp2(s_m - m).astype(BF16), v_ref[0, hh, 0:lo, :],
                            preferred_element_type=F32)
        l = jnp.sum(jnp.where(lane == 0, o, 0.0), axis=-1, keepdims=True)
        return o / l

    pending = [scores(*units[u]) for u in range(ahead)]
    outs = []
    for u, (ii, hh) in enumerate(units):
        cur = pending.pop(0)
        if u + ahead < len(units):
            pending.append(scores(*units[u + ahead]))
        outs.append(values(ii, hh, *cur))
        if hh:
            lo = ii * tq
            o_ref[0, 0, lo:lo + tq, :] = jnp.where(
                lane < B_VDIM, pltpu.roll(outs[0], B_VDIM, 1), outs[1]).astype(BF16)
            outs = []


def _attn_b(q, k, v, tq=256, ahead=3):
    B, _, T, _ = q.shape
    heads = pl.BlockSpec((1, 2, T, LANES), lambda b, p: (b, p, 0, 0))
    return pl.pallas_call(
        functools.partial(_attn_b_kernel, tq=tq, ahead=ahead),
        grid=(B, B_PAIRS),
        in_specs=[heads, heads, heads],
        out_specs=pl.BlockSpec((1, 1, T, LANES), lambda b, p: (b, p, 0, 0)),
        out_shape=jax.ShapeDtypeStruct((B, B_PAIRS, T, LANES), BF16),
        compiler_params=_cparams(("parallel", "parallel")),
        name="attn_b",
    )(q, k, v)


def _final_kernel(o_ref, z_ref, h1_ref, w_ref, g_ref, out_ref):
    y = jnp.dot(_gated(o_ref, z_ref), w_ref[...], preferred_element_type=F32)
    out_ref[0] = h1_ref[0] + _rms(y, g_ref[...])


def _final(o_b, z_b, h1, w, gain, tm=512):
    B, T, D = h1.shape
    tok = lambda w_: pl.BlockSpec((1, tm, w_), lambda b, i: (b, i, 0))
    return pl.pallas_call(
        _final_kernel,
        grid=(B, T // tm),
        in_specs=[pl.BlockSpec((1, B_PAIRS, tm, LANES), lambda b, i: (b, 0, i, 0)),
                  tok(B_WIDTH), tok(D),
                  pl.BlockSpec(w.shape, lambda b, i: (0, 0)),
                  pl.BlockSpec((1, D), lambda b, i: (0, 0))],
        out_specs=tok(D),
        out_shape=jax.ShapeDtypeStruct((B, T, D), F32),
        compiler_params=_cparams(("parallel", "parallel")),
        name="final",
    )(o_b, z_b, h1, w, gain.reshape(1, D))


def kernel(x, positions, a_pre_norm, a_w_in, a_w_out, a_post_norm, kv_norm, kv_w_down, kv_latent_norm, kv_w_up, b_pre_norm, b_w_in, b_q_norm, b_w_q_up, b_w_out, b_post_norm):
    D = x.shape[-1]
    cos, sin = _rope_tables(positions)

    w_in = a_w_in[0].astype(BF16)
    gw = 3 * A_WIDTH
    qkv = []
    z_a = None
    for g, dil in enumerate(A_DILATIONS):
        w = w_in[:, g * gw:(g + 1) * gw]
        if g == 0:
            w = jnp.concatenate([w, w_in[:, 3 * gw:]], axis=1)
            qkv_g, z_a = _proj_a(x, cos, sin, a_pre_norm[0], w, dil, True)
        else:
            (qkv_g,) = _proj_a(x, cos, sin, a_pre_norm[0], w, dil, False)
        qkv.append(qkv_g)
    o_a = _attn_a(qkv)

    row = lambda g: g.reshape(1, -1)
    wd = jnp.zeros((D, B_KV_LORA + LANES), F32)
    wd = wd.at[:, :B_KV_LORA].set(kv_w_down[:, :B_KV_LORA])
    wd = wd.at[:, B_KV_LORA + B_NOPE:B_KV_LORA + B_QK_DIM].set(kv_w_down[:, B_KV_LORA:])
    wqup = jnp.pad(b_w_q_up[0].reshape(B_Q_LORA, B_HEADS, B_QK_DIM),
                   ((0, 0), (0, 0), (0, LANES - B_QK_DIM))).reshape(B_Q_LORA, B_HEADS * LANES)
    weights = [
        a_w_out[0].astype(BF16), row(a_post_norm[0]), row(kv_norm), wd.astype(BF16),
        row(kv_latent_norm), kv_w_up.astype(BF16),
        row(b_pre_norm[0]), b_w_in[0][:, :B_Q_LORA].astype(BF16),
        b_w_in[0][:, B_Q_LORA:].astype(BF16), row(b_q_norm[0]), wqup.astype(BF16),
    ]
    h1, k_b, v_b, q_b, z_b = _mid(o_a, z_a, x, cos, sin, weights)

    o_b = _attn_b(q_b, k_b, v_b)
    return _final(o_b, z_b, h1, b_w_out[0].astype(BF16), b_post_norm[0])
```

```python
import functools

import jax
import jax.numpy as jnp
from jax import lax
from jax.experimental import pallas as pl
from jax.experimental.pallas import tpu as pltpu

F32 = jnp.float32
BF16 = jnp.bfloat16

NORM_EPS = 1e-6
LOG2E = 1.4426950408889634
LANES = 128

A_WINDOWS = (128, 512, 2048)
A_DILATIONS = (1, 4, 16)
A_HEADS = 8
A_HEAD_DIM = 128
A_WIDTH = A_HEADS * A_HEAD_DIM
A_ROT_DIM = A_HEAD_DIM // 4
A_ROPE_THETA = 500000.0
BAND = 128
MAX_ROW_STRIDE = 4

B_HEADS = 16
B_NOPE = 64
B_ROPE = 32
B_QK_DIM = B_NOPE + B_ROPE
B_VDIM = 64
B_WIDTH = B_HEADS * B_VDIM
B_Q_LORA = 384
B_KV_LORA = 256
B_ROPE_THETA = 10000.0
B_PAIRS = B_HEADS // 2

VMEM_LIMIT = 56 * 1024 * 1024


def _cparams(sem):
    return pltpu.CompilerParams(dimension_semantics=sem, vmem_limit_bytes=VMEM_LIMIT)


def _rms(x, g):
    ms = jnp.mean(x * x, axis=-1, keepdims=True)
    return x * lax.rsqrt(ms + NORM_EPS) * g


def _rot_half(a, lo, cos, sin):
    lane = lax.broadcasted_iota(jnp.int32, a.shape, 1)
    partner = jnp.where(lane < lo + 16, pltpu.roll(a, LANES - 16, 1), pltpu.roll(a, 16, 1))
    return a * cos + partner * sin


def _rope_table_kernel(pos_ref, freq_ref, sign_ref, cos_ref, sin_ref):
    ang = pos_ref[0].astype(F32) * freq_ref[...]
    cos_ref[0] = jnp.cos(ang)
    sin_ref[0] = jnp.sin(ang) * sign_ref[...]


def _rope_tables(positions):
    B, T = positions.shape
    half = A_ROT_DIM // 2
    inv_a = 1.0 / (A_ROPE_THETA ** (jnp.arange(half, dtype=F32) * (2.0 / A_ROT_DIM)))
    inv_b = 1.0 / (B_ROPE_THETA ** (jnp.arange(B_ROPE // 2, dtype=F32) * (2.0 / B_ROPE)))
    zeros = lambda n: jnp.zeros((n,), F32)
    ones = jnp.ones((half,), F32)
    freq = jnp.concatenate([inv_a, inv_a, zeros(B_NOPE - A_ROT_DIM), inv_b, inv_b,
                            zeros(LANES - B_QK_DIM)]).reshape(1, LANES)
    sign = jnp.concatenate([-ones, ones, zeros(B_NOPE - A_ROT_DIM), -ones, ones,
                            zeros(LANES - B_QK_DIM)]).reshape(1, LANES)
    tab = jax.ShapeDtypeStruct((B, T, LANES), F32)
    spec = pl.BlockSpec((1, T, LANES), lambda b: (b, 0, 0))
    const = pl.BlockSpec((1, LANES), lambda b: (0, 0))
    return pl.pallas_call(
        _rope_table_kernel,
        grid=(B,),
        in_specs=[pl.BlockSpec((1, T, 1), lambda b: (b, 0, 0)), const, const],
        out_specs=[spec] * 2,
        out_shape=[tab] * 2,
        compiler_params=_cparams(("parallel",)),
        name="rope_tables",
    )(positions.reshape(B, T, 1), freq, sign)


def _proj_a_kernel(x_ref, c_ref, s_ref, g_ref, w_ref, qkv_ref, *rest, dil, with_z):
    tm, D = x_ref.shape[1], x_ref.shape[2]
    tl = tm // dil
    nslab = D // LANES

    def by_class(load, stage):
        if dil <= MAX_ROW_STRIDE:
            return jnp.concatenate([load(pl.ds(r, tl, stride=dil)) for r in range(dil)], axis=0)
        inner = dil // MAX_ROW_STRIDE
        tq = tm // MAX_ROW_STRIDE
        for q0 in range(MAX_ROW_STRIDE):
            stage[q0 * tq:(q0 + 1) * tq, :] = load(pl.ds(q0, tq, stride=MAX_ROW_STRIDE))
        return jnp.concatenate(
            [stage[pl.ds(q0 * tq + q1, tl, stride=inner), :]
             for q1 in range(inner) for q0 in range(MAX_ROW_STRIDE)], axis=0)

    if dil == 1:
        x, cos, sin = x_ref[0], c_ref[0], s_ref[0]
    else:
        slabs, stage = rest[-2], rest[-1]
        for c in range(nslab):
            slabs[c] = x_ref[0, :, c * LANES:(c + 1) * LANES]
        slabs[nslab] = c_ref[0]
        slabs[nslab + 1] = s_ref[0]
        parts = [by_class(lambda idx, c=c: slabs[c, idx, :], stage.at[c]) for c in range(nslab + 2)]
        x, cos, sin = jnp.concatenate(parts[:nslab], axis=1), parts[nslab], parts[nslab + 1]
    h = _rms(x, g_ref[...]).astype(BF16)
    lane = lax.broadcasted_iota(jnp.int32, (tm, LANES), 1)
    cos = jnp.where(lane < A_ROT_DIM, cos, 1.0)
    sin = jnp.where(lane < A_ROT_DIM, sin, 0.0)
    scale = A_HEAD_DIM ** -0.5 * LOG2E
    for s in range(3):
        acc = jnp.dot(h, w_ref[:, s * A_WIDTH:(s + 1) * A_WIDTH],
                      preferred_element_type=F32)
        for hd in range(A_HEADS):
            a = acc[:, hd * LANES:(hd + 1) * LANES]
            if s == 0:
                a = _rot_half(a, 0, cos * scale, sin * scale)
            elif s == 1:
                a = _rot_half(a, 0, cos, sin)
            a = a.astype(BF16)
            for r in range(dil):
                qkv_ref[0, s * A_HEADS + hd, r] = a[r * tl:(r + 1) * tl]
    if with_z:
        rest[0][0] = jnp.dot(h, w_ref[:, 3 * A_WIDTH:4 * A_WIDTH],
                             preferred_element_type=F32).astype(BF16)


def _proj_a(x, cos, sin, gain, w, dil, with_z, tm=512):
    B, T, D = x.shape
    L = T // dil
    tl = tm // dil
    ncols = w.shape[1]
    tok = lambda n: pl.BlockSpec((1, tm, n), lambda b, i: (b, i, 0))
    out_shape = [jax.ShapeDtypeStruct((B, 3 * A_HEADS, dil, L, LANES), BF16)]
    out_specs = [pl.BlockSpec((1, 3 * A_HEADS, dil, tl, LANES), lambda b, i: (b, 0, 0, i, 0))]
    if with_z:
        out_shape.append(jax.ShapeDtypeStruct((B, T, A_WIDTH), BF16))
        out_specs.append(tok(A_WIDTH))
    return pl.pallas_call(
        functools.partial(_proj_a_kernel, dil=dil, with_z=with_z),
        grid=(B, T // tm),
        in_specs=[tok(D), tok(LANES), tok(LANES),
                  pl.BlockSpec((1, D), lambda b, i: (0, 0)),
                  pl.BlockSpec((D, ncols), lambda b, i: (0, 0))],
        out_specs=out_specs,
        out_shape=out_shape,
        scratch_shapes=[pltpu.VMEM((D // LANES + 2, tm, LANES), F32)] * 2 if dil > 1 else [],
        compiler_params=_cparams(("parallel", "parallel")),
        name=f"proj_a_d{dil}",
    )(x, cos, sin, gain.reshape(1, D), w)


def _attn_a_kernel(g0_ref, g1_ref, g2_ref, o_ref, onat, mnat, lnat, stage, *, chunk):
    T = o_ref.shape[2]
    nt = (((1,), (1,)), ((), ()))
    qi = lax.broadcasted_iota(jnp.int32, (BAND, 2 * BAND), 0)
    ki = lax.broadcasted_iota(jnp.int32, (BAND, 2 * BAND), 1)
    band_mask = (ki >= qi) & (ki <= qi + BAND)
    first_mask = (lax.broadcasted_iota(jnp.int32, (BAND, BAND), 1)
                  <= lax.broadcasted_iota(jnp.int32, (BAND, BAND), 0))
    nblk = T // BAND
    refs = (g0_ref, g1_ref, g2_ref)

    def key_lo(g, j):
        nb = T // A_DILATIONS[g] // BAND
        return (j - 1) * BAND if j % nb else j * BAND

    def scores(g, c0):
        ref, nb = refs[g], T // A_DILATIONS[g] // BAND
        js = range(c0, c0 + chunk)
        s = [lax.dot_general(ref[0, 0, 0, j * BAND:(j + 1) * BAND, :],
                             ref[0, 1, 0, key_lo(g, j):(j + 1) * BAND, :], nt,
                             preferred_element_type=F32) for j in js]
        s = [jnp.where(band_mask if j % nb else first_mask, x, -jnp.inf) for j, x in zip(js, s)]
        m = [jnp.max(x, axis=-1, keepdims=True) for x in s]
        return s, m

    def values(g, c0, s, m):
        ref, dil = refs[g], A_DILATIONS[g]
        nb = T // dil // BAND
        js = range(c0, c0 + chunk)
        p = [jnp.exp2(x - y).astype(BF16) for x, y in zip(s, m)]
        o = [jnp.dot(x, jnp.concatenate([ref[0, 2, 0, key_lo(g, j):(j + 1) * BAND, :],
                                         jnp.ones(((j + 1) * BAND - key_lo(g, j), LANES), BF16)],
                                        axis=1),
                     preferred_element_type=F32) for x, j in zip(p, js)]
        for j, oj, mj in zip(js, o, m):
            r, n = divmod(j, nb)
            mj = jnp.broadcast_to(mj, (BAND, LANES))
            oj, lj = oj[:, :LANES], oj[:, LANES:]
            if dil == 1:
                sl = slice(j * BAND, (j + 1) * BAND)
                m1, m2 = mnat[0, sl, :], mnat[1, sl, :]
                mx = jnp.maximum(jnp.maximum(mj, m1), m2)
                e0, e1, e2 = jnp.exp2(mj - mx), jnp.exp2(m1 - mx), jnp.exp2(m2 - mx)
                num = e0 * oj + e1 * onat[0, sl, :] + e2 * onat[1, sl, :]
                den = e0 * lj + e1 * lnat[0, sl, :] + e2 * lnat[1, sl, :]
                o_ref[0, 0, sl, :] = (num / den).astype(BF16)
            elif dil > MAX_ROW_STRIDE:
                inner = dil // MAX_ROW_STRIDE
                r1, r0 = divmod(r, MAX_ROW_STRIDE)
                idx = pl.ds(r1 + n * BAND * inner, BAND, stride=inner)
                stage[0, r0, idx, :] = oj
                stage[1, r0, idx, :] = mj
                stage[2, r0, idx, :] = lj
            else:
                idx = pl.ds(r + n * BAND * dil, BAND, stride=dil)
                onat[g - 1, idx, :] = oj
                mnat[g - 1, idx, :] = mj
                lnat[g - 1, idx, :] = lj

    def second_pass(g):
        for a, nat in enumerate((onat, mnat, lnat)):
            for r0 in range(MAX_ROW_STRIDE):
                nat[g - 1, pl.ds(r0, T // MAX_ROW_STRIDE, stride=MAX_ROW_STRIDE), :] = stage[a, r0]

    order = sorted(range(len(refs)), key=lambda g: -A_DILATIONS[g])
    assert A_DILATIONS[order[-1]] == 1
    work = [(g, c0) for g in order for c0 in range(0, nblk, chunk)]
    nxt = scores(*work[0])
    for w, (g, c0) in enumerate(work):
        cur = nxt
        if w + 1 < len(work):
            nxt = scores(*work[w + 1])
        values(g, c0, *cur)
        last_of_group = w + 1 == len(work) or work[w + 1][0] != g
        if last_of_group and A_DILATIONS[g] > MAX_ROW_STRIDE:
            second_pass(g)


def _attn_a(qkv, chunk=8):
    B, T = qkv[0].shape[0], qkv[0].shape[2] * qkv[0].shape[3]
    views = [a.reshape(B, 3, A_HEADS, T, LANES) for a in qkv]
    spec = pl.BlockSpec((1, 3, 1, T, LANES), lambda b, h: (b, 0, h, 0, 0))
    return pl.pallas_call(
        functools.partial(_attn_a_kernel, chunk=chunk),
        grid=(B, A_HEADS),
        in_specs=[spec] * 3,
        out_specs=pl.BlockSpec((1, 1, T, LANES), lambda b, h: (b, h, 0, 0)),
        out_shape=jax.ShapeDtypeStruct((B, A_HEADS, T, LANES), BF16),
        scratch_shapes=[pltpu.VMEM((len(A_DILATIONS) - 1, T, LANES), F32)] * 3
                       + [pltpu.VMEM((3, MAX_ROW_STRIDE, T // MAX_ROW_STRIDE, LANES), F32)],
        compiler_params=_cparams(("parallel", "parallel")),
        name="attn_a",
    )(*views)


def _gated(o_ref, z_ref):
    o = jnp.concatenate([o_ref[0, h] for h in range(o_ref.shape[1])], axis=1)
    z = z_ref[0].astype(F32)
    return (o.astype(F32) * (z * jax.nn.sigmoid(z))).astype(BF16)


def _mid_kernel(o_ref, z_ref, x_ref, cb_ref, sb_ref,
                wout_ref, gpost_ref, gkv_ref, wd_ref, glat_ref, wup_ref,
                gpre_ref, wcq_ref, wz_ref, gq_ref, wqup_ref,
                h1_ref, k_ref, v_ref, q_ref, zb_ref):
    y = jnp.dot(_gated(o_ref, z_ref), wout_ref[...], preferred_element_type=F32)
    h1 = x_ref[0] + _rms(y, gpost_ref[...])
    h1_ref[0] = h1
    lane = lax.broadcasted_iota(jnp.int32, cb_ref.shape[1:], 1)
    cos = jnp.where(lane >= B_NOPE, cb_ref[0], 1.0)
    sin = jnp.where(lane >= B_NOPE, sb_ref[0], 0.0)

    hn = _rms(h1, gkv_ref[...]).astype(BF16)
    ckr = jnp.dot(hn, wd_ref[...], preferred_element_type=F32)
    c_kv = _rms(ckr[:, :B_KV_LORA], glat_ref[...]).astype(BF16)
    k_rope = _rot_half(ckr[:, B_KV_LORA:], B_NOPE, cos, sin)
    kv = jnp.dot(c_kv, wup_ref[...], preferred_element_type=F32)
    one_col = jnp.where(lane == 0, 1.0, 0.0)
    for h in range(B_HEADS):
        blk = kv[:, h * LANES:(h + 1) * LANES]
        k_ref[0, h] = jnp.where(lane < B_NOPE, blk, k_rope).astype(BF16)
        v_ref[0, h] = jnp.where(lane >= B_NOPE, blk, one_col).astype(BF16)

    hb = _rms(h1, gpre_ref[...]).astype(BF16)
    c_q = jnp.dot(hb, wcq_ref[...], preferred_element_type=F32)
    c_q = _rms(c_q, gq_ref[...]).astype(BF16)
    qq = jnp.dot(c_q, wqup_ref[...], preferred_element_type=F32)
    scale = B_QK_DIM ** -0.5 * LOG2E
    cq, sq = cos * scale, sin * scale
    for h in range(B_HEADS):
        q_ref[0, h] = _rot_half(qq[:, h * LANES:(h + 1) * LANES], B_NOPE, cq, sq).astype(BF16)
    zb_ref[0] = jnp.dot(hb, wz_ref[...], preferred_element_type=F32).astype(BF16)


def _mid(o_a, z_a, x, cos, sin, weights, tm=512):
    B, T, D = x.shape
    tok = lambda w: pl.BlockSpec((1, tm, w), lambda b, i: (b, i, 0))
    heads = lambda n: pl.BlockSpec((1, n, tm, LANES), lambda b, i: (b, 0, i, 0))
    const = lambda a: pl.BlockSpec(a.shape, lambda b, i: (0, 0), pipeline_mode=pl.Buffered(1))
    return pl.pallas_call(
        _mid_kernel,
        grid=(B, T // tm),
        in_specs=[heads(A_HEADS), tok(A_WIDTH), tok(D), tok(LANES), tok(LANES)]
                 + [const(w) for w in weights],
        out_specs=[tok(D), heads(B_HEADS), heads(B_HEADS), heads(B_HEADS), tok(B_WIDTH)],
        out_shape=[
            jax.ShapeDtypeStruct((B, T, D), F32),
            jax.ShapeDtypeStruct((B, B_HEADS, T, LANES), BF16),
            jax.ShapeDtypeStruct((B, B_HEADS, T, LANES), BF16),
            jax.ShapeDtypeStruct((B, B_HEADS, T, LANES), BF16),
            jax.ShapeDtypeStruct((B, T, B_WIDTH), BF16),
        ],
        compiler_params=_cparams(("parallel", "parallel")),
        name="mid",
    )(o_a, z_a, x, cos, sin, *weights)


def _attn_b_kernel(q_ref, k_ref, v_ref, o_ref, *, tq, ahead):
    T = k_ref.shape[2]
    qi = lax.broadcasted_iota(jnp.int32, (tq, tq), 0)
    ki = lax.broadcasted_iota(jnp.int32, (tq, tq), 1)
    causal = ki <= qi
    lane = lax.broadcasted_iota(jnp.int32, (tq, LANES), 1)
    nt = (((1,), (1,)), ((), ()))
    units = [(ii, hh) for ii in range(T // tq) for hh in range(2)]

    def scores(ii, hh):
        lo = ii * tq
        q = q_ref[0, hh, lo:lo + tq, :]
        s_d = lax.dot_general(q, k_ref[0, hh, lo:lo + tq, :], nt, preferred_element_type=F32)
        s_d = jnp.where(causal, s_d, -jnp.inf)
        m = jnp.max(s_d, axis=-1, keepdims=True)
        s_m = None
        if ii:
            s_m = lax.dot_general(q, k_ref[0, hh, 0:lo, :], nt, preferred_element_type=F32)
            m = jnp.maximum(m, jnp.max(s_m, axis=-1, keepdims=True))
        return s_d, s_m, m

    def values(ii, hh, s_d, s_m, m):
        lo = ii * tq
        o = jnp.dot(jnp.exp2(s_d - m).astype(BF16), v_ref[0, hh, lo:lo + tq, :],
                    preferred_element_type=F32)
        if ii:
            o = o + jnp.dot(jnp.exp2(s_m - m).astype(BF16), v_ref[0, hh, 0:lo, :],
                            preferred_element_type=F32)
        l = jnp.sum(jnp.where(lane == 0, o, 0.0), axis=-1, keepdims=True)
        return o / l

    pending = [scores(*units[u]) for u in range(ahead)]
    outs = []
    for u, (ii, hh) in enumerate(units):
        cur = pending.pop(0)
        if u + ahead < len(units):
            pending.append(scores(*units[u + ahead]))
        outs.append(values(ii, hh, *cur))
        if hh:
            lo = ii * tq
            o_ref[0, 0, lo:lo + tq, :] = jnp.where(
                lane < B_VDIM, pltpu.roll(outs[0], B_VDIM, 1), outs[1]).astype(BF16)
            outs = []


def _attn_b(q, k, v, tq=256, ahead=3):
    B, _, T, _ = q.shape
    heads = pl.BlockSpec((1, 2, T, LANES), lambda b, p: (b, p, 0, 0))
    return pl.pallas_call(
        functools.partial(_attn_b_kernel, tq=tq, ahead=ahead),
        grid=(B, B_PAIRS),
        in_specs=[heads, heads, heads],
        out_specs=pl.BlockSpec((1, 1, T, LANES), lambda b, p: (b, p, 0, 0)),
        out_shape=jax.ShapeDtypeStruct((B, B_PAIRS, T, LANES), BF16),
        compiler_params=_cparams(("parallel", "parallel")),
        name="attn_b",
    )(q, k, v)


def _final_kernel(o_ref, z_ref, h1_ref, w_ref, g_ref, out_ref):
    y = jnp.dot(_gated(o_ref, z_ref), w_ref[...], preferred_element_type=F32)
    out_ref[0] = h1_ref[0] + _rms(y, g_ref[...])


def _final(o_b, z_b, h1, w, gain, tm=512):
    B, T, D = h1.shape
    tok = lambda w_: pl.BlockSpec((1, tm, w_), lambda b, i: (b, i, 0))
    return pl.pallas_call(
        _final_kernel,
        grid=(B, T // tm),
        in_specs=[pl.BlockSpec((1, B_PAIRS, tm, LANES), lambda b, i: (b, 0, i, 0)),
                  tok(B_WIDTH), tok(D),
                  pl.BlockSpec(w.shape, lambda b, i: (0, 0)),
                  pl.BlockSpec((1, D), lambda b, i: (0, 0))],
        out_specs=tok(D),
        out_shape=jax.ShapeDtypeStruct((B, T, D), F32),
        compiler_params=_cparams(("parallel", "parallel")),
        name="final",
    )(o_b, z_b, h1, w, gain.reshape(1, D))


def kernel(x, positions, a_pre_norm, a_w_in, a_w_out, a_post_norm, kv_norm, kv_w_down, kv_latent_norm, kv_w_up, b_pre_norm, b_w_in, b_q_norm, b_w_q_up, b_w_out, b_post_norm):
    D = x.shape[-1]
    cos, sin = _rope_tables(positions)

    w_in = a_w_in[0].astype(BF16)
    gw = 3 * A_WIDTH
    qkv = []
    z_a = None
    for g, dil in enumerate(A_DILATIONS):
        w = w_in[:, g * gw:(g + 1) * gw]
        if g == 0:
            w = jnp.concatenate([w, w_in[:, 3 * gw:]], axis=1)
            qkv_g, z_a = _proj_a(x, cos, sin, a_pre_norm[0], w, dil, True)
        else:
            (qkv_g,) = _proj_a(x, cos, sin, a_pre_norm[0], w, dil, False)
        qkv.append(qkv_g)
    o_a = _attn_a(qkv)

    row = lambda g: g.reshape(1, -1)
    wd = jnp.zeros((D, B_KV_LORA + LANES), F32)
    wd = wd.at[:, :B_KV_LORA].set(kv_w_down[:, :B_KV_LORA])
    wd = wd.at[:, B_KV_LORA + B_NOPE:B_KV_LORA + B_QK_DIM].set(kv_w_down[:, B_KV_LORA:])
    wqup = jnp.pad(b_w_q_up[0].reshape(B_Q_LORA, B_HEADS, B_QK_DIM),
                   ((0, 0), (0, 0), (0, LANES - B_QK_DIM))).reshape(B_Q_LORA, B_HEADS * LANES)
    weights = [
        a_w_out[0].astype(BF16), row(a_post_norm[0]), row(kv_norm), wd.astype(BF16),
        row(kv_latent_norm), kv_w_up.astype(BF16),
        row(b_pre_norm[0]), b_w_in[0][:, :B_Q_LORA].astype(BF16),
        b_w_in[0][:, B_Q_LORA:].astype(BF16), row(b_q_norm[0]), wqup.astype(BF16),
    ]
    h1, k_b, v_b, q_b, z_b = _mid(o_a, z_a, x, cos, sin, weights)

    o_b = _attn_b(q_b, k_b, v_b)
    return _final(o_b, z_b, h1, b_w_out[0].astype(BF16), b_post_norm[0])
```

```python
import functools

import jax
import jax.numpy as jnp
from jax import lax
from jax.experimental import pallas as pl
from jax.experimental.pallas import tpu as pltpu

F32 = jnp.float32
BF16 = jnp.bfloat16

NORM_EPS = 1e-6
LOG2E = 1.4426950408889634
LANES = 128

A_WINDOWS = (128, 512, 2048)
A_DILATIONS = (1, 4, 16)
A_HEADS = 8
A_HEAD_DIM = 128
A_WIDTH = A_HEADS * A_HEAD_DIM
A_ROT_DIM = A_HEAD_DIM // 4
A_ROPE_THETA = 500000.0
BAND = 128
MAX_ROW_STRIDE = 4

B_HEADS = 16
B_NOPE = 64
B_ROPE = 32
B_QK_DIM = B_NOPE + B_ROPE
B_VDIM = 64
B_WIDTH = B_HEADS * B_VDIM
B_Q_LORA = 384
B_KV_LORA = 256
B_ROPE_THETA = 10000.0
B_PAIRS = B_HEADS // 2

VMEM_LIMIT = 56 * 1024 * 1024


def _cparams(sem):
    return pltpu.CompilerParams(dimension_semantics=sem, vmem_limit_bytes=VMEM_LIMIT)


def _rms(x, g):
    ms = jnp.mean(x * x, axis=-1, keepdims=True)
    return x * lax.rsqrt(ms + NORM_EPS) * g


def _rot_half(a, lo, cos, sin):
    lane = lax.broadcasted_iota(jnp.int32, a.shape, 1)
    partner = jnp.where(lane < lo + 16, pltpu.roll(a, LANES - 16, 1), pltpu.roll(a, 16, 1))
    return a * cos + partner * sin


def _rope_table_kernel(pos_ref, freq_ref, sign_ref, cos_ref, sin_ref):
    ang = pos_ref[0].astype(F32) * freq_ref[...]
    cos_ref[0] = jnp.cos(ang)
    sin_ref[0] = jnp.sin(ang) * sign_ref[...]


def _rope_tables(positions):
    B, T = positions.shape
    half = A_ROT_DIM // 2
    inv_a = 1.0 / (A_ROPE_THETA ** (jnp.arange(half, dtype=F32) * (2.0 / A_ROT_DIM)))
    inv_b = 1.0 / (B_ROPE_THETA ** (jnp.arange(B_ROPE // 2, dtype=F32) * (2.0 / B_ROPE)))
    zeros = lambda n: jnp.zeros((n,), F32)
    ones = jnp.ones((half,), F32)
    freq = jnp.concatenate([inv_a, inv_a, zeros(B_NOPE - A_ROT_DIM), inv_b, inv_b,
                            zeros(LANES - B_QK_DIM)]).reshape(1, LANES)
    sign = jnp.concatenate([-ones, ones, zeros(B_NOPE - A_ROT_DIM), -ones, ones,
                            zeros(LANES - B_QK_DIM)]).reshape(1, LANES)
    tab = jax.ShapeDtypeStruct((B, T, LANES), F32)
    spec = pl.BlockSpec((1, T, LANES), lambda b: (b, 0, 0))
    const = pl.BlockSpec((1, LANES), lambda b: (0, 0))
    return pl.pallas_call(
        _rope_table_kernel,
        grid=(B,),
        in_specs=[pl.BlockSpec((1, T, 1), lambda b: (b, 0, 0)), const, const],
        out_specs=[spec] * 2,
        out_shape=[tab] * 2,
        compiler_params=_cparams(("parallel",)),
        name="rope_tables",
    )(positions.reshape(B, T, 1), freq, sign)


def _proj_a_kernel(x_ref, c_ref, s_ref, g_ref, w_ref, qkv_ref, *rest, dil, with_z):
    tm, D = x_ref.shape[1], x_ref.shape[2]
    tl = tm // dil
    nslab = D // LANES

    def by_class(load, stage):
        if dil <= MAX_ROW_STRIDE:
            return jnp.concatenate([load(pl.ds(r, tl, stride=dil)) for r in range(dil)], axis=0)
        inner = dil // MAX_ROW_STRIDE
        tq = tm // MAX_ROW_STRIDE
        for q0 in range(MAX_ROW_STRIDE):
            stage[q0 * tq:(q0 + 1) * tq, :] = load(pl.ds(q0, tq, stride=MAX_ROW_STRIDE))
        return jnp.concatenate(
            [stage[pl.ds(q0 * tq + q1, tl, stride=inner), :]
             for q1 in range(inner) for q0 in range(MAX_ROW_STRIDE)], axis=0)

    if dil == 1:
        x, cos, sin = x_ref[0], c_ref[0], s_ref[0]
    else:
        slabs, stage = rest[-2], rest[-1]
        for c in range(nslab):
            slabs[c] = x_ref[0, :, c * LANES:(c + 1) * LANES]
        slabs[nslab] = c_ref[0]
        slabs[nslab + 1] = s_ref[0]
        parts = [by_class(lambda idx, c=c: slabs[c, idx, :], stage.at[c]) for c in range(nslab + 2)]
        x, cos, sin = jnp.concatenate(parts[:nslab], axis=1), parts[nslab], parts[nslab + 1]
    h = _rms(x, g_ref[...]).astype(BF16)
    lane = lax.broadcasted_iota(jnp.int32, (tm, LANES), 1)
    cos = jnp.where(lane < A_ROT_DIM, cos, 1.0)
    sin = jnp.where(lane < A_ROT_DIM, sin, 0.0)
    scale = A_HEAD_DIM ** -0.5 * LOG2E
    for s in range(3):
        acc = jnp.dot(h, w_ref[:, s * A_WIDTH:(s + 1) * A_WIDTH],
                      preferred_element_type=F32)
        for hd in range(A_HEADS):
            a = acc[:, hd * LANES:(hd + 1) * LANES]
            if s == 0:
                a = _rot_half(a, 0, cos * scale, sin * scale)
            elif s == 1:
                a = _rot_half(a, 0, cos, sin)
            a = a.astype(BF16)
            for r in range(dil):
                qkv_ref[0, s * A_HEADS + hd, r] = a[r * tl:(r + 1) * tl]
    if with_z:
        rest[0][0] = jnp.dot(h, w_ref[:, 3 * A_WIDTH:4 * A_WIDTH],
                             preferred_element_type=F32).astype(BF16)


def _proj_a(x, cos, sin, gain, w, dil, with_z, tm=512):
    B, T, D = x.shape
    L = T // dil
    tl = tm // dil
    ncols = w.shape[1]
    tok = lambda n: pl.BlockSpec((1, tm, n), lambda b, i: (b, i, 0))
    out_shape = [jax.ShapeDtypeStruct((B, 3 * A_HEADS, dil, L, LANES), BF16)]
    out_specs = [pl.BlockSpec((1, 3 * A_HEADS, dil, tl, LANES), lambda b, i: (b, 0, 0, i, 0))]
    if with_z:
        out_shape.append(jax.ShapeDtypeStruct((B, T, A_WIDTH), BF16))
        out_specs.append(tok(A_WIDTH))
    return pl.pallas_call(
        functools.partial(_proj_a_kernel, dil=dil, with_z=with_z),
        grid=(B, T // tm),
        in_specs=[tok(D), tok(LANES), tok(LANES),
                  pl.BlockSpec((1, D), lambda b, i: (0, 0)),
                  pl.BlockSpec((D, ncols), lambda b, i: (0, 0))],
        out_specs=out_specs,
        out_shape=out_shape,
        scratch_shapes=[pltpu.VMEM((D // LANES + 2, tm, LANES), F32)] * 2 if dil > 1 else [],
        compiler_params=_cparams(("parallel", "parallel")),
        name=f"proj_a_d{dil}",
    )(x, cos, sin, gain.reshape(1, D), w)


def _attn_a_kernel(g0_ref, g1_ref, g2_ref, o_ref, onat, mnat, lnat, stage, *, chunk):
    T = o_ref.shape[2]
    nt = (((1,), (1,)), ((), ()))
    qi = lax.broadcasted_iota(jnp.int32, (BAND, 2 * BAND), 0)
    ki = lax.broadcasted_iota(jnp.int32, (BAND, 2 * BAND), 1)
    band_mask = (ki >= qi) & (ki <= qi + BAND)
    first_mask = (lax.broadcasted_iota(jnp.int32, (BAND, BAND), 1)
                  <= lax.broadcasted_iota(jnp.int32, (BAND, BAND), 0))
    nblk = T // BAND
    refs = (g0_ref, g1_ref, g2_ref)

    def key_lo(g, j):
        nb = T // A_DILATIONS[g] // BAND
        return (j - 1) * BAND if j % nb else j * BAND

    def scores(g, c0):
        ref, nb = refs[g], T // A_DILATIONS[g] // BAND
        js = range(c0, c0 + chunk)
        s = [lax.dot_general(ref[0, 0, 0, j * BAND:(j + 1) * BAND, :],
                             ref[0, 1, 0, key_lo(g, j):(j + 1) * BAND, :], nt,
                             preferred_element_type=F32) for j in js]
        s = [jnp.where(band_mask if j % nb else first_mask, x, -jnp.inf) for j, x in zip(js, s)]
        m = [jnp.max(x, axis=-1, keepdims=True) for x in s]
        return s, m

    def values(g, c0, s, m):
        ref, dil = refs[g], A_DILATIONS[g]
        nb = T // dil // BAND
        js = range(c0, c0 + chunk)
        p = [jnp.exp2(x - y).astype(BF16) for x, y in zip(s, m)]
        o = [jnp.dot(x, jnp.concatenate([ref[0, 2, 0, key_lo(g, j):(j + 1) * BAND, :],
                                         jnp.ones(((j + 1) * BAND - key_lo(g, j), LANES), BF16)],
                                        axis=1),
                     preferred_element_type=F32) for x, j in zip(p, js)]
        for j, oj, mj in zip(js, o, m):
            r, n = divmod(j, nb)
            mj = jnp.broadcast_to(mj, (BAND, LANES))
            oj, lj = oj[:, :LANES], oj[:, LANES:]
            if dil == 1:
                sl = slice(j * BAND, (j + 1) * BAND)
                m1, m2 = mnat[0, sl, :], mnat[1, sl, :]
                mx = jnp.maximum(jnp.maximum(mj, m1), m2)
                e0, e1, e2 = jnp.exp2(mj - mx), jnp.exp2(m1 - mx), jnp.exp2(m2 - mx)
                num = e0 * oj + e1 * onat[0, sl, :] + e2 * onat[1, sl, :]
                den = e0 * lj + e1 * lnat[0, sl, :] + e2 * lnat[1, sl, :]
                o_ref[0, 0, sl, :] = (num / den).astype(BF16)
            elif dil > MAX_ROW_STRIDE:
                inner = dil // MAX_ROW_STRIDE
                r1, r0 = divmod(r, MAX_ROW_STRIDE)
                idx = pl.ds(r1 + n * BAND * inner, BAND, stride=inner)
                stage[0, r0, idx, :] = oj
                stage[1, r0, idx, :] = mj
                stage[2, r0, idx, :] = lj
            else:
                idx = pl.ds(r + n * BAND * dil, BAND, stride=dil)
                onat[g - 1, idx, :] = oj
                mnat[g - 1, idx, :] = mj
                lnat[g - 1, idx, :] = lj

    def second_pass(g):
        for a, nat in enumerate((onat, mnat, lnat)):
            for r0 in range(MAX_ROW_STRIDE):
                nat[g - 1, pl.ds(r0, T // MAX_ROW_STRIDE, stride=MAX_ROW_STRIDE), :] = stage[a, r0]

    order = sorted(range(len(refs)), key=lambda g: -A_DILATIONS[g])
    assert A_DILATIONS[order[-1]] == 1
    work = [(g, c0) for g in order for c0 in range(0, nblk, chunk)]
    nxt = scores(*work[0])
    for w, (g, c0) in enumerate(work):
        cur = nxt
        if w + 1 < len(work):
            nxt = scores(*work[w + 1])
        values(g, c0, *cur)
        last_of_group = w + 1 == len(work) or work[w + 1][0] != g
        if last_of_group and A_DILATIONS[g] > MAX_ROW_STRIDE:
            second_pass(g)


def _attn_a(qkv, chunk=8):
    B, T = qkv[0].shape[0], qkv[0].shape[2] * qkv[0].shape[3]
    views = [a.reshape(B, 3, A_HEADS, T, LANES) for a in qkv]
    spec = pl.BlockSpec((1, 3, 1, T, LANES), lambda b, h: (b, 0, h, 0, 0))
    return pl.pallas_call(
        functools.partial(_attn_a_kernel, chunk=chunk),
        grid=(B, A_HEADS),
        in_specs=[spec] * 3,
        out_specs=pl.BlockSpec((1, 1, T, LANES), lambda b, h: (b, h, 0, 0)),
        out_shape=jax.ShapeDtypeStruct((B, A_HEADS, T, LANES), BF16),
        scratch_shapes=[pltpu.VMEM((len(A_DILATIONS) - 1, T, LANES), F32)] * 3
                       + [pltpu.VMEM((3, MAX_ROW_STRIDE, T // MAX_ROW_STRIDE, LANES), F32)],
        compiler_params=_cparams(("parallel", "parallel")),
        name="attn_a",
    )(*views)


def _gated(o_ref, z_ref):
    o = jnp.concatenate([o_ref[0, h] for h in range(o_ref.shape[1])], axis=1)
    z = z_ref[0].astype(F32)
    return (o.astype(F32) * (z * jax.nn.sigmoid(z))).astype(BF16)


def _mid_kernel(o_ref, z_ref, x_ref, cb_ref, sb_ref,
                wout_ref, gpost_ref, gkv_ref, wd_ref, glat_ref, wup_ref,
                gpre_ref, wcq_ref, wz_ref, gq_ref, wqup_ref,
                h1_ref, k_ref, v_ref, q_ref, zb_ref, *, parts):
    tm = x_ref.shape[1]
    tp = tm // parts
    lane = lax.broadcasted_iota(jnp.int32, (tp, LANES), 1)
    one_col = jnp.where(lane == 0, 1.0, 0.0)
    scale = B_QK_DIM ** -0.5 * LOG2E
    st = [dict() for _ in range(parts)]

    def stage1(i):
        rs = slice(i * tp, (i + 1) * tp)
        o = jnp.concatenate([o_ref[0, h, rs, :] for h in range(o_ref.shape[1])], axis=1)
        z = z_ref[0, rs, :].astype(F32)
        g = (o.astype(F32) * (z * jax.nn.sigmoid(z))).astype(BF16)
        y = jnp.dot(g, wout_ref[...], preferred_element_type=F32)
        h1 = x_ref[0, rs, :] + _rms(y, gpost_ref[...])
        h1_ref[0, rs, :] = h1
        st[i]["h1"] = h1

    def stage2(i):
        h1 = st[i].pop("h1")
        hn = _rms(h1, gkv_ref[...]).astype(BF16)
        hb = _rms(h1, gpre_ref[...]).astype(BF16)
        st[i]["ckr"] = jnp.dot(hn, wd_ref[...], preferred_element_type=F32)
        st[i]["cq"] = jnp.dot(hb, wcq_ref[...], preferred_element_type=F32)
        st[i]["hb"] = hb

    def stage3(i):
        rs = slice(i * tp, (i + 1) * tp)
        ckr = st[i].pop("ckr")
        c_kv = _rms(ckr[:, :B_KV_LORA], glat_ref[...]).astype(BF16)
        c_q = _rms(st[i].pop("cq"), gq_ref[...]).astype(BF16)
        st[i]["kv"] = jnp.dot(c_kv, wup_ref[...], preferred_element_type=F32)
        st[i]["qq"] = jnp.dot(c_q, wqup_ref[...], preferred_element_type=F32)
        zb_ref[0, rs, :] = jnp.dot(st[i].pop("hb"), wz_ref[...],
                                   preferred_element_type=F32).astype(BF16)
        st[i]["kr"] = ckr[:, B_KV_LORA:]

    def stage4(i):
        rs = slice(i * tp, (i + 1) * tp)
        cos = jnp.where(lane >= B_NOPE, cb_ref[0, rs, :], 1.0)
        sin = jnp.where(lane >= B_NOPE, sb_ref[0, rs, :], 0.0)
        k_rope = _rot_half(st[i].pop("kr"), B_NOPE, cos, sin)
        kv, qq = st[i].pop("kv"), st[i].pop("qq")
        cq, sq = cos * scale, sin * scale
        for h in range(B_HEADS):
            blk = kv[:, h * LANES:(h + 1) * LANES]
            k_ref[0, h, rs, :] = jnp.where(lane < B_NOPE, blk, k_rope).astype(BF16)
            v_ref[0, h, rs, :] = jnp.where(lane >= B_NOPE, blk, one_col).astype(BF16)
            q_ref[0, h, rs, :] = _rot_half(qq[:, h * LANES:(h + 1) * LANES],
                                           B_NOPE, cq, sq).astype(BF16)

    for stage in (stage1, stage2, stage3, stage4):
        for i in range(parts):
            stage(i)


def _mid(o_a, z_a, x, cos, sin, weights, tm=512, parts=2):
    B, T, D = x.shape
    tok = lambda w: pl.BlockSpec((1, tm, w), lambda b, i: (b, i, 0))
    heads = lambda n: pl.BlockSpec((1, n, tm, LANES), lambda b, i: (b, 0, i, 0))
    const = lambda a: pl.BlockSpec(a.shape, lambda b, i: (0, 0), pipeline_mode=pl.Buffered(1))
    return pl.pallas_call(
        functools.partial(_mid_kernel, parts=parts),
        grid=(B, T // tm),
        in_specs=[heads(A_HEADS), tok(A_WIDTH), tok(D), tok(LANES), tok(LANES)]
                 + [const(w) for w in weights],
        out_specs=[tok(D), heads(B_HEADS), heads(B_HEADS), heads(B_HEADS), tok(B_WIDTH)],
        out_shape=[
            jax.ShapeDtypeStruct((B, T, D), F32),
            jax.ShapeDtypeStruct((B, B_HEADS, T, LANES), BF16),
            jax.ShapeDtypeStruct((B, B_HEADS, T, LANES), BF16),
            jax.ShapeDtypeStruct((B, B_HEADS, T, LANES), BF16),
            jax.ShapeDtypeStruct((B, T, B_WIDTH), BF16),
        ],
        compiler_params=_cparams(("parallel", "parallel")),
        name="mid",
    )(o_a, z_a, x, cos, sin, *weights)


def _attn_b_kernel(q_ref, k_ref, v_ref, o_ref, *, tq, ahead):
    T = k_ref.shape[2]
    qi = lax.broadcasted_iota(jnp.int32, (tq, tq), 0)
    ki = lax.broadcasted_iota(jnp.int32, (tq, tq), 1)
    causal = ki <= qi
    lane = lax.broadcasted_iota(jnp.int32, (tq, LANES), 1)
    nt = (((1,), (1,)), ((), ()))
    units = [(ii, hh) for ii in range(T // tq) for hh in range(2)]

    def scores(ii, hh):
        lo = ii * tq
        q = q_ref[0, hh, lo:lo + tq, :]
        s_d = lax.dot_general(q, k_ref[0, hh, lo:lo + tq, :], nt, preferred_element_type=F32)
        s_d = jnp.where(causal, s_d, -jnp.inf)
        m = jnp.max(s_d, axis=-1, keepdims=True)
        s_m = None
        if ii:
            s_m = lax.dot_general(q, k_ref[0, hh, 0:lo, :], nt, preferred_element_type=F32)
            m = jnp.maximum(m, jnp.max(s_m, axis=-1, keepdims=True))
        return s_d, s_m, m

    def values(ii, hh, s_d, s_m, m):
        lo = ii * tq
        o = jnp.dot(jnp.exp2(s_d - m).astype(BF16), v_ref[0, hh, lo:lo + tq, :],
                    preferred_element_type=F32)
        if ii:
            o = o + jnp.dot(jnp.exp2(s_m - m).astype(BF16), v_ref[0, hh, 0:lo, :],
                            preferred_element_type=F32)
        l = jnp.sum(jnp.where(lane == 0, o, 0.0), axis=-1, keepdims=True)
        return o / l

    pending = [scores(*units[u]) for u in range(ahead)]
    outs = []
    for u, (ii, hh) in enumerate(units):
        cur = pending.pop(0)
        if u + ahead < len(units):
            pending.append(scores(*units[u + ahead]))
        outs.append(values(ii, hh, *cur))
        if hh:
            lo = ii * tq
            o_ref[0, 0, lo:lo + tq, :] = jnp.where(
                lane < B_VDIM, pltpu.roll(outs[0], B_VDIM, 1), outs[1]).astype(BF16)
            outs = []


def _attn_b(q, k, v, tq=256, ahead=3):
    B, _, T, _ = q.shape
    heads = pl.BlockSpec((1, 2, T, LANES), lambda b, p: (b, p, 0, 0))
    return pl.pallas_call(
        functools.partial(_attn_b_kernel, tq=tq, ahead=ahead),
        grid=(B, B_PAIRS),
        in_specs=[heads, heads, heads],
        out_specs=pl.BlockSpec((1, 1, T, LANES), lambda b, p: (b, p, 0, 0)),
        out_shape=jax.ShapeDtypeStruct((B, B_PAIRS, T, LANES), BF16),
        compiler_params=_cparams(("parallel", "parallel")),
        name="attn_b",
    )(q, k, v)


def _final_kernel(o_ref, z_ref, h1_ref, w_ref, g_ref, out_ref):
    y = jnp.dot(_gated(o_ref, z_ref), w_ref[...], preferred_element_type=F32)
    out_ref[0] = h1_ref[0] + _rms(y, g_ref[...])


def _final(o_b, z_b, h1, w, gain, tm=1024):
    B, T, D = h1.shape
    tok = lambda w_: pl.BlockSpec((1, tm, w_), lambda b, i: (b, i, 0))
    return pl.pallas_call(
        _final_kernel,
        grid=(B, T // tm),
        in_specs=[pl.BlockSpec((1, B_PAIRS, tm, LANES), lambda b, i: (b, 0, i, 0)),
                  tok(B_WIDTH), tok(D),
                  pl.BlockSpec(w.shape, lambda b, i: (0, 0)),
                  pl.BlockSpec((1, D), lambda b, i: (0, 0))],
        out_specs=tok(D),
        out_shape=jax.ShapeDtypeStruct((B, T, D), F32),
        compiler_params=_cparams(("parallel", "parallel")),
        name="final",
    )(o_b, z_b, h1, w, gain.reshape(1, D))


def kernel(x, positions, a_pre_norm, a_w_in, a_w_out, a_post_norm, kv_norm, kv_w_down, kv_latent_norm, kv_w_up, b_pre_norm, b_w_in, b_q_norm, b_w_q_up, b_w_out, b_post_norm):
    D = x.shape[-1]
    cos, sin = _rope_tables(positions)

    w_in = a_w_in[0].astype(BF16)
    gw = 3 * A_WIDTH
    qkv = []
    z_a = None
    for g, dil in enumerate(A_DILATIONS):
        w = w_in[:, g * gw:(g + 1) * gw]
        if g == 0:
            w = jnp.concatenate([w, w_in[:, 3 * gw:]], axis=1)
            qkv_g, z_a = _proj_a(x, cos, sin, a_pre_norm[0], w, dil, True)
        else:
            (qkv_g,) = _proj_a(x, cos, sin, a_pre_norm[0], w, dil, False)
        qkv.append(qkv_g)
    o_a = _attn_a(qkv)

    row = lambda g: g.reshape(1, -1)
    wd = jnp.zeros((D, B_KV_LORA + LANES), F32)
    wd = wd.at[:, :B_KV_LORA].set(kv_w_down[:, :B_KV_LORA])
    wd = wd.at[:, B_KV_LORA + B_NOPE:B_KV_LORA + B_QK_DIM].set(kv_w_down[:, B_KV_LORA:])
    wqup = jnp.pad(b_w_q_up[0].reshape(B_Q_LORA, B_HEADS, B_QK_DIM),
                   ((0, 0), (0, 0), (0, LANES - B_QK_DIM))).reshape(B_Q_LORA, B_HEADS * LANES)
    weights = [
        a_w_out[0].astype(BF16), row(a_post_norm[0]), row(kv_norm), wd.astype(BF16),
        row(kv_latent_norm), kv_w_up.astype(BF16),
        row(b_pre_norm[0]), b_w_in[0][:, :B_Q_LORA].astype(BF16),
        b_w_in[0][:, B_Q_LORA:].astype(BF16), row(b_q_norm[0]), wqup.astype(BF16),
    ]
    h1, k_b, v_b, q_b, z_b = _mid(o_a, z_a, x, cos, sin, weights)

    o_b = _attn_b(q_b, k_b, v_b)
    return _final(o_b, z_b, h1, b_w_out[0].astype(BF16), b_post_norm[0])
```

```python
import functools

import jax
import jax.numpy as jnp
from jax import lax
from jax.experimental import pallas as pl
from jax.experimental.pallas import tpu as pltpu

F32 = jnp.float32
BF16 = jnp.bfloat16

NORM_EPS = 1e-6
LOG2E = 1.4426950408889634
LANES = 128

A_WINDOWS = (128, 512, 2048)
A_DILATIONS = (1, 4, 16)
A_HEADS = 8
A_HEAD_DIM = 128
A_WIDTH = A_HEADS * A_HEAD_DIM
A_ROT_DIM = A_HEAD_DIM // 4
A_ROPE_THETA = 500000.0
BAND = 128
MAX_ROW_STRIDE = 4

B_HEADS = 16
B_NOPE = 64
B_ROPE = 32
B_QK_DIM = B_NOPE + B_ROPE
B_VDIM = 64
B_WIDTH = B_HEADS * B_VDIM
B_Q_LORA = 384
B_KV_LORA = 256
B_ROPE_THETA = 10000.0
B_PAIRS = B_HEADS // 2

VMEM_LIMIT = 56 * 1024 * 1024


def _cparams(sem):
    return pltpu.CompilerParams(dimension_semantics=sem, vmem_limit_bytes=VMEM_LIMIT)


def _rms(x, g):
    ms = jnp.mean(x * x, axis=-1, keepdims=True)
    return x * lax.rsqrt(ms + NORM_EPS) * g


def _rot_half(a, lo, cos, sin):
    lane = lax.broadcasted_iota(jnp.int32, a.shape, 1)
    partner = jnp.where(lane < lo + 16, pltpu.roll(a, LANES - 16, 1), pltpu.roll(a, 16, 1))
    return a * cos + partner * sin


def _rope_table_kernel(pe_ref, po_ref, freq_ref, sign_ref, cos_ref, sin_ref):
    half_t = pe_ref.shape[1]
    lane = lax.broadcasted_iota(jnp.int32, (half_t, LANES), 1)
    pos = jnp.where(lane < LANES // 2, pe_ref[0].astype(F32), po_ref[0].astype(F32))
    ang = pos * freq_ref[...]
    for ref, packed, ident in ((cos_ref, jnp.cos(ang), 1.0),
                               (sin_ref, jnp.sin(ang) * sign_ref[...], 0.0)):
        def unpack(lo0, lo1):
            a = packed if lo0 == 0 else pltpu.roll(packed, LANES - lo0, 1)
            b_ = pltpu.roll(packed, (B_NOPE - lo1) % LANES, 1)
            return jnp.where(lane < A_ROT_DIM, a,
                             jnp.where((lane >= B_NOPE) & (lane < B_QK_DIM), b_, ident))
        ref[0, pl.ds(0, half_t, stride=2), :] = unpack(0, A_ROT_DIM)
        ref[0, pl.ds(1, half_t, stride=2), :] = unpack(LANES // 2, LANES // 2 + A_ROT_DIM)


def _rope_tables(positions):
    B, T = positions.shape
    half = A_ROT_DIM // 2
    inv_a = 1.0 / (A_ROPE_THETA ** (jnp.arange(half, dtype=F32) * (2.0 / A_ROT_DIM)))
    inv_b = 1.0 / (B_ROPE_THETA ** (jnp.arange(B_ROPE // 2, dtype=F32) * (2.0 / B_ROPE)))
    ones = jnp.ones((half,), F32)
    freq = jnp.tile(jnp.concatenate([inv_a, inv_a, inv_b, inv_b]), 2).reshape(1, LANES)
    sign = jnp.tile(jnp.concatenate([-ones, ones, -ones, ones]), 2).reshape(1, LANES)
    tab = jax.ShapeDtypeStruct((B, T, LANES), F32)
    spec = pl.BlockSpec((1, T, LANES), lambda b: (b, 0, 0))
    const = pl.BlockSpec((1, LANES), lambda b: (0, 0))
    pos = pl.BlockSpec((1, T // 2, 1), lambda b: (b, 0, 0))
    return pl.pallas_call(
        _rope_table_kernel,
        grid=(B,),
        in_specs=[pos, pos, const, const],
        out_specs=[spec] * 2,
        out_shape=[tab] * 2,
        compiler_params=_cparams(("parallel",)),
        name="rope_tables",
    )(positions[:, 0::2].reshape(B, T // 2, 1), positions[:, 1::2].reshape(B, T // 2, 1),
      freq, sign)


def _proj_a_kernel(x_ref, c_ref, s_ref, g_ref, w_ref, *rest, dil, with_z, parts):
    if with_z:
        wz_ref, qkv_ref, z_ref = rest[:3]
    else:
        qkv_ref = rest[0]
    tm_all, D = x_ref.shape[1], x_ref.shape[2]
    tm = tm_all // parts
    tl = tm // dil
    nslab = D // LANES

    def by_class(load, stage):
        if dil <= MAX_ROW_STRIDE:
            return jnp.concatenate([load(pl.ds(r, tl, stride=dil)) for r in range(dil)], axis=0)
        inner = dil // MAX_ROW_STRIDE
        tq = tm // MAX_ROW_STRIDE
        for q0 in range(MAX_ROW_STRIDE):
            stage[q0 * tq:(q0 + 1) * tq, :] = load(pl.ds(q0, tq, stride=MAX_ROW_STRIDE))
        return jnp.concatenate(
            [stage[pl.ds(q0 * tq + q1, tl, stride=inner), :]
             for q1 in range(inner) for q0 in range(MAX_ROW_STRIDE)], axis=0)

    def inputs(i):
        rs = slice(i * tm, (i + 1) * tm)
        if dil == 1:
            return x_ref[0, rs, :], c_ref[0, rs, :], s_ref[0, rs, :]
        slabs, stage = rest[-2].at[i], rest[-1].at[i]
        for c in range(nslab):
            slabs[c] = x_ref[0, rs, c * LANES:(c + 1) * LANES]
        slabs[nslab] = c_ref[0, rs, :]
        slabs[nslab + 1] = s_ref[0, rs, :]
        p = [by_class(lambda idx, c=c: slabs[c, idx, :], stage.at[c]) for c in range(nslab + 2)]
        return jnp.concatenate(p[:nslab], axis=1), p[nslab], p[nslab + 1]

    lane = lax.broadcasted_iota(jnp.int32, (tm, LANES), 1)
    scale = A_HEAD_DIM ** -0.5 * LOG2E

    def project(i, x, cos, sin):
        h = _rms(x, g_ref[...]).astype(BF16)
        cos = jnp.where(lane < A_ROT_DIM, cos, 1.0)
        sin = jnp.where(lane < A_ROT_DIM, sin, 0.0)
        for s in range(3):
            acc = jnp.dot(h, w_ref[:, s * A_WIDTH:(s + 1) * A_WIDTH],
                          preferred_element_type=F32)
            for hd in range(A_HEADS):
                a = acc[:, hd * LANES:(hd + 1) * LANES]
                if s == 0:
                    a = _rot_half(a, 0, cos * scale, sin * scale)
                elif s == 1:
                    a = _rot_half(a, 0, cos, sin)
                a = a.astype(BF16)
                for r in range(dil):
                    qkv_ref[0, s * A_HEADS + hd, r, i * tl:(i + 1) * tl, :] = a[r * tl:(r + 1) * tl]
        if with_z:
            z_ref[0, i * tm:(i + 1) * tm, :] = jnp.dot(
                h, wz_ref[...], preferred_element_type=F32).astype(BF16)

    staged = [inputs(i) for i in range(parts)]
    for i in range(parts):
        project(i, *staged[i])


def _proj_a(x, cos, sin, gain, w, group, with_z, tm=512, parts=1):
    B, T, D = x.shape
    dil = A_DILATIONS[group]
    L = T // dil
    tm = tm * parts
    tl = tm // dil
    tok = lambda n: pl.BlockSpec((1, tm, n), lambda b, i: (b, i, 0))
    out_shape = [jax.ShapeDtypeStruct((B, 3 * A_HEADS, dil, L, LANES), BF16)]
    out_specs = [pl.BlockSpec((1, 3 * A_HEADS, dil, tl, LANES), lambda b, i: (b, 0, 0, i, 0))]
    if with_z:
        out_shape.append(jax.ShapeDtypeStruct((B, T, A_WIDTH), BF16))
        out_specs.append(tok(A_WIDTH))
    return pl.pallas_call(
        functools.partial(_proj_a_kernel, dil=dil, with_z=with_z, parts=parts),
        grid=(B, T // tm),
        in_specs=[tok(D), tok(LANES), tok(LANES),
                  pl.BlockSpec((1, D), lambda b, i: (0, 0)),
                  pl.BlockSpec((D, 3 * A_WIDTH), lambda b, i: (0, group))]
                 + ([pl.BlockSpec((D, A_WIDTH), lambda b, i: (0, 3 * len(A_DILATIONS)))]
                    if with_z else []),
        out_specs=out_specs,
        out_shape=out_shape,
        scratch_shapes=([pltpu.VMEM((parts, D // LANES + 2, tm // parts, LANES), F32)] * 2
                        if dil > 1 else []),
        compiler_params=_cparams(("parallel", "parallel")),
        name=f"proj_a_d{dil}",
    )(x, cos, sin, gain.reshape(1, D), *([w, w] if with_z else [w]))


def _attn_a_kernel(g0_ref, g1_ref, g2_ref, o_ref, onat, mnat, lnat, stage, *, chunk):
    T = o_ref.shape[2]
    nt = (((1,), (1,)), ((), ()))
    qi = lax.broadcasted_iota(jnp.int32, (BAND, 2 * BAND), 0)
    ki = lax.broadcasted_iota(jnp.int32, (BAND, 2 * BAND), 1)
    band_mask = (ki >= qi) & (ki <= qi + BAND)
    first_mask = (lax.broadcasted_iota(jnp.int32, (BAND, BAND), 1)
                  <= lax.broadcasted_iota(jnp.int32, (BAND, BAND), 0))
    nblk = T // BAND
    refs = (g0_ref, g1_ref, g2_ref)

    def key_lo(g, j):
        nb = T // A_DILATIONS[g] // BAND
        return (j - 1) * BAND if j % nb else j * BAND

    def scores(g, c0):
        ref, nb = refs[g], T // A_DILATIONS[g] // BAND
        js = range(c0, c0 + chunk)
        s = [lax.dot_general(ref[0, 0, 0, j * BAND:(j + 1) * BAND, :],
                             ref[0, 1, 0, key_lo(g, j):(j + 1) * BAND, :], nt,
                             preferred_element_type=F32) for j in js]
        s = [jnp.where(band_mask if j % nb else first_mask, x, -jnp.inf) for j, x in zip(js, s)]
        m = [jnp.max(x, axis=-1, keepdims=True) for x in s]
        return s, m

    def values(g, c0, s, m):
        ref, dil = refs[g], A_DILATIONS[g]
        nb = T // dil // BAND
        js = range(c0, c0 + chunk)
        p = [jnp.exp2(x - y).astype(BF16) for x, y in zip(s, m)]
        o = [jnp.dot(x, jnp.concatenate([ref[0, 2, 0, key_lo(g, j):(j + 1) * BAND, :],
                                         jnp.ones(((j + 1) * BAND - key_lo(g, j), LANES), BF16)],
                                        axis=1),
                     preferred_element_type=F32) for x, j in zip(p, js)]
        for j, oj, mj in zip(js, o, m):
            r, n = divmod(j, nb)
            mj = jnp.broadcast_to(mj, (BAND, LANES))
            oj, lj = oj[:, :LANES], oj[:, LANES:]
            if dil == 1:
                sl = slice(j * BAND, (j + 1) * BAND)
                m1, m2 = mnat[0, sl, :], mnat[1, sl, :]
                mx = jnp.maximum(jnp.maximum(mj, m1), m2)
                e0, e1, e2 = jnp.exp2(mj - mx), jnp.exp2(m1 - mx), jnp.exp2(m2 - mx)
                num = e0 * oj + e1 * onat[0, sl, :] + e2 * onat[1, sl, :]
                den = e0 * lj + e1 * lnat[0, sl, :] + e2 * lnat[1, sl, :]
                o_ref[0, 0, sl, :] = (num / den).astype(BF16)
            elif dil > MAX_ROW_STRIDE:
                inner = dil // MAX_ROW_STRIDE
                r1, r0 = divmod(r, MAX_ROW_STRIDE)
                idx = pl.ds(r1 + n * BAND * inner, BAND, stride=inner)
                stage[0, r0, idx, :] = oj
                stage[1, r0, idx, :] = mj
                stage[2, r0, idx, :] = lj
            else:
                idx = pl.ds(r + n * BAND * dil, BAND, stride=dil)
                onat[g - 1, idx, :] = oj
                mnat[g - 1, idx, :] = mj
                lnat[g - 1, idx, :] = lj

    def second_pass(g):
        for a, nat in enumerate((onat, mnat, lnat)):
            for r0 in range(MAX_ROW_STRIDE):
                nat[g - 1, pl.ds(r0, T // MAX_ROW_STRIDE, stride=MAX_ROW_STRIDE), :] = stage[a, r0]

    order = sorted(range(len(refs)), key=lambda g: -A_DILATIONS[g])
    assert A_DILATIONS[order[-1]] == 1
    work = [(g, c0) for g in order for c0 in range(0, nblk, chunk)]
    nxt = scores(*work[0])
    for w, (g, c0) in enumerate(work):
        cur = nxt
        if w + 1 < len(work):
            nxt = scores(*work[w + 1])
        values(g, c0, *cur)
        last_of_group = w + 1 == len(work) or work[w + 1][0] != g
        if last_of_group and A_DILATIONS[g] > MAX_ROW_STRIDE:
            second_pass(g)


def _attn_a(qkv, chunk=8):
    B, T = qkv[0].shape[0], qkv[0].shape[2] * qkv[0].shape[3]
    views = [a.reshape(B, 3, A_HEADS, T, LANES) for a in qkv]
    spec = pl.BlockSpec((1, 3, 1, T, LANES), lambda b, h: (b, 0, h, 0, 0))
    return pl.pallas_call(
        functools.partial(_attn_a_kernel, chunk=chunk),
        grid=(B, A_HEADS),
        in_specs=[spec] * 3,
        out_specs=pl.BlockSpec((1, 1, T, LANES), lambda b, h: (b, h, 0, 0)),
        out_shape=jax.ShapeDtypeStruct((B, A_HEADS, T, LANES), BF16),
        scratch_shapes=[pltpu.VMEM((len(A_DILATIONS) - 1, T, LANES), F32)] * 3
                       + [pltpu.VMEM((3, MAX_ROW_STRIDE, T // MAX_ROW_STRIDE, LANES), F32)],
        compiler_params=_cparams(("parallel", "parallel")),
        name="attn_a",
    )(*views)


def _gated(o_ref, z_ref):
    o = jnp.concatenate([o_ref[0, h] for h in range(o_ref.shape[1])], axis=1)
    z = z_ref[0].astype(F32)
    return (o.astype(F32) * (z * jax.nn.sigmoid(z))).astype(BF16)


def _mid_kernel(o_ref, z_ref, x_ref, cb_ref, sb_ref,
                wout_ref, gpost_ref, gkv_ref, wd_ref, glat_ref, wup_ref,
                gpre_ref, wcq_ref, wz_ref, gq_ref, wqup_ref,
                h1_ref, k_ref, v_ref, q_ref, zb_ref, *, parts):
    tm = x_ref.shape[1]
    tp = tm // parts
    lane = lax.broadcasted_iota(jnp.int32, (tp, LANES), 1)
    one_col = jnp.where(lane == 0, 1.0, 0.0)
    scale = B_QK_DIM ** -0.5 * LOG2E
    st = [dict() for _ in range(parts)]

    def stage1(i):
        rs = slice(i * tp, (i + 1) * tp)
        o = jnp.concatenate([o_ref[0, h, rs, :] for h in range(o_ref.shape[1])], axis=1)
        z = z_ref[0, rs, :].astype(F32)
        g = (o.astype(F32) * (z * jax.nn.sigmoid(z))).astype(BF16)
        y = jnp.dot(g, wout_ref[...], preferred_element_type=F32)
        h1 = x_ref[0, rs, :] + _rms(y, gpost_ref[...])
        h1_ref[0, rs, :] = h1
        st[i]["h1"] = h1

    def stage2(i):
        h1 = st[i].pop("h1")
        hn = _rms(h1, gkv_ref[...]).astype(BF16)
        hb = _rms(h1, gpre_ref[...]).astype(BF16)
        st[i]["ckr"] = jnp.dot(hn, wd_ref[...], preferred_element_type=F32)
        st[i]["cq"] = jnp.dot(hb, wcq_ref[...], preferred_element_type=F32)
        st[i]["hb"] = hb

    def stage3(i):
        rs = slice(i * tp, (i + 1) * tp)
        ckr = st[i].pop("ckr")
        c_kv = _rms(ckr[:, :B_KV_LORA], glat_ref[...]).astype(BF16)
        c_q = _rms(st[i].pop("cq"), gq_ref[...]).astype(BF16)
        st[i]["kv"] = jnp.dot(c_kv, wup_ref[...], preferred_element_type=F32)
        st[i]["qq"] = jnp.dot(c_q, wqup_ref[...], preferred_element_type=F32)
        zb_ref[0, rs, :] = jnp.dot(st[i].pop("hb"), wz_ref[...],
                                   preferred_element_type=F32).astype(BF16)
        st[i]["kr"] = ckr[:, B_KV_LORA:]

    def stage4(i):
        rs = slice(i * tp, (i + 1) * tp)
        cos = jnp.where(lane >= B_NOPE, cb_ref[0, rs, :], 1.0)
        sin = jnp.where(lane >= B_NOPE, sb_ref[0, rs, :], 0.0)
        k_rope = _rot_half(st[i].pop("kr"), B_NOPE, cos, sin)
        kv, qq = st[i].pop("kv"), st[i].pop("qq")
        cq, sq = cos * scale, sin * scale
        for h in range(B_HEADS):
            blk = kv[:, h * LANES:(h + 1) * LANES]
            k_ref[0, h, rs, :] = jnp.where(lane < B_NOPE, blk, k_rope).astype(BF16)
            v_ref[0, h, rs, :] = jnp.where(lane >= B_NOPE, blk, one_col).astype(BF16)
            q_ref[0, h, rs, :] = _rot_half(qq[:, h * LANES:(h + 1) * LANES],
                                           B_NOPE, cq, sq).astype(BF16)

    for stage in (stage1, stage2, stage3, stage4):
        for i in range(parts):
            stage(i)


def _mid(o_a, z_a, x, cos, sin, weights, tm=512, parts=2):
    B, T, D = x.shape
    tok = lambda w: pl.BlockSpec((1, tm, w), lambda b, i: (b, i, 0))
    heads = lambda n: pl.BlockSpec((1, n, tm, LANES), lambda b, i: (b, 0, i, 0))
    const = lambda a: pl.BlockSpec(a.shape, lambda b, i: (0, 0), pipeline_mode=pl.Buffered(1))
    return pl.pallas_call(
        functools.partial(_mid_kernel, parts=parts),
        grid=(B, T // tm),
        in_specs=[heads(A_HEADS), tok(A_WIDTH), tok(D), tok(LANES), tok(LANES)]
                 + [const(w) for w in weights],
        out_specs=[tok(D), heads(B_HEADS), heads(B_HEADS), heads(B_HEADS), tok(B_WIDTH)],
        out_shape=[
            jax.ShapeDtypeStruct((B, T, D), F32),
            jax.ShapeDtypeStruct((B, B_HEADS, T, LANES), BF16),
            jax.ShapeDtypeStruct((B, B_HEADS, T, LANES), BF16),
            jax.ShapeDtypeStruct((B, B_HEADS, T, LANES), BF16),
            jax.ShapeDtypeStruct((B, T, B_WIDTH), BF16),
        ],
        compiler_params=_cparams(("parallel", "parallel")),
        name="mid",
    )(o_a, z_a, x, cos, sin, *weights)


def _attn_b_kernel(q_ref, k_ref, v_ref, o_ref, *, tq, ahead):
    T = k_ref.shape[2]
    qi = lax.broadcasted_iota(jnp.int32, (tq, tq), 0)
    ki = lax.broadcasted_iota(jnp.int32, (tq, tq), 1)
    causal = ki <= qi
    lane = lax.broadcasted_iota(jnp.int32, (tq, LANES), 1)
    nt = (((1,), (1,)), ((), ()))
    units = [(ii, hh) for ii in range(T // tq) for hh in range(2)]

    def scores(ii, hh):
        lo = ii * tq
        q = q_ref[0, hh, lo:lo + tq, :]
        s_d = lax.dot_general(q, k_ref[0, hh, lo:lo + tq, :], nt, preferred_element_type=F32)
        s_d = jnp.where(causal, s_d, -jnp.inf)
        m = jnp.max(s_d, axis=-1, keepdims=True)
        s_m = None
        if ii:
            s_m = lax.dot_general(q, k_ref[0, hh, 0:lo, :], nt, preferred_element_type=F32)
            m = jnp.maximum(m, jnp.max(s_m, axis=-1, keepdims=True))
        return s_d, s_m, m

    def values(ii, hh, s_d, s_m, m):
        lo = ii * tq
        o = jnp.dot(jnp.exp2(s_d - m).astype(BF16), v_ref[0, hh, lo:lo + tq, :],
                    preferred_element_type=F32)
        if ii:
            o = o + jnp.dot(jnp.exp2(s_m - m).astype(BF16), v_ref[0, hh, 0:lo, :],
                            preferred_element_type=F32)
        l = jnp.sum(jnp.where(lane == 0, o, 0.0), axis=-1, keepdims=True)
        return o / l

    pending = [scores(*units[u]) for u in range(ahead)]
    outs = []
    for u, (ii, hh) in enumerate(units):
        cur = pending.pop(0)
        if u + ahead < len(units):
            pending.append(scores(*units[u + ahead]))
        outs.append(values(ii, hh, *cur))
        if hh:
            lo = ii * tq
            o_ref[0, 0, lo:lo + tq, :] = jnp.where(
                lane < B_VDIM, pltpu.roll(outs[0], B_VDIM, 1), outs[1]).astype(BF16)
            outs = []


def _attn_b(q, k, v, tq=256, ahead=3):
    B, _, T, _ = q.shape
    heads = pl.BlockSpec((1, 2, T, LANES), lambda b, p: (b, p, 0, 0))
    return pl.pallas_call(
        functools.partial(_attn_b_kernel, tq=tq, ahead=ahead),
        grid=(B, B_PAIRS),
        in_specs=[heads, heads, heads],
        out_specs=pl.BlockSpec((1, 1, T, LANES), lambda b, p: (b, p, 0, 0)),
        out_shape=jax.ShapeDtypeStruct((B, B_PAIRS, T, LANES), BF16),
        compiler_params=_cparams(("parallel", "parallel")),
        name="attn_b",
    )(q, k, v)


def _final_kernel(o_ref, z_ref, h1_ref, w_ref, g_ref, out_ref):
    y = jnp.dot(_gated(o_ref, z_ref), w_ref[...], preferred_element_type=F32)
    out_ref[0] = h1_ref[0] + _rms(y, g_ref[...])


def _final(o_b, z_b, h1, w, gain, tm=1024):
    B, T, D = h1.shape
    tok = lambda w_: pl.BlockSpec((1, tm, w_), lambda b, i: (b, i, 0))
    return pl.pallas_call(
        _final_kernel,
        grid=(B, T // tm),
        in_specs=[pl.BlockSpec((1, B_PAIRS, tm, LANES), lambda b, i: (b, 0, i, 0)),
                  tok(B_WIDTH), tok(D),
                  pl.BlockSpec(w.shape, lambda b, i: (0, 0)),
                  pl.BlockSpec((1, D), lambda b, i: (0, 0))],
        out_specs=tok(D),
        out_shape=jax.ShapeDtypeStruct((B, T, D), F32),
        compiler_params=_cparams(("parallel", "parallel")),
        name="final",
    )(o_b, z_b, h1, w, gain.reshape(1, D))


def kernel(x, positions, a_pre_norm, a_w_in, a_w_out, a_post_norm, kv_norm, kv_w_down, kv_latent_norm, kv_w_up, b_pre_norm, b_w_in, b_q_norm, b_w_q_up, b_w_out, b_post_norm):
    D = x.shape[-1]
    cos, sin = _rope_tables(positions)

    w_in = a_w_in[0].astype(BF16)
    qkv = []
    z_a = None
    for g, dil in enumerate(A_DILATIONS):
        if dil == 1:
            qkv_g, z_a = _proj_a(x, cos, sin, a_pre_norm[0], w_in, g, True)
        else:
            (qkv_g,) = _proj_a(x, cos, sin, a_pre_norm[0], w_in, g, False, parts=2)
        qkv.append(qkv_g)
    o_a = _attn_a(qkv)

    row = lambda g: g.reshape(1, -1)
    wd = jnp.zeros((D, B_KV_LORA + LANES), F32)
    wd = wd.at[:, :B_KV_LORA].set(kv_w_down[:, :B_KV_LORA])
    wd = wd.at[:, B_KV_LORA + B_NOPE:B_KV_LORA + B_QK_DIM].set(kv_w_down[:, B_KV_LORA:])
    wqup = jnp.pad(b_w_q_up[0].reshape(B_Q_LORA, B_HEADS, B_QK_DIM),
                   ((0, 0), (0, 0), (0, LANES - B_QK_DIM))).reshape(B_Q_LORA, B_HEADS * LANES)
    weights = [
        a_w_out[0].astype(BF16), row(a_post_norm[0]), row(kv_norm), wd.astype(BF16),
        row(kv_latent_norm), kv_w_up.astype(BF16),
        row(b_pre_norm[0]), b_w_in[0][:, :B_Q_LORA].astype(BF16),
        b_w_in[0][:, B_Q_LORA:].astype(BF16), row(b_q_norm[0]), wqup.astype(BF16),
    ]
    h1, k_b, v_b, q_b, z_b = _mid(o_a, z_a, x, cos, sin, weights)

    o_b = _attn_b(q_b, k_b, v_b)
    return _final(o_b, z_b, h1, b_w_out[0].astype(BF16), b_post_norm[0])
```

```python
import functools

import jax
import jax.numpy as jnp
from jax import lax
from jax.experimental import pallas as pl
from jax.experimental.pallas import tpu as pltpu

F32 = jnp.float32
BF16 = jnp.bfloat16

NORM_EPS = 1e-6
LOG2E = 1.4426950408889634
LANES = 128

A_WINDOWS = (128, 512, 2048)
A_DILATIONS = (1, 4, 16)
A_HEADS = 8
A_HEAD_DIM = 128
A_WIDTH = A_HEADS * A_HEAD_DIM
A_ROT_DIM = A_HEAD_DIM // 4
A_ROPE_THETA = 500000.0
BAND = 128
MAX_ROW_STRIDE = 4

B_HEADS = 16
B_NOPE = 64
B_ROPE = 32
B_QK_DIM = B_NOPE + B_ROPE
B_VDIM = 64
B_WIDTH = B_HEADS * B_VDIM
B_Q_LORA = 384
B_KV_LORA = 256
B_ROPE_THETA = 10000.0
B_PAIRS = B_HEADS // 2

VMEM_LIMIT = 56 * 1024 * 1024


def _cparams(sem):
    return pltpu.CompilerParams(dimension_semantics=sem, vmem_limit_bytes=VMEM_LIMIT)


def _rms(x, g):
    ms = jnp.mean(x * x, axis=-1, keepdims=True)
    return x * lax.rsqrt(ms + NORM_EPS) * g


def _rot_half(a, lo, cos, sin):
    lane = lax.broadcasted_iota(jnp.int32, a.shape, 1)
    partner = jnp.where(lane < lo + 16, pltpu.roll(a, LANES - 16, 1), pltpu.roll(a, 16, 1))
    return a * cos + partner * sin


def _rope_table_kernel(pe_ref, po_ref, freq_ref, sign_ref, cos_ref, sin_ref):
    half_t = pe_ref.shape[1]
    lane = lax.broadcasted_iota(jnp.int32, (half_t, LANES), 1)
    pos = jnp.where(lane < LANES // 2, pe_ref[0].astype(F32), po_ref[0].astype(F32))
    ang = pos * freq_ref[...]
    for ref, packed, ident in ((cos_ref, jnp.cos(ang), 1.0),
                               (sin_ref, jnp.sin(ang) * sign_ref[...], 0.0)):
        def unpack(lo0, lo1):
            a = packed if lo0 == 0 else pltpu.roll(packed, LANES - lo0, 1)
            b_ = pltpu.roll(packed, (B_NOPE - lo1) % LANES, 1)
            return jnp.where(lane < A_ROT_DIM, a,
                             jnp.where((lane >= B_NOPE) & (lane < B_QK_DIM), b_, ident))
        ref[0, pl.ds(0, half_t, stride=2), :] = unpack(0, A_ROT_DIM)
        ref[0, pl.ds(1, half_t, stride=2), :] = unpack(LANES // 2, LANES // 2 + A_ROT_DIM)


def _rope_tables(positions):
    B, T = positions.shape
    half = A_ROT_DIM // 2
    inv_a = 1.0 / (A_ROPE_THETA ** (jnp.arange(half, dtype=F32) * (2.0 / A_ROT_DIM)))
    inv_b = 1.0 / (B_ROPE_THETA ** (jnp.arange(B_ROPE // 2, dtype=F32) * (2.0 / B_ROPE)))
    ones = jnp.ones((half,), F32)
    freq = jnp.tile(jnp.concatenate([inv_a, inv_a, inv_b, inv_b]), 2).reshape(1, LANES)
    sign = jnp.tile(jnp.concatenate([-ones, ones, -ones, ones]), 2).reshape(1, LANES)
    tab = jax.ShapeDtypeStruct((B, T, LANES), F32)
    spec = pl.BlockSpec((1, T, LANES), lambda b: (b, 0, 0))
    const = pl.BlockSpec((1, LANES), lambda b: (0, 0))
    pos = pl.BlockSpec((1, T // 2, 1), lambda b: (b, 0, 0))
    return pl.pallas_call(
        _rope_table_kernel,
        grid=(B,),
        in_specs=[pos, pos, const, const],
        out_specs=[spec] * 2,
        out_shape=[tab] * 2,
        compiler_params=_cparams(("parallel",)),
        name="rope_tables",
    )(positions[:, 0::2].reshape(B, T // 2, 1), positions[:, 1::2].reshape(B, T // 2, 1),
      freq, sign)


def _proj_a_kernel(x_ref, c_ref, s_ref, g_ref, w_ref, *rest, dil, with_z, parts):
    if with_z:
        wz_ref, qkv_ref, z_ref = rest[:3]
    else:
        qkv_ref = rest[0]
    tm_all, D = x_ref.shape[1], x_ref.shape[2]
    tm = tm_all // parts
    tl = tm // dil
    nslab = D // LANES

    def by_class(load, stage):
        if dil <= MAX_ROW_STRIDE:
            return jnp.concatenate([load(pl.ds(r, tl, stride=dil)) for r in range(dil)], axis=0)
        inner = dil // MAX_ROW_STRIDE
        tq = tm // MAX_ROW_STRIDE
        for q0 in range(MAX_ROW_STRIDE):
            stage[q0 * tq:(q0 + 1) * tq, :] = load(pl.ds(q0, tq, stride=MAX_ROW_STRIDE))
        return jnp.concatenate(
            [stage[pl.ds(q0 * tq + q1, tl, stride=inner), :]
             for q1 in range(inner) for q0 in range(MAX_ROW_STRIDE)], axis=0)

    def inputs(i):
        rs = slice(i * tm, (i + 1) * tm)
        if dil == 1:
            return x_ref[0, rs, :], c_ref[0, rs, :], s_ref[0, rs, :]
        slabs, stage = rest[-2].at[i], rest[-1].at[i]
        for c in range(nslab):
            slabs[c] = x_ref[0, rs, c * LANES:(c + 1) * LANES]
        slabs[nslab] = c_ref[0, rs, :]
        slabs[nslab + 1] = s_ref[0, rs, :]
        p = [by_class(lambda idx, c=c: slabs[c, idx, :], stage.at[c]) for c in range(nslab + 2)]
        return jnp.concatenate(p[:nslab], axis=1), p[nslab], p[nslab + 1]

    lane = lax.broadcasted_iota(jnp.int32, (tm, LANES), 1)
    scale = A_HEAD_DIM ** -0.5 * LOG2E

    def project(i, x, cos, sin):
        h = _rms(x, g_ref[...]).astype(BF16)
        cos = jnp.where(lane < A_ROT_DIM, cos, 1.0)
        sin = jnp.where(lane < A_ROT_DIM, sin, 0.0)
        for s in range(3):
            acc = jnp.dot(h, w_ref[:, s * A_WIDTH:(s + 1) * A_WIDTH],
                          preferred_element_type=F32)
            for hd in range(A_HEADS):
                a = acc[:, hd * LANES:(hd + 1) * LANES]
                if s == 0:
                    a = _rot_half(a, 0, cos * scale, sin * scale)
                elif s == 1:
                    a = _rot_half(a, 0, cos, sin)
                a = a.astype(BF16)
                for r in range(dil):
                    qkv_ref[0, s * A_HEADS + hd, r, i * tl:(i + 1) * tl, :] = a[r * tl:(r + 1) * tl]
        if with_z:
            z_ref[0, i * tm:(i + 1) * tm, :] = jnp.dot(
                h, wz_ref[...], preferred_element_type=F32).astype(BF16)

    staged = [inputs(i) for i in range(parts)]
    for i in range(parts):
        project(i, *staged[i])


def _proj_a(x, cos, sin, gain, w, group, with_z, tm=512, parts=1):
    B, T, D = x.shape
    dil = A_DILATIONS[group]
    L = T // dil
    tm = tm * parts
    tl = tm // dil
    tok = lambda n: pl.BlockSpec((1, tm, n), lambda b, i: (b, i, 0))
    out_shape = [jax.ShapeDtypeStruct((B, 3 * A_HEADS, dil, L, LANES), BF16)]
    out_specs = [pl.BlockSpec((1, 3 * A_HEADS, dil, tl, LANES), lambda b, i: (b, 0, 0, i, 0))]
    if with_z:
        out_shape.append(jax.ShapeDtypeStruct((B, T, A_WIDTH), BF16))
        out_specs.append(tok(A_WIDTH))
    return pl.pallas_call(
        functools.partial(_proj_a_kernel, dil=dil, with_z=with_z, parts=parts),
        grid=(B, T // tm),
        in_specs=[tok(D), tok(LANES), tok(LANES),
                  pl.BlockSpec((1, D), lambda b, i: (0, 0)),
                  pl.BlockSpec((D, 3 * A_WIDTH), lambda b, i: (0, group))]
                 + ([pl.BlockSpec((D, A_WIDTH), lambda b, i: (0, 3 * len(A_DILATIONS)))]
                    if with_z else []),
        out_specs=out_specs,
        out_shape=out_shape,
        scratch_shapes=([pltpu.VMEM((parts, D // LANES + 2, tm // parts, LANES), F32)] * 2
                        if dil > 1 else []),
        compiler_params=_cparams(("parallel", "parallel")),
        name=f"proj_a_d{dil}",
    )(x, cos, sin, gain.reshape(1, D), *([w, w] if with_z else [w]))


def _attn_a_kernel(g0_ref, g1_ref, g2_ref, o_ref, onat, mnat, lnat, stage, *, chunk):
    for hd in range(o_ref.shape[1]):
        _attn_a_head(hd, g0_ref, g1_ref, g2_ref, o_ref, onat, mnat, lnat, stage, chunk)


def _attn_a_head(hd, g0_ref, g1_ref, g2_ref, o_ref, onat, mnat, lnat, stage, chunk):
    T = o_ref.shape[2]
    nt = (((1,), (1,)), ((), ()))
    qi = lax.broadcasted_iota(jnp.int32, (BAND, 2 * BAND), 0)
    ki = lax.broadcasted_iota(jnp.int32, (BAND, 2 * BAND), 1)
    band_mask = (ki >= qi) & (ki <= qi + BAND)
    first_mask = (lax.broadcasted_iota(jnp.int32, (BAND, BAND), 1)
                  <= lax.broadcasted_iota(jnp.int32, (BAND, BAND), 0))
    nblk = T // BAND
    refs = (g0_ref, g1_ref, g2_ref)

    def key_lo(g, j):
        nb = T // A_DILATIONS[g] // BAND
        return (j - 1) * BAND if j % nb else j * BAND

    def scores(g, c0):
        ref, nb = refs[g], T // A_DILATIONS[g] // BAND
        js = range(c0, c0 + chunk)
        s = [lax.dot_general(ref[0, 0, hd, j * BAND:(j + 1) * BAND, :],
                             ref[0, 1, hd, key_lo(g, j):(j + 1) * BAND, :], nt,
                             preferred_element_type=F32) for j in js]
        s = [jnp.where(band_mask if j % nb else first_mask, x, -jnp.inf) for j, x in zip(js, s)]
        m = [jnp.max(x, axis=-1, keepdims=True) for x in s]
        return s, m

    def values(g, c0, s, m):
        ref, dil = refs[g], A_DILATIONS[g]
        nb = T // dil // BAND
        js = range(c0, c0 + chunk)
        p = [jnp.exp2(x - y).astype(BF16) for x, y in zip(s, m)]
        o = [jnp.dot(x, jnp.concatenate([ref[0, 2, hd, key_lo(g, j):(j + 1) * BAND, :],
                                         jnp.ones(((j + 1) * BAND - key_lo(g, j), LANES), BF16)],
                                        axis=1),
                     preferred_element_type=F32) for x, j in zip(p, js)]
        for j, oj, mj in zip(js, o, m):
            r, n = divmod(j, nb)
            mj = jnp.broadcast_to(mj, (BAND, LANES))
            oj, lj = oj[:, :LANES], oj[:, LANES:]
            if dil == 1:
                sl = slice(j * BAND, (j + 1) * BAND)
                m1, m2 = mnat[0, sl, :], mnat[1, sl, :]
                mx = jnp.maximum(jnp.maximum(mj, m1), m2)
                e0, e1, e2 = jnp.exp2(mj - mx), jnp.exp2(m1 - mx), jnp.exp2(m2 - mx)
                num = e0 * oj + e1 * onat[0, sl, :] + e2 * onat[1, sl, :]
                den = e0 * lj + e1 * lnat[0, sl, :] + e2 * lnat[1, sl, :]
                o_ref[0, hd, sl, :] = (num / den).astype(BF16)
            elif dil > MAX_ROW_STRIDE:
                inner = dil // MAX_ROW_STRIDE
                r1, r0 = divmod(r, MAX_ROW_STRIDE)
                idx = pl.ds(r1 + n * BAND * inner, BAND, stride=inner)
                stage[0, r0, idx, :] = oj
                stage[1, r0, idx, :] = mj
                stage[2, r0, idx, :] = lj
            else:
                idx = pl.ds(r + n * BAND * dil, BAND, stride=dil)
                onat[g - 1, idx, :] = oj
                mnat[g - 1, idx, :] = mj
                lnat[g - 1, idx, :] = lj

    def second_pass(g):
        for a, nat in enumerate((onat, mnat, lnat)):
            for r0 in range(MAX_ROW_STRIDE):
                nat[g - 1, pl.ds(r0, T // MAX_ROW_STRIDE, stride=MAX_ROW_STRIDE), :] = stage[a, r0]

    order = sorted(range(len(refs)), key=lambda g: -A_DILATIONS[g])
    assert A_DILATIONS[order[-1]] == 1
    work = [(g, c0) for g in order for c0 in range(0, nblk, chunk)]
    nxt = scores(*work[0])
    for w, (g, c0) in enumerate(work):
        cur = nxt
        if w + 1 < len(work):
            nxt = scores(*work[w + 1])
        values(g, c0, *cur)
        last_of_group = w + 1 == len(work) or work[w + 1][0] != g
        if last_of_group and A_DILATIONS[g] > MAX_ROW_STRIDE:
            second_pass(g)


def _attn_a(qkv, chunk=8, heads=2):
    B, T = qkv[0].shape[0], qkv[0].shape[2] * qkv[0].shape[3]
    views = [a.reshape(B, 3, A_HEADS, T, LANES) for a in qkv]
    spec = pl.BlockSpec((1, 3, heads, T, LANES), lambda b, h: (b, 0, h, 0, 0))
    return pl.pallas_call(
        functools.partial(_attn_a_kernel, chunk=chunk),
        grid=(B, A_HEADS // heads),
        in_specs=[spec] * 3,
        out_specs=pl.BlockSpec((1, heads, T, LANES), lambda b, h: (b, h, 0, 0)),
        out_shape=jax.ShapeDtypeStruct((B, A_HEADS, T, LANES), BF16),
        scratch_shapes=[pltpu.VMEM((len(A_DILATIONS) - 1, T, LANES), F32)] * 3
                       + [pltpu.VMEM((3, MAX_ROW_STRIDE, T // MAX_ROW_STRIDE, LANES), F32)],
        compiler_params=_cparams(("parallel", "parallel")),
        name="attn_a",
    )(*views)


def _gated(o_ref, z_ref):
    o = jnp.concatenate([o_ref[0, h] for h in range(o_ref.shape[1])], axis=1)
    z = z_ref[0].astype(F32)
    return (o.astype(F32) * (z * jax.nn.sigmoid(z))).astype(BF16)


def _mid_kernel(o_ref, z_ref, x_ref, cb_ref, sb_ref,
                wout_ref, gpost_ref, gkv_ref, wd_ref, glat_ref, wup_ref,
                gpre_ref, wcq_ref, wz_ref, gq_ref, wqup_ref,
                h1_ref, k_ref, v_ref, q_ref, zb_ref, *, parts):
    tm = x_ref.shape[1]
    tp = tm // parts
    lane = lax.broadcasted_iota(jnp.int32, (tp, LANES), 1)
    one_col = jnp.where(lane == 0, 1.0, 0.0)
    scale = B_QK_DIM ** -0.5 * LOG2E
    st = [dict() for _ in range(parts)]

    def stage1(i):
        rs = slice(i * tp, (i + 1) * tp)
        o = jnp.concatenate([o_ref[0, h, rs, :] for h in range(o_ref.shape[1])], axis=1)
        z = z_ref[0, rs, :].astype(F32)
        g = (o.astype(F32) * (z * jax.nn.sigmoid(z))).astype(BF16)
        y = jnp.dot(g, wout_ref[...], preferred_element_type=F32)
        h1 = x_ref[0, rs, :] + _rms(y, gpost_ref[...])
        h1_ref[0, rs, :] = h1
        st[i]["h1"] = h1

    def stage2(i):
        h1 = st[i].pop("h1")
        hn = _rms(h1, gkv_ref[...]).astype(BF16)
        hb = _rms(h1, gpre_ref[...]).astype(BF16)
        st[i]["ckr"] = jnp.dot(hn, wd_ref[...], preferred_element_type=F32)
        st[i]["cq"] = jnp.dot(hb, wcq_ref[...], preferred_element_type=F32)
        st[i]["hb"] = hb

    def stage3(i):
        rs = slice(i * tp, (i + 1) * tp)
        ckr = st[i].pop("ckr")
        c_kv = _rms(ckr[:, :B_KV_LORA], glat_ref[...]).astype(BF16)
        c_q = _rms(st[i].pop("cq"), gq_ref[...]).astype(BF16)
        st[i]["kv"] = jnp.dot(c_kv, wup_ref[...], preferred_element_type=F32)
        st[i]["qq"] = jnp.dot(c_q, wqup_ref[...], preferred_element_type=F32)
        zb_ref[0, rs, :] = jnp.dot(st[i].pop("hb"), wz_ref[...],
                                   preferred_element_type=F32).astype(BF16)
        st[i]["kr"] = ckr[:, B_KV_LORA:]

    def stage4(i):
        rs = slice(i * tp, (i + 1) * tp)
        cos = jnp.where(lane >= B_NOPE, cb_ref[0, rs, :], 1.0)
        sin = jnp.where(lane >= B_NOPE, sb_ref[0, rs, :], 0.0)
        k_rope = _rot_half(st[i].pop("kr"), B_NOPE, cos, sin)
        kv, qq = st[i].pop("kv"), st[i].pop("qq")
        cq, sq = cos * scale, sin * scale
        for h in range(B_HEADS):
            blk = kv[:, h * LANES:(h + 1) * LANES]
            k_ref[0, h, rs, :] = jnp.where(lane < B_NOPE, blk, k_rope).astype(BF16)
            v_ref[0, h, rs, :] = jnp.where(lane >= B_NOPE, blk, one_col).astype(BF16)
            q_ref[0, h, rs, :] = _rot_half(qq[:, h * LANES:(h + 1) * LANES],
                                           B_NOPE, cq, sq).astype(BF16)

    for stage in (stage1, stage2, stage3, stage4):
        for i in range(parts):
            stage(i)


def _mid(o_a, z_a, x, cos, sin, weights, tm=512, parts=2):
    B, T, D = x.shape
    tok = lambda w: pl.BlockSpec((1, tm, w), lambda b, i: (b, i, 0))
    heads = lambda n: pl.BlockSpec((1, n, tm, LANES), lambda b, i: (b, 0, i, 0))
    const = lambda a: pl.BlockSpec(a.shape, lambda b, i: (0, 0), pipeline_mode=pl.Buffered(1))
    return pl.pallas_call(
        functools.partial(_mid_kernel, parts=parts),
        grid=(B, T // tm),
        in_specs=[heads(A_HEADS), tok(A_WIDTH), tok(D), tok(LANES), tok(LANES)]
                 + [const(w) for w in weights],
        out_specs=[tok(D), heads(B_HEADS), heads(B_HEADS), heads(B_HEADS), tok(B_WIDTH)],
        out_shape=[
            jax.ShapeDtypeStruct((B, T, D), F32),
            jax.ShapeDtypeStruct((B, B_HEADS, T, LANES), BF16),
            jax.ShapeDtypeStruct((B, B_HEADS, T, LANES), BF16),
            jax.ShapeDtypeStruct((B, B_HEADS, T, LANES), BF16),
            jax.ShapeDtypeStruct((B, T, B_WIDTH), BF16),
        ],
        compiler_params=_cparams(("parallel", "parallel")),
        name="mid",
    )(o_a, z_a, x, cos, sin, *weights)


def _attn_b_kernel(q_ref, k_ref, v_ref, o_ref, *, tq, ahead):
    T = k_ref.shape[2]
    qi = lax.broadcasted_iota(jnp.int32, (tq, tq), 0)
    ki = lax.broadcasted_iota(jnp.int32, (tq, tq), 1)
    causal = ki <= qi
    lane = lax.broadcasted_iota(jnp.int32, (tq, LANES), 1)
    nt = (((1,), (1,)), ((), ()))
    units = [(ii, hh) for ii in range(T // tq) for hh in range(2)]

    def scores(ii, hh):
        lo = ii * tq
        q = q_ref[0, hh, lo:lo + tq, :]
        s_d = lax.dot_general(q, k_ref[0, hh, lo:lo + tq, :], nt, preferred_element_type=F32)
        s_d = jnp.where(causal, s_d, -jnp.inf)
        m = jnp.max(s_d, axis=-1, keepdims=True)
        s_m = None
        if ii:
            s_m = lax.dot_general(q, k_ref[0, hh, 0:lo, :], nt, preferred_element_type=F32)
            m = jnp.maximum(m, jnp.max(s_m, axis=-1, keepdims=True))
        return s_d, s_m, m

    def values(ii, hh, s_d, s_m, m):
        lo = ii * tq
        o = jnp.dot(jnp.exp2(s_d - m).astype(BF16), v_ref[0, hh, lo:lo + tq, :],
                    preferred_element_type=F32)
        if ii:
            o = o + jnp.dot(jnp.exp2(s_m - m).astype(BF16), v_ref[0, hh, 0:lo, :],
                            preferred_element_type=F32)
        l = jnp.sum(jnp.where(lane == 0, o, 0.0), axis=-1, keepdims=True)
        return o / l

    pending = [scores(*units[u]) for u in range(ahead)]
    outs = []
    for u, (ii, hh) in enumerate(units):
        cur = pending.pop(0)
        if u + ahead < len(units):
            pending.append(scores(*units[u + ahead]))
        outs.append(values(ii, hh, *cur))
        if hh:
            lo = ii * tq
            o_ref[0, 0, lo:lo + tq, :] = jnp.where(
                lane < B_VDIM, pltpu.roll(outs[0], B_VDIM, 1), outs[1]).astype(BF16)
            outs = []


def _attn_b(q, k, v, tq=256, ahead=3):
    B, _, T, _ = q.shape
    heads = pl.BlockSpec((1, 2, T, LANES), lambda b, p: (b, p, 0, 0))
    return pl.pallas_call(
        functools.partial(_attn_b_kernel, tq=tq, ahead=ahead),
        grid=(B, B_PAIRS),
        in_specs=[heads, heads, heads],
        out_specs=pl.BlockSpec((1, 1, T, LANES), lambda b, p: (b, p, 0, 0)),
        out_shape=jax.ShapeDtypeStruct((B, B_PAIRS, T, LANES), BF16),
        compiler_params=_cparams(("parallel", "parallel")),
        name="attn_b",
    )(q, k, v)


def _final_kernel(o_ref, z_ref, h1_ref, w_ref, g_ref, out_ref):
    y = jnp.dot(_gated(o_ref, z_ref), w_ref[...], preferred_element_type=F32)
    out_ref[0] = h1_ref[0] + _rms(y, g_ref[...])


def _final(o_b, z_b, h1, w, gain, tm=1024):
    B, T, D = h1.shape
    tok = lambda w_: pl.BlockSpec((1, tm, w_), lambda b, i: (b, i, 0))
    return pl.pallas_call(
        _final_kernel,
        grid=(B, T // tm),
        in_specs=[pl.BlockSpec((1, B_PAIRS, tm, LANES), lambda b, i: (b, 0, i, 0)),
                  tok(B_WIDTH), tok(D),
                  pl.BlockSpec(w.shape, lambda b, i: (0, 0)),
                  pl.BlockSpec((1, D), lambda b, i: (0, 0))],
        out_specs=tok(D),
        out_shape=jax.ShapeDtypeStruct((B, T, D), F32),
        compiler_params=_cparams(("parallel", "parallel")),
        name="final",
    )(o_b, z_b, h1, w, gain.reshape(1, D))


def kernel(x, positions, a_pre_norm, a_w_in, a_w_out, a_post_norm, kv_norm, kv_w_down, kv_latent_norm, kv_w_up, b_pre_norm, b_w_in, b_q_norm, b_w_q_up, b_w_out, b_post_norm):
    D = x.shape[-1]
    cos, sin = _rope_tables(positions)

    w_in = a_w_in[0].astype(BF16)
    qkv = []
    z_a = None
    for g, dil in enumerate(A_DILATIONS):
        if dil == 1:
            qkv_g, z_a = _proj_a(x, cos, sin, a_pre_norm[0], w_in, g, True)
        else:
            (qkv_g,) = _proj_a(x, cos, sin, a_pre_norm[0], w_in, g, False, parts=2)
        qkv.append(qkv_g)
    o_a = _attn_a(qkv)

    row = lambda g: g.reshape(1, -1)
    wd = jnp.zeros((D, B_KV_LORA + LANES), F32)
    wd = wd.at[:, :B_KV_LORA].set(kv_w_down[:, :B_KV_LORA])
    wd = wd.at[:, B_KV_LORA + B_NOPE:B_KV_LORA + B_QK_DIM].set(kv_w_down[:, B_KV_LORA:])
    wqup = jnp.pad(b_w_q_up[0].reshape(B_Q_LORA, B_HEADS, B_QK_DIM),
                   ((0, 0), (0, 0), (0, LANES - B_QK_DIM))).reshape(B_Q_LORA, B_HEADS * LANES)
    weights = [
        a_w_out[0].astype(BF16), row(a_post_norm[0]), row(kv_norm), wd.astype(BF16),
        row(kv_latent_norm), kv_w_up.astype(BF16),
        row(b_pre_norm[0]), b_w_in[0][:, :B_Q_LORA].astype(BF16),
        b_w_in[0][:, B_Q_LORA:].astype(BF16), row(b_q_norm[0]), wqup.astype(BF16),
    ]
    h1, k_b, v_b, q_b, z_b = _mid(o_a, z_a, x, cos, sin, weights)

    o_b = _attn_b(q_b, k_b, v_b)
    return _final(o_b, z_b, h1, b_w_out[0].astype(BF16), b_post_norm[0])
```

```python
import functools

import jax
import jax.numpy as jnp
from jax import lax
from jax.experimental import pallas as pl
from jax.experimental.pallas import tpu as pltpu

F32 = jnp.float32
BF16 = jnp.bfloat16

NORM_EPS = 1e-6
LOG2E = 1.4426950408889634
LANES = 128

A_WINDOWS = (128, 512, 2048)
A_DILATIONS = (1, 4, 16)
A_HEADS = 8
A_HEAD_DIM = 128
A_WIDTH = A_HEADS * A_HEAD_DIM
A_ROT_DIM = A_HEAD_DIM // 4
A_ROPE_THETA = 500000.0
BAND = 128
MAX_ROW_STRIDE = 4

B_HEADS = 16
B_NOPE = 64
B_ROPE = 32
B_QK_DIM = B_NOPE + B_ROPE
B_VDIM = 64
B_WIDTH = B_HEADS * B_VDIM
B_Q_LORA = 384
B_KV_LORA = 256
B_ROPE_THETA = 10000.0
B_PAIRS = B_HEADS // 2

VMEM_LIMIT = 56 * 1024 * 1024


def _cparams(sem):
    return pltpu.CompilerParams(dimension_semantics=sem, vmem_limit_bytes=VMEM_LIMIT)


def _rms(x, g):
    ms = jnp.mean(x * x, axis=-1, keepdims=True)
    return x * lax.rsqrt(ms + NORM_EPS) * g


def _rot_half(a, lo, cos, sin):
    lane = lax.broadcasted_iota(jnp.int32, a.shape, 1)
    partner = jnp.where(lane < lo + 16, pltpu.roll(a, LANES - 16, 1), pltpu.roll(a, 16, 1))
    return a * cos + partner * sin


def _rope_table_kernel(pe_ref, po_ref, freq_ref, sign_ref, cos_ref, sin_ref):
    half_t = pe_ref.shape[1]
    lane = lax.broadcasted_iota(jnp.int32, (half_t, LANES), 1)
    pos = jnp.where(lane < LANES // 2, pe_ref[0].astype(F32), po_ref[0].astype(F32))
    ang = pos * freq_ref[...]
    for ref, packed, ident in ((cos_ref, jnp.cos(ang), 1.0),
                               (sin_ref, jnp.sin(ang) * sign_ref[...], 0.0)):
        def unpack(lo0, lo1):
            a = packed if lo0 == 0 else pltpu.roll(packed, LANES - lo0, 1)
            b_ = pltpu.roll(packed, (B_NOPE - lo1) % LANES, 1)
            return jnp.where(lane < A_ROT_DIM, a,
                             jnp.where((lane >= B_NOPE) & (lane < B_QK_DIM), b_, ident))
        ref[0, pl.ds(0, half_t, stride=2), :] = unpack(0, A_ROT_DIM)
        ref[0, pl.ds(1, half_t, stride=2), :] = unpack(LANES // 2, LANES // 2 + A_ROT_DIM)


def _rope_tables(positions):
    B, T = positions.shape
    half = A_ROT_DIM // 2
    inv_a = 1.0 / (A_ROPE_THETA ** (jnp.arange(half, dtype=F32) * (2.0 / A_ROT_DIM)))
    inv_b = 1.0 / (B_ROPE_THETA ** (jnp.arange(B_ROPE // 2, dtype=F32) * (2.0 / B_ROPE)))
    ones = jnp.ones((half,), F32)
    freq = jnp.tile(jnp.concatenate([inv_a, inv_a, inv_b, inv_b]), 2).reshape(1, LANES)
    sign = jnp.tile(jnp.concatenate([-ones, ones, -ones, ones]), 2).reshape(1, LANES)
    tab = jax.ShapeDtypeStruct((B, T, LANES), F32)
    spec = pl.BlockSpec((1, T, LANES), lambda b: (b, 0, 0))
    const = pl.BlockSpec((1, LANES), lambda b: (0, 0))
    pos = pl.BlockSpec((1, T // 2, 1), lambda b: (b, 0, 0))
    return pl.pallas_call(
        _rope_table_kernel,
        grid=(B,),
        in_specs=[pos, pos, const, const],
        out_specs=[spec] * 2,
        out_shape=[tab] * 2,
        compiler_params=_cparams(("parallel",)),
        name="rope_tables",
    )(positions[:, 0::2].reshape(B, T // 2, 1), positions[:, 1::2].reshape(B, T // 2, 1),
      freq, sign)


def _proj_a_kernel(x_ref, c_ref, s_ref, g_ref, w_ref, *rest, dil, with_z, parts):
    if with_z:
        wz_ref, qkv_ref, z_ref = rest[:3]
    else:
        qkv_ref = rest[0]
    tm_all, D = x_ref.shape[1], x_ref.shape[2]
    tm = tm_all // parts
    tl = tm // dil
    nslab = D // LANES

    def by_class(load, stage):
        if dil <= MAX_ROW_STRIDE:
            return jnp.concatenate([load(pl.ds(r, tl, stride=dil)) for r in range(dil)], axis=0)
        inner = dil // MAX_ROW_STRIDE
        tq = tm // MAX_ROW_STRIDE
        for q0 in range(MAX_ROW_STRIDE):
            stage[q0 * tq:(q0 + 1) * tq, :] = load(pl.ds(q0, tq, stride=MAX_ROW_STRIDE))
        return jnp.concatenate(
            [stage[pl.ds(q0 * tq + q1, tl, stride=inner), :]
             for q1 in range(inner) for q0 in range(MAX_ROW_STRIDE)], axis=0)

    def inputs(i):
        rs = slice(i * tm, (i + 1) * tm)
        if dil == 1:
            return x_ref[0, rs, :], c_ref[0, rs, :], s_ref[0, rs, :]
        slabs, stage = rest[-2].at[i], rest[-1].at[i]
        for c in range(nslab):
            slabs[c] = x_ref[0, rs, c * LANES:(c + 1) * LANES]
        slabs[nslab] = c_ref[0, rs, :]
        slabs[nslab + 1] = s_ref[0, rs, :]
        p = [by_class(lambda idx, c=c: slabs[c, idx, :], stage.at[c]) for c in range(nslab + 2)]
        return jnp.concatenate(p[:nslab], axis=1), p[nslab], p[nslab + 1]

    lane = lax.broadcasted_iota(jnp.int32, (tm, LANES), 1)
    scale = A_HEAD_DIM ** -0.5 * LOG2E

    def project(i, x, cos, sin):
        h = _rms(x, g_ref[...]).astype(BF16)
        cos = jnp.where(lane < A_ROT_DIM, cos, 1.0)
        sin = jnp.where(lane < A_ROT_DIM, sin, 0.0)
        for s in range(3):
            acc = jnp.dot(h, w_ref[:, s * A_WIDTH:(s + 1) * A_WIDTH],
                          preferred_element_type=F32)
            for hd in range(A_HEADS):
                a = acc[:, hd * LANES:(hd + 1) * LANES]
                if s == 0:
                    a = _rot_half(a, 0, cos * scale, sin * scale)
                elif s == 1:
                    a = _rot_half(a, 0, cos, sin)
                a = a.astype(BF16)
                for r in range(dil):
                    qkv_ref[0, s * A_HEADS + hd, r, i * tl:(i + 1) * tl, :] = a[r * tl:(r + 1) * tl]
        if with_z:
            z_ref[0, i * tm:(i + 1) * tm, :] = jnp.dot(
                h, wz_ref[...], preferred_element_type=F32).astype(BF16)

    staged = [inputs(i) for i in range(parts)]
    for i in range(parts):
        project(i, *staged[i])


def _proj_a(x, cos, sin, gain, w, group, with_z, tm=512, parts=2):
    B, T, D = x.shape
    dil = A_DILATIONS[group]
    L = T // dil
    tm = tm * parts
    tl = tm // dil
    tok = lambda n: pl.BlockSpec((1, tm, n), lambda b, i: (b, i, 0))
    out_shape = [jax.ShapeDtypeStruct((B, 3 * A_HEADS, dil, L, LANES), BF16)]
    out_specs = [pl.BlockSpec((1, 3 * A_HEADS, dil, tl, LANES), lambda b, i: (b, 0, 0, i, 0))]
    if with_z:
        out_shape.append(jax.ShapeDtypeStruct((B, T, A_WIDTH), BF16))
        out_specs.append(tok(A_WIDTH))
    return pl.pallas_call(
        functools.partial(_proj_a_kernel, dil=dil, with_z=with_z, parts=parts),
        grid=(B, T // tm),
        in_specs=[tok(D), tok(LANES), tok(LANES),
                  pl.BlockSpec((1, D), lambda b, i: (0, 0)),
                  pl.BlockSpec((D, 3 * A_WIDTH), lambda b, i: (0, group))]
                 + ([pl.BlockSpec((D, A_WIDTH), lambda b, i: (0, 3 * len(A_DILATIONS)))]
                    if with_z else []),
        out_specs=out_specs,
        out_shape=out_shape,
        scratch_shapes=([pltpu.VMEM((parts, D // LANES + 2, tm // parts, LANES), F32)] * 2
                        if dil > 1 else []),
        compiler_params=_cparams(("parallel", "parallel")),
        name=f"proj_a_d{dil}",
    )(x, cos, sin, gain.reshape(1, D), *([w, w] if with_z else [w]))


def _attn_a_kernel(g0_ref, g1_ref, g2_ref, o_ref, onat, mnat, lnat, stage, *, chunk):
    for hd in range(o_ref.shape[1]):
        _attn_a_head(hd, g0_ref, g1_ref, g2_ref, o_ref, onat, mnat, lnat, stage, chunk)


def _attn_a_head(hd, g0_ref, g1_ref, g2_ref, o_ref, onat, mnat, lnat, stage, chunk):
    T = o_ref.shape[2]
    nt = (((1,), (1,)), ((), ()))
    qi = lax.broadcasted_iota(jnp.int32, (BAND, 2 * BAND), 0)
    ki = lax.broadcasted_iota(jnp.int32, (BAND, 2 * BAND), 1)
    band_mask = (ki >= qi) & (ki <= qi + BAND)
    first_mask = (lax.broadcasted_iota(jnp.int32, (BAND, BAND), 1)
                  <= lax.broadcasted_iota(jnp.int32, (BAND, BAND), 0))
    nblk = T // BAND
    refs = (g0_ref, g1_ref, g2_ref)

    def key_lo(g, j):
        nb = T // A_DILATIONS[g] // BAND
        return (j - 1) * BAND if j % nb else j * BAND

    def scores(g, c0):
        ref, nb = refs[g], T // A_DILATIONS[g] // BAND
        js = range(c0, c0 + chunk)
        s = [lax.dot_general(ref[0, 0, hd, j * BAND:(j + 1) * BAND, :],
                             ref[0, 1, hd, key_lo(g, j):(j + 1) * BAND, :], nt,
                             preferred_element_type=F32) for j in js]
        s = [jnp.where(band_mask if j % nb else first_mask, x, -jnp.inf) for j, x in zip(js, s)]
        m = [jnp.max(x, axis=-1, keepdims=True) for x in s]
        return s, m

    def values(g, c0, s, m):
        ref, dil = refs[g], A_DILATIONS[g]
        nb = T // dil // BAND
        js = range(c0, c0 + chunk)
        p = [jnp.exp2(x - y).astype(BF16) for x, y in zip(s, m)]
        o = [jnp.dot(x, jnp.concatenate([ref[0, 2, hd, key_lo(g, j):(j + 1) * BAND, :],
                                         jnp.ones(((j + 1) * BAND - key_lo(g, j), LANES), BF16)],
                                        axis=1),
                     preferred_element_type=F32) for x, j in zip(p, js)]
        for j, oj, mj in zip(js, o, m):
            r, n = divmod(j, nb)
            mj = jnp.broadcast_to(mj, (BAND, LANES))
            oj, lj = oj[:, :LANES], oj[:, LANES:]
            if dil == 1:
                sl = slice(j * BAND, (j + 1) * BAND)
                m1, m2 = mnat[0, sl, :], mnat[1, sl, :]
                mx = jnp.maximum(jnp.maximum(mj, m1), m2)
                e0, e1, e2 = jnp.exp2(mj - mx), jnp.exp2(m1 - mx), jnp.exp2(m2 - mx)
                num = e0 * oj + e1 * onat[0, sl, :] + e2 * onat[1, sl, :]
                den = e0 * lj + e1 * lnat[0, sl, :] + e2 * lnat[1, sl, :]
                o_ref[0, hd, sl, :] = (num / den).astype(BF16)
            elif dil > MAX_ROW_STRIDE:
                inner = dil // MAX_ROW_STRIDE
                r1, r0 = divmod(r, MAX_ROW_STRIDE)
                idx = pl.ds(r1 + n * BAND * inner, BAND, stride=inner)
                stage[0, r0, idx, :] = oj
                stage[1, r0, idx, :] = mj
                stage[2, r0, idx, :] = lj
            else:
                idx = pl.ds(r + n * BAND * dil, BAND, stride=dil)
                onat[g - 1, idx, :] = oj
                mnat[g - 1, idx, :] = mj
                lnat[g - 1, idx, :] = lj

    def second_pass(g):
        for a, nat in enumerate((onat, mnat, lnat)):
            for r0 in range(MAX_ROW_STRIDE):
                nat[g - 1, pl.ds(r0, T // MAX_ROW_STRIDE, stride=MAX_ROW_STRIDE), :] = stage[a, r0]

    order = sorted(range(len(refs)), key=lambda g: -A_DILATIONS[g])
    assert A_DILATIONS[order[-1]] == 1
    work = [(g, c0) for g in order for c0 in range(0, nblk, chunk)]
    nxt = scores(*work[0])
    for w, (g, c0) in enumerate(work):
        cur = nxt
        if w + 1 < len(work):
            nxt = scores(*work[w + 1])
        values(g, c0, *cur)
        last_of_group = w + 1 == len(work) or work[w + 1][0] != g
        if last_of_group and A_DILATIONS[g] > MAX_ROW_STRIDE:
            second_pass(g)


def _attn_a(qkv, chunk=8, heads=2):
    B, T = qkv[0].shape[0], qkv[0].shape[2] * qkv[0].shape[3]
    views = [a.reshape(B, 3, A_HEADS, T, LANES) for a in qkv]
    spec = pl.BlockSpec((1, 3, heads, T, LANES), lambda b, h: (b, 0, h, 0, 0))
    return pl.pallas_call(
        functools.partial(_attn_a_kernel, chunk=chunk),
        grid=(B, A_HEADS // heads),
        in_specs=[spec] * 3,
        out_specs=pl.BlockSpec((1, heads, T, LANES), lambda b, h: (b, h, 0, 0)),
        out_shape=jax.ShapeDtypeStruct((B, A_HEADS, T, LANES), BF16),
        scratch_shapes=[pltpu.VMEM((len(A_DILATIONS) - 1, T, LANES), F32)] * 3
                       + [pltpu.VMEM((3, MAX_ROW_STRIDE, T // MAX_ROW_STRIDE, LANES), F32)],
        compiler_params=_cparams(("parallel", "parallel")),
        name="attn_a",
    )(*views)


def _gated(o_ref, z_ref):
    o = jnp.concatenate([o_ref[0, h] for h in range(o_ref.shape[1])], axis=1)
    z = z_ref[0].astype(F32)
    return (o.astype(F32) * (z * jax.nn.sigmoid(z))).astype(BF16)


def _mid_kernel(o_ref, z_ref, x_ref, cb_ref, sb_ref,
                wout_ref, gpost_ref, gkv_ref, wd_ref, glat_ref, wup_ref,
                gpre_ref, wcq_ref, wz_ref, gq_ref, wqup_ref,
                h1_ref, k_ref, v_ref, q_ref, zb_ref, *, parts):
    tm = x_ref.shape[1]
    tp = tm // parts
    lane = lax.broadcasted_iota(jnp.int32, (tp, LANES), 1)
    one_col = jnp.where(lane == 0, 1.0, 0.0)
    scale = B_QK_DIM ** -0.5 * LOG2E
    st = [dict() for _ in range(parts)]

    def stage1(i):
        rs = slice(i * tp, (i + 1) * tp)
        o = jnp.concatenate([o_ref[0, h, rs, :] for h in range(o_ref.shape[1])], axis=1)
        z = z_ref[0, rs, :].astype(F32)
        g = (o.astype(F32) * (z * jax.nn.sigmoid(z))).astype(BF16)
        y = jnp.dot(g, wout_ref[...], preferred_element_type=F32)
        h1 = x_ref[0, rs, :] + _rms(y, gpost_ref[...])
        h1_ref[0, rs, :] = h1
        st[i]["h1"] = h1

    def stage2(i):
        h1 = st[i].pop("h1")
        hn = _rms(h1, gkv_ref[...]).astype(BF16)
        hb = _rms(h1, gpre_ref[...]).astype(BF16)
        st[i]["ckr"] = jnp.dot(hn, wd_ref[...], preferred_element_type=F32)
        st[i]["cq"] = jnp.dot(hb, wcq_ref[...], preferred_element_type=F32)
        st[i]["hb"] = hb

    def stage3(i):
        rs = slice(i * tp, (i + 1) * tp)
        ckr = st[i].pop("ckr")
        c_kv = _rms(ckr[:, :B_KV_LORA], glat_ref[...]).astype(BF16)
        c_q = _rms(st[i].pop("cq"), gq_ref[...]).astype(BF16)
        st[i]["kv"] = jnp.dot(c_kv, wup_ref[...], preferred_element_type=F32)
        st[i]["qq"] = jnp.dot(c_q, wqup_ref[...], preferred_element_type=F32)
        zb_ref[0, rs, :] = jnp.dot(st[i].pop("hb"), wz_ref[...],
                                   preferred_element_type=F32).astype(BF16)
        st[i]["kr"] = ckr[:, B_KV_LORA:]

    def stage4(i):
        rs = slice(i * tp, (i + 1) * tp)
        cos = jnp.where(lane >= B_NOPE, cb_ref[0, rs, :], 1.0)
        sin = jnp.where(lane >= B_NOPE, sb_ref[0, rs, :], 0.0)
        k_rope = _rot_half(st[i].pop("kr"), B_NOPE, cos, sin)
        kv, qq = st[i].pop("kv"), st[i].pop("qq")
        cq, sq = cos * scale, sin * scale
        for h in range(B_HEADS):
            blk = kv[:, h * LANES:(h + 1) * LANES]
            k_ref[0, h, rs, :] = jnp.where(lane < B_NOPE, blk, k_rope).astype(BF16)
            v_ref[0, h, rs, :] = jnp.where(lane >= B_NOPE, blk, one_col).astype(BF16)
            q_ref[0, h, rs, :] = _rot_half(qq[:, h * LANES:(h + 1) * LANES],
                                           B_NOPE, cq, sq).astype(BF16)

    for stage in (stage1, stage2, stage3, stage4):
        for i in range(parts):
            stage(i)


def _mid(o_a, z_a, x, cos, sin, weights, tm=512, parts=2):
    B, T, D = x.shape
    tok = lambda w: pl.BlockSpec((1, tm, w), lambda b, i: (b, i, 0))
    heads = lambda n: pl.BlockSpec((1, n, tm, LANES), lambda b, i: (b, 0, i, 0))
    const = lambda a: pl.BlockSpec(a.shape, lambda b, i: (0, 0), pipeline_mode=pl.Buffered(1))
    return pl.pallas_call(
        functools.partial(_mid_kernel, parts=parts),
        grid=(B, T // tm),
        in_specs=[heads(A_HEADS), tok(A_WIDTH), tok(D), tok(LANES), tok(LANES)]
                 + [const(w) for w in weights],
        out_specs=[tok(D), heads(B_HEADS), heads(B_HEADS), heads(B_HEADS), tok(B_WIDTH)],
        out_shape=[
            jax.ShapeDtypeStruct((B, T, D), F32),
            jax.ShapeDtypeStruct((B, B_HEADS, T, LANES), BF16),
            jax.ShapeDtypeStruct((B, B_HEADS, T, LANES), BF16),
            jax.ShapeDtypeStruct((B, B_HEADS, T, LANES), BF16),
            jax.ShapeDtypeStruct((B, T, B_WIDTH), BF16),
        ],
        compiler_params=_cparams(("parallel", "parallel")),
        name="mid",
    )(o_a, z_a, x, cos, sin, *weights)


def _attn_b_kernel(q_ref, k_ref, v_ref, o_ref, *, tq, ahead):
    T = k_ref.shape[2]
    qi = lax.broadcasted_iota(jnp.int32, (tq, tq), 0)
    ki = lax.broadcasted_iota(jnp.int32, (tq, tq), 1)
    causal = ki <= qi
    lane = lax.broadcasted_iota(jnp.int32, (tq, LANES), 1)
    nt = (((1,), (1,)), ((), ()))
    units = [(ii, hh) for ii in range(T // tq) for hh in range(2)]

    def scores(ii, hh):
        lo = ii * tq
        q = q_ref[0, hh, lo:lo + tq, :]
        s_d = lax.dot_general(q, k_ref[0, hh, lo:lo + tq, :], nt, preferred_element_type=F32)
        s_d = jnp.where(causal, s_d, -jnp.inf)
        m = jnp.max(s_d, axis=-1, keepdims=True)
        s_m = None
        if ii:
            s_m = lax.dot_general(q, k_ref[0, hh, 0:lo, :], nt, preferred_element_type=F32)
            m = jnp.maximum(m, jnp.max(s_m, axis=-1, keepdims=True))
        return s_d, s_m, m

    def values(ii, hh, s_d, s_m, m):
        lo = ii * tq
        o = jnp.dot(jnp.exp2(s_d - m).astype(BF16), v_ref[0, hh, lo:lo + tq, :],
                    preferred_element_type=F32)
        if ii:
            o = o + jnp.dot(jnp.exp2(s_m - m).astype(BF16), v_ref[0, hh, 0:lo, :],
                            preferred_element_type=F32)
        l = jnp.sum(jnp.where(lane == 0, o, 0.0), axis=-1, keepdims=True)
        return o / l

    pending = [scores(*units[u]) for u in range(ahead)]
    outs = []
    for u, (ii, hh) in enumerate(units):
        cur = pending.pop(0)
        if u + ahead < len(units):
            pending.append(scores(*units[u + ahead]))
        outs.append(values(ii, hh, *cur))
        if hh:
            lo = ii * tq
            o_ref[0, 0, lo:lo + tq, :] = jnp.where(
                lane < B_VDIM, pltpu.roll(outs[0], B_VDIM, 1), outs[1]).astype(BF16)
            outs = []


def _attn_b(q, k, v, tq=256, ahead=3):
    B, _, T, _ = q.shape
    heads = pl.BlockSpec((1, 2, T, LANES), lambda b, p: (b, p, 0, 0))
    return pl.pallas_call(
        functools.partial(_attn_b_kernel, tq=tq, ahead=ahead),
        grid=(B, B_PAIRS),
        in_specs=[heads, heads, heads],
        out_specs=pl.BlockSpec((1, 1, T, LANES), lambda b, p: (b, p, 0, 0)),
        out_shape=jax.ShapeDtypeStruct((B, B_PAIRS, T, LANES), BF16),
        compiler_params=_cparams(("parallel", "parallel")),
        name="attn_b",
    )(q, k, v)


def _final_kernel(o_ref, z_ref, h1_ref, w_ref, g_ref, out_ref):
    y = jnp.dot(_gated(o_ref, z_ref), w_ref[...], preferred_element_type=F32)
    out_ref[0] = h1_ref[0] + _rms(y, g_ref[...])


def _final(o_b, z_b, h1, w, gain, tm=1024):
    B, T, D = h1.shape
    tok = lambda w_: pl.BlockSpec((1, tm, w_), lambda b, i: (b, i, 0))
    return pl.pallas_call(
        _final_kernel,
        grid=(B, T // tm),
        in_specs=[pl.BlockSpec((1, B_PAIRS, tm, LANES), lambda b, i: (b, 0, i, 0)),
                  tok(B_WIDTH), tok(D),
                  pl.BlockSpec(w.shape, lambda b, i: (0, 0)),
                  pl.BlockSpec((1, D), lambda b, i: (0, 0))],
        out_specs=tok(D),
        out_shape=jax.ShapeDtypeStruct((B, T, D), F32),
        compiler_params=_cparams(("parallel", "parallel")),
        name="final",
    )(o_b, z_b, h1, w, gain.reshape(1, D))


def kernel(x, positions, a_pre_norm, a_w_in, a_w_out, a_post_norm, kv_norm, kv_w_down, kv_latent_norm, kv_w_up, b_pre_norm, b_w_in, b_q_norm, b_w_q_up, b_w_out, b_post_norm):
    D = x.shape[-1]
    cos, sin = _rope_tables(positions)

    w_in = a_w_in[0].astype(BF16)
    qkv = []
    z_a = None
    for g, dil in enumerate(A_DILATIONS):
        if dil == 1:
            qkv_g, z_a = _proj_a(x, cos, sin, a_pre_norm[0], w_in, g, True)
        else:
            (qkv_g,) = _proj_a(x, cos, sin, a_pre_norm[0], w_in, g, False)
        qkv.append(qkv_g)
    o_a = _attn_a(qkv)

    row = lambda g: g.reshape(1, -1)
    wd = jnp.zeros((D, B_KV_LORA + LANES), F32)
    wd = wd.at[:, :B_KV_LORA].set(kv_w_down[:, :B_KV_LORA])
    wd = wd.at[:, B_KV_LORA + B_NOPE:B_KV_LORA + B_QK_DIM].set(kv_w_down[:, B_KV_LORA:])
    wqup = jnp.pad(b_w_q_up[0].reshape(B_Q_LORA, B_HEADS, B_QK_DIM),
                   ((0, 0), (0, 0), (0, LANES - B_QK_DIM))).reshape(B_Q_LORA, B_HEADS * LANES)
    weights = [
        a_w_out[0].astype(BF16), row(a_post_norm[0]), row(kv_norm), wd.astype(BF16),
        row(kv_latent_norm), kv_w_up.astype(BF16),
        row(b_pre_norm[0]), b_w_in[0][:, :B_Q_LORA].astype(BF16),
        b_w_in[0][:, B_Q_LORA:].astype(BF16), row(b_q_norm[0]), wqup.astype(BF16),
    ]
    h1, k_b, v_b, q_b, z_b = _mid(o_a, z_a, x, cos, sin, weights)

    o_b = _attn_b(q_b, k_b, v_b)
    return _final(o_b, z_b, h1, b_w_out[0].astype(BF16), b_post_norm[0])
```

```python
import functools

import jax
import jax.numpy as jnp
from jax import lax
from jax.experimental import pallas as pl
from jax.experimental.pallas import tpu as pltpu

F32 = jnp.float32
BF16 = jnp.bfloat16

NORM_EPS = 1e-6
LOG2E = 1.4426950408889634
LANES = 128

A_WINDOWS = (128, 512, 2048)
A_DILATIONS = (1, 4, 16)
A_HEADS = 8
A_HEAD_DIM = 128
A_WIDTH = A_HEADS * A_HEAD_DIM
A_ROT_DIM = A_HEAD_DIM // 4
A_ROPE_THETA = 500000.0
BAND = 128
MAX_ROW_STRIDE = 4

B_HEADS = 16
B_NOPE = 64
B_ROPE = 32
B_QK_DIM = B_NOPE + B_ROPE
B_VDIM = 64
B_WIDTH = B_HEADS * B_VDIM
B_Q_LORA = 384
B_KV_LORA = 256
B_ROPE_THETA = 10000.0
B_PAIRS = B_HEADS // 2

VMEM_LIMIT = 56 * 1024 * 1024


def _cparams(sem):
    return pltpu.CompilerParams(dimension_semantics=sem, vmem_limit_bytes=VMEM_LIMIT)


def _rms(x, g):
    ms = jnp.mean(x * x, axis=-1, keepdims=True)
    return x * lax.rsqrt(ms + NORM_EPS) * g


def _rot_half(a, lo, cos, sin):
    lane = lax.broadcasted_iota(jnp.int32, a.shape, 1)
    partner = jnp.where(lane < lo + 16, pltpu.roll(a, LANES - 16, 1), pltpu.roll(a, 16, 1))
    return a * cos + partner * sin


def _rope_table_kernel(pe_ref, po_ref, freq_ref, sign_ref, cos_ref, sin_ref):
    half_t = pe_ref.shape[1]
    lane = lax.broadcasted_iota(jnp.int32, (half_t, LANES), 1)
    pos = jnp.where(lane < LANES // 2, pe_ref[0].astype(F32), po_ref[0].astype(F32))
    ang = pos * freq_ref[...]
    for ref, packed, ident in ((cos_ref, jnp.cos(ang), 1.0),
                               (sin_ref, jnp.sin(ang) * sign_ref[...], 0.0)):
        def unpack(lo0, lo1):
            a = packed if lo0 == 0 else pltpu.roll(packed, LANES - lo0, 1)
            b_ = pltpu.roll(packed, (B_NOPE - lo1) % LANES, 1)
            return jnp.where(lane < A_ROT_DIM, a,
                             jnp.where((lane >= B_NOPE) & (lane < B_QK_DIM), b_, ident))
        ref[0, pl.ds(0, half_t, stride=2), :] = unpack(0, A_ROT_DIM)
        ref[0, pl.ds(1, half_t, stride=2), :] = unpack(LANES // 2, LANES // 2 + A_ROT_DIM)


def _rope_lane_rows():
    half = A_ROT_DIM // 2
    inv_a = 1.0 / (A_ROPE_THETA ** (jnp.arange(half, dtype=F32) * (2.0 / A_ROT_DIM)))
    inv_b = 1.0 / (B_ROPE_THETA ** (jnp.arange(B_ROPE // 2, dtype=F32) * (2.0 / B_ROPE)))
    ones = jnp.ones((half,), F32)
    zeros = lambda n: jnp.zeros((n,), F32)
    row = lambda v: v.reshape(1, LANES)
    packed = (row(jnp.tile(jnp.concatenate([inv_a, inv_a, inv_b, inv_b]), 2)),
              row(jnp.tile(jnp.concatenate([-ones, ones, -ones, ones]), 2)))
    table = (row(jnp.concatenate([inv_a, inv_a, zeros(B_NOPE - A_ROT_DIM), inv_b, inv_b,
                                  zeros(LANES - B_QK_DIM)])),
             row(jnp.concatenate([-ones, ones, zeros(B_NOPE - A_ROT_DIM), -ones, ones,
                                  zeros(LANES - B_QK_DIM)])))
    return packed, table


def _rope_tables_direct(positions):
    B, T = positions.shape
    (freq, sign), _ = _rope_lane_rows()
    tab = jax.ShapeDtypeStruct((B, T, LANES), F32)
    spec = pl.BlockSpec((1, T, LANES), lambda b: (b, 0, 0))
    const = pl.BlockSpec((1, LANES), lambda b: (0, 0))
    pos = pl.BlockSpec((1, T // 2, 1), lambda b: (b, 0, 0))
    return pl.pallas_call(
        _rope_table_kernel,
        grid=(B,),
        in_specs=[pos, pos, const, const],
        out_specs=[spec] * 2,
        out_shape=[tab] * 2,
        compiler_params=_cparams(("parallel",)),
        name="rope_tables",
    )(positions[:, 0::2].reshape(B, T // 2, 1), positions[:, 1::2].reshape(B, T // 2, 1),
      freq, sign)


def _rope_shift_kernel(off_ref, freq_ref, sign_ref, ct_ref, st_ref, cos_ref, sin_ref):
    ang = off_ref[0].astype(F32) * freq_ref[...]
    co, so = jnp.cos(ang), jnp.sin(ang) * sign_ref[...]
    ct, st = ct_ref[0], st_ref[0]
    cos_ref[0] = co * ct - so * st
    sin_ref[0] = so * ct + co * st


def _rope_tables(positions):
    B, T = positions.shape
    steps = jnp.arange(T, dtype=positions.dtype)[None, :]

    def shifted(pos):
        ct, st = _rope_tables_direct(steps)
        _, (freq, sign) = _rope_lane_rows()
        tab = jax.ShapeDtypeStruct((B, T, LANES), F32)
        spec = pl.BlockSpec((1, T, LANES), lambda b: (b, 0, 0))
        base = pl.BlockSpec((1, T, LANES), lambda b: (0, 0, 0))
        const = pl.BlockSpec((1, LANES), lambda b: (0, 0))
        return tuple(pl.pallas_call(
            _rope_shift_kernel,
            grid=(B,),
            in_specs=[pl.BlockSpec((1, 1, 1), lambda b: (b, 0, 0)), const, const, base, base],
            out_specs=[spec] * 2,
            out_shape=[tab] * 2,
            compiler_params=_cparams(("parallel",)),
            name="rope_shift",
        )(pos[:, :1].reshape(B, 1, 1), freq, sign, ct, st))

    contiguous = jnp.all(positions == positions[:, :1] + steps)
    return lax.cond(contiguous, shifted, lambda pos: tuple(_rope_tables_direct(pos)), positions)


def _proj_a_kernel(x_ref, c_ref, s_ref, g_ref, w_ref, *rest, dil, with_z, parts):
    if with_z:
        wz_ref, qkv_ref, z_ref = rest[:3]
    else:
        qkv_ref = rest[0]
    tm_all, D = x_ref.shape[1], x_ref.shape[2]
    tm = tm_all // parts
    tl = tm // dil
    nslab = D // LANES

    def by_class(load, stage):
        if dil <= MAX_ROW_STRIDE:
            return jnp.concatenate([load(pl.ds(r, tl, stride=dil)) for r in range(dil)], axis=0)
        inner = dil // MAX_ROW_STRIDE
        tq = tm // MAX_ROW_STRIDE
        for q0 in range(MAX_ROW_STRIDE):
            stage[q0 * tq:(q0 + 1) * tq, :] = load(pl.ds(q0, tq, stride=MAX_ROW_STRIDE))
        return jnp.concatenate(
            [stage[pl.ds(q0 * tq + q1, tl, stride=inner), :]
             for q1 in range(inner) for q0 in range(MAX_ROW_STRIDE)], axis=0)

    def inputs(i):
        rs = slice(i * tm, (i + 1) * tm)
        if dil == 1:
            return x_ref[0, rs, :], c_ref[0, rs, :], s_ref[0, rs, :]
        slabs, stage = rest[-2].at[i], rest[-1].at[i]
        for c in range(nslab):
            slabs[c] = x_ref[0, rs, c * LANES:(c + 1) * LANES]
        slabs[nslab] = c_ref[0, rs, :]
        slabs[nslab + 1] = s_ref[0, rs, :]
        p = [by_class(lambda idx, c=c: slabs[c, idx, :], stage.at[c]) for c in range(nslab + 2)]
        return jnp.concatenate(p[:nslab], axis=1), p[nslab], p[nslab + 1]

    lane = lax.broadcasted_iota(jnp.int32, (tm, LANES), 1)
    scale = A_HEAD_DIM ** -0.5 * LOG2E

    def project(i, x, cos, sin):
        h = _rms(x, g_ref[...]).astype(BF16)
        cos = jnp.where(lane < A_ROT_DIM, cos, 1.0)
        sin = jnp.where(lane < A_ROT_DIM, sin, 0.0)
        for s in range(3):
            acc = jnp.dot(h, w_ref[:, s * A_WIDTH:(s + 1) * A_WIDTH],
                          preferred_element_type=F32)
            for hd in range(A_HEADS):
                a = acc[:, hd * LANES:(hd + 1) * LANES]
                if s == 0:
                    a = _rot_half(a, 0, cos * scale, sin * scale)
                elif s == 1:
                    a = _rot_half(a, 0, cos, sin)
                a = a.astype(BF16)
                for r in range(dil):
                    qkv_ref[0, s * A_HEADS + hd, r, i * tl:(i + 1) * tl, :] = a[r * tl:(r + 1) * tl]
        if with_z:
            z_ref[0, i * tm:(i + 1) * tm, :] = jnp.dot(
                h, wz_ref[...], preferred_element_type=F32).astype(BF16)

    staged = [inputs(i) for i in range(parts)]
    for i in range(parts):
        project(i, *staged[i])


def _proj_a(x, cos, sin, gain, w, group, with_z, tm=512, parts=2):
    B, T, D = x.shape
    dil = A_DILATIONS[group]
    L = T // dil
    tm = tm * parts
    tl = tm // dil
    tok = lambda n: pl.BlockSpec((1, tm, n), lambda b, i: (b, i, 0))
    out_shape = [jax.ShapeDtypeStruct((B, 3 * A_HEADS, dil, L, LANES), BF16)]
    out_specs = [pl.BlockSpec((1, 3 * A_HEADS, dil, tl, LANES), lambda b, i: (b, 0, 0, i, 0))]
    if with_z:
        out_shape.append(jax.ShapeDtypeStruct((B, T, A_WIDTH), BF16))
        out_specs.append(tok(A_WIDTH))
    return pl.pallas_call(
        functools.partial(_proj_a_kernel, dil=dil, with_z=with_z, parts=parts),
        grid=(B, T // tm),
        in_specs=[tok(D), tok(LANES), tok(LANES),
                  pl.BlockSpec((1, D), lambda b, i: (0, 0)),
                  pl.BlockSpec((D, 3 * A_WIDTH), lambda b, i: (0, group))]
                 + ([pl.BlockSpec((D, A_WIDTH), lambda b, i: (0, 3 * len(A_DILATIONS)))]
                    if with_z else []),
        out_specs=out_specs,
        out_shape=out_shape,
        scratch_shapes=([pltpu.VMEM((parts, D // LANES + 2, tm // parts, LANES), F32)] * 2
                        if dil > 1 else []),
        compiler_params=_cparams(("parallel", "parallel")),
        name=f"proj_a_d{dil}",
    )(x, cos, sin, gain.reshape(1, D), *([w, w] if with_z else [w]))


def _attn_a_kernel(g0_ref, g1_ref, g2_ref, o_ref, onat, mnat, lnat, stage, *, chunk):
    for hd in range(o_ref.shape[1]):
        _attn_a_head(hd, g0_ref, g1_ref, g2_ref, o_ref, onat, mnat, lnat, stage, chunk)


def _attn_a_head(hd, g0_ref, g1_ref, g2_ref, o_ref, onat, mnat, lnat, stage, chunk):
    T = o_ref.shape[2]
    nt = (((1,), (1,)), ((), ()))
    qi = lax.broadcasted_iota(jnp.int32, (BAND, 2 * BAND), 0)
    ki = lax.broadcasted_iota(jnp.int32, (BAND, 2 * BAND), 1)
    band_mask = (ki >= qi) & (ki <= qi + BAND)
    first_mask = (lax.broadcasted_iota(jnp.int32, (BAND, BAND), 1)
                  <= lax.broadcasted_iota(jnp.int32, (BAND, BAND), 0))
    nblk = T // BAND
    refs = (g0_ref, g1_ref, g2_ref)

    def key_lo(g, j):
        nb = T // A_DILATIONS[g] // BAND
        return (j - 1) * BAND if j % nb else j * BAND

    def scores(g, c0):
        ref, nb = refs[g], T // A_DILATIONS[g] // BAND
        js = range(c0, c0 + chunk)
        s = [lax.dot_general(ref[0, 0, hd, j * BAND:(j + 1) * BAND, :],
                             ref[0, 1, hd, key_lo(g, j):(j + 1) * BAND, :], nt,
                             preferred_element_type=F32) for j in js]
        s = [jnp.where(band_mask if j % nb else first_mask, x, -jnp.inf) for j, x in zip(js, s)]
        m = [jnp.max(x, axis=-1, keepdims=True) for x in s]
        return s, m

    def values(g, c0, s, m):
        ref, dil = refs[g], A_DILATIONS[g]
        nb = T // dil // BAND
        js = range(c0, c0 + chunk)
        p = [jnp.exp2(x - y).astype(BF16) for x, y in zip(s, m)]
        o = [jnp.dot(x, jnp.concatenate([ref[0, 2, hd, key_lo(g, j):(j + 1) * BAND, :],
                                         jnp.ones(((j + 1) * BAND - key_lo(g, j), LANES), BF16)],
                                        axis=1),
                     preferred_element_type=F32) for x, j in zip(p, js)]
        for j, oj, mj in zip(js, o, m):
            r, n = divmod(j, nb)
            mj = jnp.broadcast_to(mj, (BAND, LANES))
            oj, lj = oj[:, :LANES], oj[:, LANES:]
            if dil == 1:
                sl = slice(j * BAND, (j + 1) * BAND)
                m1, m2 = mnat[0, sl, :], mnat[1, sl, :]
                mx = jnp.maximum(jnp.maximum(mj, m1), m2)
                e0, e1, e2 = jnp.exp2(mj - mx), jnp.exp2(m1 - mx), jnp.exp2(m2 - mx)
                num = e0 * oj + e1 * onat[0, sl, :] + e2 * onat[1, sl, :]
                den = e0 * lj + e1 * lnat[0, sl, :] + e2 * lnat[1, sl, :]
                o_ref[0, hd, sl, :] = (num / den).astype(BF16)
            elif dil > MAX_ROW_STRIDE:
                inner = dil // MAX_ROW_STRIDE
                r1, r0 = divmod(r, MAX_ROW_STRIDE)
                idx = pl.ds(r1 + n * BAND * inner, BAND, stride=inner)
                stage[0, r0, idx, :] = oj
                stage[1, r0, idx, :] = mj
                stage[2, r0, idx, :] = lj
            else:
                idx = pl.ds(r + n * BAND * dil, BAND, stride=dil)
                onat[g - 1, idx, :] = oj
                mnat[g - 1, idx, :] = mj
                lnat[g - 1, idx, :] = lj

    def second_pass(g):
        for a, nat in enumerate((onat, mnat, lnat)):
            for r0 in range(MAX_ROW_STRIDE):
                nat[g - 1, pl.ds(r0, T // MAX_ROW_STRIDE, stride=MAX_ROW_STRIDE), :] = stage[a, r0]

    order = sorted(range(len(refs)), key=lambda g: -A_DILATIONS[g])
    assert A_DILATIONS[order[-1]] == 1
    work = [(g, c0) for g in order for c0 in range(0, nblk, chunk)]
    nxt = scores(*work[0])
    for w, (g, c0) in enumerate(work):
        cur = nxt
        if w + 1 < len(work):
            nxt = scores(*work[w + 1])
        values(g, c0, *cur)
        last_of_group = w + 1 == len(work) or work[w + 1][0] != g
        if last_of_group and A_DILATIONS[g] > MAX_ROW_STRIDE:
            second_pass(g)


def _attn_a(qkv, chunk=8, heads=2):
    B, T = qkv[0].shape[0], qkv[0].shape[2] * qkv[0].shape[3]
    views = [a.reshape(B, 3, A_HEADS, T, LANES) for a in qkv]
    spec = pl.BlockSpec((1, 3, heads, T, LANES), lambda b, h: (b, 0, h, 0, 0))
    return pl.pallas_call(
        functools.partial(_attn_a_kernel, chunk=chunk),
        grid=(B, A_HEADS // heads),
        in_specs=[spec] * 3,
        out_specs=pl.BlockSpec((1, heads, T, LANES), lambda b, h: (b, h, 0, 0)),
        out_shape=jax.ShapeDtypeStruct((B, A_HEADS, T, LANES), BF16),
        scratch_shapes=[pltpu.VMEM((len(A_DILATIONS) - 1, T, LANES), F32)] * 3
                       + [pltpu.VMEM((3, MAX_ROW_STRIDE, T // MAX_ROW_STRIDE, LANES), F32)],
        compiler_params=_cparams(("parallel", "parallel")),
        name="attn_a",
    )(*views)


def _gated(o_ref, z_ref):
    o = jnp.concatenate([o_ref[0, h] for h in range(o_ref.shape[1])], axis=1)
    z = z_ref[0].astype(F32)
    return (o.astype(F32) * (z * jax.nn.sigmoid(z))).astype(BF16)


def _mid_kernel(o_ref, z_ref, x_ref, cb_ref, sb_ref,
                wout_ref, gpost_ref, gkv_ref, wd_ref, glat_ref, wup_ref,
                gpre_ref, wcq_ref, wz_ref, gq_ref, wqup_ref,
                h1_ref, k_ref, v_ref, q_ref, zb_ref, *, parts):
    tm = x_ref.shape[1]
    tp = tm // parts
    lane = lax.broadcasted_iota(jnp.int32, (tp, LANES), 1)
    one_col = jnp.where(lane == 0, 1.0, 0.0)
    scale = B_QK_DIM ** -0.5 * LOG2E
    st = [dict() for _ in range(parts)]

    def stage1(i):
        rs = slice(i * tp, (i + 1) * tp)
        o = jnp.concatenate([o_ref[0, h, rs, :] for h in range(o_ref.shape[1])], axis=1)
        z = z_ref[0, rs, :].astype(F32)
        g = (o.astype(F32) * (z * jax.nn.sigmoid(z))).astype(BF16)
        y = jnp.dot(g, wout_ref[...], preferred_element_type=F32)
        h1 = x_ref[0, rs, :] + _rms(y, gpost_ref[...])
        h1_ref[0, rs, :] = h1
        st[i]["h1"] = h1

    def stage2(i):
        h1 = st[i].pop("h1")
        hn = _rms(h1, gkv_ref[...]).astype(BF16)
        hb = _rms(h1, gpre_ref[...]).astype(BF16)
        st[i]["ckr"] = jnp.dot(hn, wd_ref[...], preferred_element_type=F32)
        st[i]["cq"] = jnp.dot(hb, wcq_ref[...], preferred_element_type=F32)
        st[i]["hb"] = hb

    def stage3(i):
        rs = slice(i * tp, (i + 1) * tp)
        ckr = st[i].pop("ckr")
        c_kv = _rms(ckr[:, :B_KV_LORA], glat_ref[...]).astype(BF16)
        c_q = _rms(st[i].pop("cq"), gq_ref[...]).astype(BF16)
        st[i]["kv"] = jnp.dot(c_kv, wup_ref[...], preferred_element_type=F32)
        st[i]["qq"] = jnp.dot(c_q, wqup_ref[...], preferred_element_type=F32)
        zb_ref[0, rs, :] = jnp.dot(st[i].pop("hb"), wz_ref[...],
                                   preferred_element_type=F32).astype(BF16)
        st[i]["kr"] = ckr[:, B_KV_LORA:]

    def stage4(i):
        rs = slice(i * tp, (i + 1) * tp)
        cos = jnp.where(lane >= B_NOPE, cb_ref[0, rs, :], 1.0)
        sin = jnp.where(lane >= B_NOPE, sb_ref[0, rs, :], 0.0)
        k_rope = _rot_half(st[i].pop("kr"), B_NOPE, cos, sin)
        kv, qq = st[i].pop("kv"), st[i].pop("qq")
        cq, sq = cos * scale, sin * scale
        for h in range(B_HEADS):
            blk = kv[:, h * LANES:(h + 1) * LANES]
            k_ref[0, h, rs, :] = jnp.where(lane < B_NOPE, blk, k_rope).astype(BF16)
            v_ref[0, h, rs, :] = jnp.where(lane >= B_NOPE, blk, one_col).astype(BF16)
            q_ref[0, h, rs, :] = _rot_half(qq[:, h * LANES:(h + 1) * LANES],
                                           B_NOPE, cq, sq).astype(BF16)

    for stage in (stage1, stage2, stage3, stage4):
        for i in range(parts):
            stage(i)


def _mid(o_a, z_a, x, cos, sin, weights, tm=512, parts=2):
    B, T, D = x.shape
    tok = lambda w: pl.BlockSpec((1, tm, w), lambda b, i: (b, i, 0))
    heads = lambda n: pl.BlockSpec((1, n, tm, LANES), lambda b, i: (b, 0, i, 0))
    const = lambda a: pl.BlockSpec(a.shape, lambda b, i: (0, 0), pipeline_mode=pl.Buffered(1))
    return pl.pallas_call(
        functools.partial(_mid_kernel, parts=parts),
        grid=(B, T // tm),
        in_specs=[heads(A_HEADS), tok(A_WIDTH), tok(D), tok(LANES), tok(LANES)]
                 + [const(w) for w in weights],
        out_specs=[tok(D), heads(B_HEADS), heads(B_HEADS), heads(B_HEADS), tok(B_WIDTH)],
        out_shape=[
            jax.ShapeDtypeStruct((B, T, D), F32),
            jax.ShapeDtypeStruct((B, B_HEADS, T, LANES), BF16),
            jax.ShapeDtypeStruct((B, B_HEADS, T, LANES), BF16),
            jax.ShapeDtypeStruct((B, B_HEADS, T, LANES), BF16),
            jax.ShapeDtypeStruct((B, T, B_WIDTH), BF16),
        ],
        compiler_params=_cparams(("parallel", "parallel")),
        name="mid",
    )(o_a, z_a, x, cos, sin, *weights)


def _attn_b_kernel(q_ref, k_ref, v_ref, o_ref, *, tq, ahead):
    T = k_ref.shape[2]
    qi = lax.broadcasted_iota(jnp.int32, (tq, tq), 0)
    ki = lax.broadcasted_iota(jnp.int32, (tq, tq), 1)
    causal = ki <= qi
    lane = lax.broadcasted_iota(jnp.int32, (tq, LANES), 1)
    nt = (((1,), (1,)), ((), ()))
    units = [(ii, hh) for ii in range(T // tq) for hh in range(2)]

    def scores(ii, hh):
        lo = ii * tq
        q = q_ref[0, hh, lo:lo + tq, :]
        s_d = lax.dot_general(q, k_ref[0, hh, lo:lo + tq, :], nt, preferred_element_type=F32)
        s_d = jnp.where(causal, s_d, -jnp.inf)
        m = jnp.max(s_d, axis=-1, keepdims=True)
        s_m = None
        if ii:
            s_m = lax.dot_general(q, k_ref[0, hh, 0:lo, :], nt, preferred_element_type=F32)
            m = jnp.maximum(m, jnp.max(s_m, axis=-1, keepdims=True))
        return s_d, s_m, m

    def values(ii, hh, s_d, s_m, m):
        lo = ii * tq
        o = jnp.dot(jnp.exp2(s_d - m).astype(BF16), v_ref[0, hh, lo:lo + tq, :],
                    preferred_element_type=F32)
        if ii:
            o = o + jnp.dot(jnp.exp2(s_m - m).astype(BF16), v_ref[0, hh, 0:lo, :],
                            preferred_element_type=F32)
        l = jnp.sum(jnp.where(lane == 0, o, 0.0), axis=-1, keepdims=True)
        return o / l

    pending = [scores(*units[u]) for u in range(ahead)]
    outs = []
    for u, (ii, hh) in enumerate(units):
        cur = pending.pop(0)
        if u + ahead < len(units):
            pending.append(scores(*units[u + ahead]))
        outs.append(values(ii, hh, *cur))
        if hh:
            lo = ii * tq
            o_ref[0, 0, lo:lo + tq, :] = jnp.where(
                lane < B_VDIM, pltpu.roll(outs[0], B_VDIM, 1), outs[1]).astype(BF16)
            outs = []


def _attn_b(q, k, v, tq=256, ahead=3):
    B, _, T, _ = q.shape
    heads = pl.BlockSpec((1, 2, T, LANES), lambda b, p: (b, p, 0, 0))
    return pl.pallas_call(
        functools.partial(_attn_b_kernel, tq=tq, ahead=ahead),
        grid=(B, B_PAIRS),
        in_specs=[heads, heads, heads],
        out_specs=pl.BlockSpec((1, 1, T, LANES), lambda b, p: (b, p, 0, 0)),
        out_shape=jax.ShapeDtypeStruct((B, B_PAIRS, T, LANES), BF16),
        compiler_params=_cparams(("parallel", "parallel")),
        name="attn_b",
    )(q, k, v)


def _final_kernel(o_ref, z_ref, h1_ref, w_ref, g_ref, out_ref):
    y = jnp.dot(_gated(o_ref, z_ref), w_ref[...], preferred_element_type=F32)
    out_ref[0] = h1_ref[0] + _rms(y, g_ref[...])


def _final(o_b, z_b, h1, w, gain, tm=1024):
    B, T, D = h1.shape
    tok = lambda w_: pl.BlockSpec((1, tm, w_), lambda b, i: (b, i, 0))
    return pl.pallas_call(
        _final_kernel,
        grid=(B, T // tm),
        in_specs=[pl.BlockSpec((1, B_PAIRS, tm, LANES), lambda b, i: (b, 0, i, 0)),
                  tok(B_WIDTH), tok(D),
                  pl.BlockSpec(w.shape, lambda b, i: (0, 0)),
                  pl.BlockSpec((1, D), lambda b, i: (0, 0))],
        out_specs=tok(D),
        out_shape=jax.ShapeDtypeStruct((B, T, D), F32),
        compiler_params=_cparams(("parallel", "parallel")),
        name="final",
    )(o_b, z_b, h1, w, gain.reshape(1, D))


def kernel(x, positions, a_pre_norm, a_w_in, a_w_out, a_post_norm, kv_norm, kv_w_down, kv_latent_norm, kv_w_up, b_pre_norm, b_w_in, b_q_norm, b_w_q_up, b_w_out, b_post_norm):
    D = x.shape[-1]
    cos, sin = _rope_tables(positions)

    w_in = a_w_in[0].astype(BF16)
    qkv = []
    z_a = None
    for g, dil in enumerate(A_DILATIONS):
        if dil == 1:
            qkv_g, z_a = _proj_a(x, cos, sin, a_pre_norm[0], w_in, g, True)
        else:
            (qkv_g,) = _proj_a(x, cos, sin, a_pre_norm[0], w_in, g, False)
        qkv.append(qkv_g)
    o_a = _attn_a(qkv)

    row = lambda g: g.reshape(1, -1)
    wd = jnp.zeros((D, B_KV_LORA + LANES), F32)
    wd = wd.at[:, :B_KV_LORA].set(kv_w_down[:, :B_KV_LORA])
    wd = wd.at[:, B_KV_LORA + B_NOPE:B_KV_LORA + B_QK_DIM].set(kv_w_down[:, B_KV_LORA:])
    wqup = jnp.pad(b_w_q_up[0].reshape(B_Q_LORA, B_HEADS, B_QK_DIM),
                   ((0, 0), (0, 0), (0, LANES - B_QK_DIM))).reshape(B_Q_LORA, B_HEADS * LANES)
    weights = [
        a_w_out[0].astype(BF16), row(a_post_norm[0]), row(kv_norm), wd.astype(BF16),
        row(kv_latent_norm), kv_w_up.astype(BF16),
        row(b_pre_norm[0]), b_w_in[0][:, :B_Q_LORA].astype(BF16),
        b_w_in[0][:, B_Q_LORA:].astype(BF16), row(b_q_norm[0]), wqup.astype(BF16),
    ]
    h1, k_b, v_b, q_b, z_b = _mid(o_a, z_a, x, cos, sin, weights)

    o_b = _attn_b(q_b, k_b, v_b)
    return _final(o_b, z_b, h1, b_w_out[0].astype(BF16), b_post_norm[0])
```

```python
import functools

import jax
import jax.numpy as jnp
from jax import lax
from jax.experimental import pallas as pl
from jax.experimental.pallas import tpu as pltpu

F32 = jnp.float32
BF16 = jnp.bfloat16

NORM_EPS = 1e-6
LOG2E = 1.4426950408889634
LANES = 128

A_WINDOWS = (128, 512, 2048)
A_DILATIONS = (1, 4, 16)
A_HEADS = 8
A_HEAD_DIM = 128
A_WIDTH = A_HEADS * A_HEAD_DIM
A_ROT_DIM = A_HEAD_DIM // 4
A_ROPE_THETA = 500000.0
BAND = 128
MAX_ROW_STRIDE = 4

B_HEADS = 16
B_NOPE = 64
B_ROPE = 32
B_QK_DIM = B_NOPE + B_ROPE
B_VDIM = 64
B_WIDTH = B_HEADS * B_VDIM
B_Q_LORA = 384
B_KV_LORA = 256
B_ROPE_THETA = 10000.0
B_PAIRS = B_HEADS // 2

VMEM_LIMIT = 56 * 1024 * 1024


def _cparams(sem):
    return pltpu.CompilerParams(dimension_semantics=sem, vmem_limit_bytes=VMEM_LIMIT)


def _rms(x, g):
    ms = jnp.mean(x * x, axis=-1, keepdims=True)
    return x * lax.rsqrt(ms + NORM_EPS) * g


def _rot_half(a, lo, cos, sin):
    lane = lax.broadcasted_iota(jnp.int32, a.shape, 1)
    partner = jnp.where(lane < lo + 16, pltpu.roll(a, LANES - 16, 1), pltpu.roll(a, 16, 1))
    return a * cos + partner * sin


def _rope_table_kernel(pe_ref, po_ref, freq_ref, sign_ref, cos_ref, sin_ref):
    half_t = pe_ref.shape[1]
    lane = lax.broadcasted_iota(jnp.int32, (half_t, LANES), 1)
    pos = jnp.where(lane < LANES // 2, pe_ref[0].astype(F32), po_ref[0].astype(F32))
    ang = pos * freq_ref[...]
    for ref, packed, ident in ((cos_ref, jnp.cos(ang), 1.0),
                               (sin_ref, jnp.sin(ang) * sign_ref[...], 0.0)):
        def unpack(lo0, lo1):
            a = packed if lo0 == 0 else pltpu.roll(packed, LANES - lo0, 1)
            b_ = pltpu.roll(packed, (B_NOPE - lo1) % LANES, 1)
            return jnp.where(lane < A_ROT_DIM, a,
                             jnp.where((lane >= B_NOPE) & (lane < B_QK_DIM), b_, ident))
        ref[0, pl.ds(0, half_t, stride=2), :] = unpack(0, A_ROT_DIM)
        ref[0, pl.ds(1, half_t, stride=2), :] = unpack(LANES // 2, LANES // 2 + A_ROT_DIM)


def _rope_lane_rows():
    half = A_ROT_DIM // 2
    inv_a = 1.0 / (A_ROPE_THETA ** (jnp.arange(half, dtype=F32) * (2.0 / A_ROT_DIM)))
    inv_b = 1.0 / (B_ROPE_THETA ** (jnp.arange(B_ROPE // 2, dtype=F32) * (2.0 / B_ROPE)))
    ones = jnp.ones((half,), F32)
    zeros = lambda n: jnp.zeros((n,), F32)
    row = lambda v: v.reshape(1, LANES)
    packed = (row(jnp.tile(jnp.concatenate([inv_a, inv_a, inv_b, inv_b]), 2)),
              row(jnp.tile(jnp.concatenate([-ones, ones, -ones, ones]), 2)))
    table = (row(jnp.concatenate([inv_a, inv_a, zeros(B_NOPE - A_ROT_DIM), inv_b, inv_b,
                                  zeros(LANES - B_QK_DIM)])),
             row(jnp.concatenate([-ones, ones, zeros(B_NOPE - A_ROT_DIM), -ones, ones,
                                  zeros(LANES - B_QK_DIM)])))
    return packed, table


def _rope_tables_direct(positions):
    B, T = positions.shape
    (freq, sign), _ = _rope_lane_rows()
    tab = jax.ShapeDtypeStruct((B, T, LANES), F32)
    spec = pl.BlockSpec((1, T, LANES), lambda b: (b, 0, 0))
    const = pl.BlockSpec((1, LANES), lambda b: (0, 0))
    pos = pl.BlockSpec((1, T // 2, 1), lambda b: (b, 0, 0))
    return pl.pallas_call(
        _rope_table_kernel,
        grid=(B,),
        in_specs=[pos, pos, const, const],
        out_specs=[spec] * 2,
        out_shape=[tab] * 2,
        compiler_params=_cparams(("parallel",)),
        name="rope_tables",
    )(positions[:, 0::2].reshape(B, T // 2, 1), positions[:, 1::2].reshape(B, T // 2, 1),
      freq, sign)


def _rope_shift_kernel(off_ref, freq_ref, sign_ref, ct_ref, st_ref, cos_ref, sin_ref):
    ang = off_ref[0].astype(F32) * freq_ref[...]
    co, so = jnp.cos(ang), jnp.sin(ang) * sign_ref[...]
    ct, st = ct_ref[0], st_ref[0]
    cos_ref[0] = co * ct - so * st
    sin_ref[0] = so * ct + co * st


def _rope_tables(positions):
    B, T = positions.shape
    steps = jnp.arange(T, dtype=positions.dtype)[None, :]

    def shifted(pos):
        ct, st = _rope_tables_direct(steps)
        _, (freq, sign) = _rope_lane_rows()
        tab = jax.ShapeDtypeStruct((B, T, LANES), F32)
        spec = pl.BlockSpec((1, T, LANES), lambda b: (b, 0, 0))
        base = pl.BlockSpec((1, T, LANES), lambda b: (0, 0, 0))
        const = pl.BlockSpec((1, LANES), lambda b: (0, 0))
        return tuple(pl.pallas_call(
            _rope_shift_kernel,
            grid=(B,),
            in_specs=[pl.BlockSpec((1, 1, 1), lambda b: (b, 0, 0)), const, const, base, base],
            out_specs=[spec] * 2,
            out_shape=[tab] * 2,
            compiler_params=_cparams(("parallel",)),
            name="rope_shift",
        )(pos[:, :1].reshape(B, 1, 1), freq, sign, ct, st))

    contiguous = jnp.all(positions == positions[:, :1] + steps)
    return lax.cond(contiguous, shifted, lambda pos: tuple(_rope_tables_direct(pos)), positions)


def _proj_a_kernel(x_ref, c_ref, s_ref, g_ref, w_ref, *rest, dil, with_z, parts):
    if with_z:
        wz_ref, qkv_ref, z_ref = rest[:3]
    else:
        qkv_ref = rest[0]
    tm_all, D = x_ref.shape[1], x_ref.shape[2]
    tm = tm_all // parts
    tl = tm // dil
    nslab = D // LANES

    def by_class(load, stage):
        if dil <= MAX_ROW_STRIDE:
            return jnp.concatenate([load(pl.ds(r, tl, stride=dil)) for r in range(dil)], axis=0)
        inner = dil // MAX_ROW_STRIDE
        tq = tm // MAX_ROW_STRIDE
        for q0 in range(MAX_ROW_STRIDE):
            stage[q0 * tq:(q0 + 1) * tq, :] = load(pl.ds(q0, tq, stride=MAX_ROW_STRIDE))
        return jnp.concatenate(
            [stage[pl.ds(q0 * tq + q1, tl, stride=inner), :]
             for q1 in range(inner) for q0 in range(MAX_ROW_STRIDE)], axis=0)

    def inputs(i):
        rs = slice(i * tm, (i + 1) * tm)
        if dil == 1:
            return x_ref[0, rs, :], c_ref[0, rs, :], s_ref[0, rs, :]
        slabs, stage = rest[-2].at[i], rest[-1].at[i]
        for c in range(nslab):
            slabs[c] = x_ref[0, rs, c * LANES:(c + 1) * LANES]
        slabs[nslab] = c_ref[0, rs, :]
        slabs[nslab + 1] = s_ref[0, rs, :]
        p = [by_class(lambda idx, c=c: slabs[c, idx, :], stage.at[c]) for c in range(nslab + 2)]
        return jnp.concatenate(p[:nslab], axis=1), p[nslab], p[nslab + 1]

    lane = lax.broadcasted_iota(jnp.int32, (tm, LANES), 1)
    scale = A_HEAD_DIM ** -0.5 * LOG2E

    def project(i, x, cos, sin):
        h = _rms(x, g_ref[...]).astype(BF16)
        cos = jnp.where(lane < A_ROT_DIM, cos, 1.0)
        sin = jnp.where(lane < A_ROT_DIM, sin, 0.0)
        for s in range(3):
            acc = jnp.dot(h, w_ref[:, s * A_WIDTH:(s + 1) * A_WIDTH],
                          preferred_element_type=F32)
            for hd in range(A_HEADS):
                a = acc[:, hd * LANES:(hd + 1) * LANES]
                if s == 0:
                    a = _rot_half(a, 0, cos * scale, sin * scale)
                elif s == 1:
                    a = _rot_half(a, 0, cos, sin)
                a = a.astype(BF16)
                for r in range(dil):
                    qkv_ref[0, s * A_HEADS + hd, r, i * tl:(i + 1) * tl, :] = a[r * tl:(r + 1) * tl]
        if with_z:
            z_ref[0, i * tm:(i + 1) * tm, :] = jnp.dot(
                h, wz_ref[...], preferred_element_type=F32).astype(BF16)

    staged = [inputs(i) for i in range(parts)]
    for i in range(parts):
        project(i, *staged[i])


def _proj_a(x, cos, sin, gain, w, group, with_z, tm=512, parts=2):
    B, T, D = x.shape
    dil = A_DILATIONS[group]
    L = T // dil
    tm = tm * parts
    tl = tm // dil
    tok = lambda n: pl.BlockSpec((1, tm, n), lambda b, i: (b, i, 0))
    out_shape = [jax.ShapeDtypeStruct((B, 3 * A_HEADS, dil, L, LANES), BF16)]
    out_specs = [pl.BlockSpec((1, 3 * A_HEADS, dil, tl, LANES), lambda b, i: (b, 0, 0, i, 0))]
    if with_z:
        out_shape.append(jax.ShapeDtypeStruct((B, T, A_WIDTH), BF16))
        out_specs.append(tok(A_WIDTH))
    return pl.pallas_call(
        functools.partial(_proj_a_kernel, dil=dil, with_z=with_z, parts=parts),
        grid=(B, T // tm),
        in_specs=[tok(D), tok(LANES), tok(LANES),
                  pl.BlockSpec((1, D), lambda b, i: (0, 0)),
                  pl.BlockSpec((D, 3 * A_WIDTH), lambda b, i: (0, group))]
                 + ([pl.BlockSpec((D, A_WIDTH), lambda b, i: (0, 3 * len(A_DILATIONS)))]
                    if with_z else []),
        out_specs=out_specs,
        out_shape=out_shape,
        scratch_shapes=([pltpu.VMEM((parts, D // LANES + 2, tm // parts, LANES), F32)] * 2
                        if dil > 1 else []),
        compiler_params=_cparams(("parallel", "parallel")),
        name=f"proj_a_d{dil}",
    )(x, cos, sin, gain.reshape(1, D), *([w, w] if with_z else [w]))


def _attn_a_kernel(g0_ref, g1_ref, g2_ref, o_ref, onat, mnat, lnat, stage, *, chunk):
    for hd in range(o_ref.shape[1]):
        _attn_a_head(hd, g0_ref, g1_ref, g2_ref, o_ref, onat, mnat, lnat, stage, chunk)


def _attn_a_head(hd, g0_ref, g1_ref, g2_ref, o_ref, onat, mnat, lnat, stage, chunk):
    T = o_ref.shape[2]
    nt = (((1,), (1,)), ((), ()))
    qi = lax.broadcasted_iota(jnp.int32, (BAND, 2 * BAND), 0)
    ki = lax.broadcasted_iota(jnp.int32, (BAND, 2 * BAND), 1)
    band_mask = (ki >= qi) & (ki <= qi + BAND)
    first_mask = (lax.broadcasted_iota(jnp.int32, (BAND, BAND), 1)
                  <= lax.broadcasted_iota(jnp.int32, (BAND, BAND), 0))
    nblk = T // BAND
    refs = (g0_ref, g1_ref, g2_ref)

    def key_lo(g, j):
        nb = T // A_DILATIONS[g] // BAND
        return (j - 1) * BAND if j % nb else j * BAND

    def scores(g, c0):
        ref, nb = refs[g], T // A_DILATIONS[g] // BAND
        js = range(c0, c0 + chunk)
        s = [lax.dot_general(ref[0, 0, hd, j * BAND:(j + 1) * BAND, :],
                             ref[0, 1, hd, key_lo(g, j):(j + 1) * BAND, :], nt,
                             preferred_element_type=F32) for j in js]
        s = [jnp.where(band_mask if j % nb else first_mask, x, -jnp.inf) for j, x in zip(js, s)]
        m = [jnp.max(x, axis=-1, keepdims=True) for x in s]
        return s, m

    def values(g, c0, s, m):
        ref, dil = refs[g], A_DILATIONS[g]
        nb = T // dil // BAND
        js = range(c0, c0 + chunk)
        p = [jnp.exp2(x - y).astype(BF16) for x, y in zip(s, m)]
        o = [jnp.dot(x, jnp.concatenate([ref[0, 2, hd, key_lo(g, j):(j + 1) * BAND, :],
                                         jnp.ones(((j + 1) * BAND - key_lo(g, j), LANES), BF16)],
                                        axis=1),
                     preferred_element_type=F32) for x, j in zip(p, js)]
        for j, oj, mj in zip(js, o, m):
            r, n = divmod(j, nb)
            mj = jnp.broadcast_to(mj, (BAND, LANES))
            oj, lj = oj[:, :LANES], oj[:, LANES:]
            if dil == 1:
                sl = slice(j * BAND, (j + 1) * BAND)
                m1, m2 = mnat[0, sl, :], mnat[1, sl, :]
                mx = jnp.maximum(jnp.maximum(mj, m1), m2)
                e0, e1, e2 = jnp.exp2(mj - mx), jnp.exp2(m1 - mx), jnp.exp2(m2 - mx)
                num = e0 * oj + e1 * onat[0, sl, :] + e2 * onat[1, sl, :]
                den = e0 * lj + e1 * lnat[0, sl, :] + e2 * lnat[1, sl, :]
                o_ref[0, hd, sl, :] = (num / den).astype(BF16)
            elif dil > MAX_ROW_STRIDE:
                inner = dil // MAX_ROW_STRIDE
                r1, r0 = divmod(r, MAX_ROW_STRIDE)
                idx = pl.ds(r1 + n * BAND * inner, BAND, stride=inner)
                stage[0, r0, idx, :] = oj
                stage[1, r0, idx, :] = mj
                stage[2, r0, idx, :] = lj
            else:
                idx = pl.ds(r + n * BAND * dil, BAND, stride=dil)
                onat[g - 1, idx, :] = oj
                mnat[g - 1, idx, :] = mj
                lnat[g - 1, idx, :] = lj

    def second_pass(g):
        for a, nat in enumerate((onat, mnat, lnat)):
            for r0 in range(MAX_ROW_STRIDE):
                nat[g - 1, pl.ds(r0, T // MAX_ROW_STRIDE, stride=MAX_ROW_STRIDE), :] = stage[a, r0]

    order = sorted(range(len(refs)), key=lambda g: -A_DILATIONS[g])
    assert A_DILATIONS[order[-1]] == 1
    work = [(g, c0) for g in order for c0 in range(0, nblk, chunk)]
    nxt = scores(*work[0])
    for w, (g, c0) in enumerate(work):
        cur = nxt
        if w + 1 < len(work):
            nxt = scores(*work[w + 1])
        values(g, c0, *cur)
        last_of_group = w + 1 == len(work) or work[w + 1][0] != g
        if last_of_group and A_DILATIONS[g] > MAX_ROW_STRIDE:
            second_pass(g)


def _attn_a(qkv, chunk=8, heads=2):
    B, T = qkv[0].shape[0], qkv[0].shape[2] * qkv[0].shape[3]
    views = [a.reshape(B, 3, A_HEADS, T, LANES) for a in qkv]
    spec = pl.BlockSpec((1, 3, heads, T, LANES), lambda b, h: (b, 0, h, 0, 0))
    return pl.pallas_call(
        functools.partial(_attn_a_kernel, chunk=chunk),
        grid=(B, A_HEADS // heads),
        in_specs=[spec] * 3,
        out_specs=pl.BlockSpec((1, heads, T, LANES), lambda b, h: (b, h, 0, 0)),
        out_shape=jax.ShapeDtypeStruct((B, A_HEADS, T, LANES), BF16),
        scratch_shapes=[pltpu.VMEM((len(A_DILATIONS) - 1, T, LANES), F32)] * 3
                       + [pltpu.VMEM((3, MAX_ROW_STRIDE, T // MAX_ROW_STRIDE, LANES), F32)],
        compiler_params=_cparams(("parallel", "parallel")),
        name="attn_a",
    )(*views)


def _gated(o_ref, z_ref):
    o = jnp.concatenate([o_ref[0, h] for h in range(o_ref.shape[1])], axis=1)
    z = z_ref[0].astype(F32)
    return (o.astype(F32) * (z * jax.nn.sigmoid(z))).astype(BF16)


def _mid_kernel(o_ref, z_ref, x_ref, cb_ref, sb_ref,
                wout_ref, gpost_ref, gkv_ref, wd_ref, glat_ref, wup_ref,
                gpre_ref, wcq_ref, wz_ref, gq_ref, wqup_ref,
                h1_ref, k_ref, v_ref, q_ref, zb_ref, *, parts):
    tm = x_ref.shape[1]
    tp = tm // parts
    lane = lax.broadcasted_iota(jnp.int32, (tp, LANES), 1)
    one_col = jnp.where(lane == 0, 1.0, 0.0)
    scale = B_QK_DIM ** -0.5 * LOG2E
    st = [dict() for _ in range(parts)]

    def stage1(i):
        rs = slice(i * tp, (i + 1) * tp)
        o = jnp.concatenate([o_ref[0, h, rs, :] for h in range(o_ref.shape[1])], axis=1)
        z = z_ref[0, rs, :].astype(F32)
        g = (o.astype(F32) * (z * jax.nn.sigmoid(z))).astype(BF16)
        y = jnp.dot(g, wout_ref[...], preferred_element_type=F32)
        h1 = x_ref[0, rs, :] + _rms(y, gpost_ref[...])
        h1_ref[0, rs, :] = h1
        st[i]["h1"] = h1

    def stage2(i):
        h1 = st[i].pop("h1")
        hn = _rms(h1, gkv_ref[...]).astype(BF16)
        hb = _rms(h1, gpre_ref[...]).astype(BF16)
        st[i]["ckr"] = jnp.dot(hn, wd_ref[...], preferred_element_type=F32)
        st[i]["cq"] = jnp.dot(hb, wcq_ref[...], preferred_element_type=F32)
        st[i]["hb"] = hb

    def stage3(i):
        rs = slice(i * tp, (i + 1) * tp)
        ckr = st[i].pop("ckr")
        c_kv = _rms(ckr[:, :B_KV_LORA], glat_ref[...]).astype(BF16)
        c_q = _rms(st[i].pop("cq"), gq_ref[...]).astype(BF16)
        st[i]["kv"] = jnp.dot(c_kv, wup_ref[...], preferred_element_type=F32)
        st[i]["qq"] = jnp.dot(c_q, wqup_ref[...], preferred_element_type=F32)
        zb_ref[0, rs, :] = jnp.dot(st[i].pop("hb"), wz_ref[...],
                                   preferred_element_type=F32).astype(BF16)
        st[i]["kr"] = ckr[:, B_KV_LORA:]

    def stage4(i):
        rs = slice(i * tp, (i + 1) * tp)
        cos = jnp.where(lane >= B_NOPE, cb_ref[0, rs, :], 1.0)
        sin = jnp.where(lane >= B_NOPE, sb_ref[0, rs, :], 0.0)
        k_rope = _rot_half(st[i].pop("kr"), B_NOPE, cos, sin)
        kv, qq = st[i].pop("kv"), st[i].pop("qq")
        cq, sq = cos * scale, sin * scale
        for h in range(B_HEADS):
            blk = kv[:, h * LANES:(h + 1) * LANES]
            k_ref[0, h, rs, :] = jnp.where(lane < B_NOPE, blk, k_rope).astype(BF16)
            v_ref[0, h, rs, :] = jnp.where(lane >= B_NOPE, blk, one_col).astype(BF16)
            q_ref[0, h, rs, :] = _rot_half(qq[:, h * LANES:(h + 1) * LANES],
                                           B_NOPE, cq, sq).astype(BF16)

    for stage in (stage1, stage2, stage3, stage4):
        for i in range(parts):
            stage(i)


def _mid(o_a, z_a, x, cos, sin, weights, tm=512, parts=2):
    B, T, D = x.shape
    tok = lambda w: pl.BlockSpec((1, tm, w), lambda b, i: (b, i, 0))
    heads = lambda n: pl.BlockSpec((1, n, tm, LANES), lambda b, i: (b, 0, i, 0))
    const = lambda a: pl.BlockSpec(a.shape, lambda b, i: (0, 0), pipeline_mode=pl.Buffered(1))
    return pl.pallas_call(
        functools.partial(_mid_kernel, parts=parts),
        grid=(B, T // tm),
        in_specs=[heads(A_HEADS), tok(A_WIDTH), tok(D), tok(LANES), tok(LANES)]
                 + [const(w) for w in weights],
        out_specs=[tok(D), heads(B_HEADS), heads(B_HEADS), heads(B_HEADS), tok(B_WIDTH)],
        out_shape=[
            jax.ShapeDtypeStruct((B, T, D), F32),
            jax.ShapeDtypeStruct((B, B_HEADS, T, LANES), BF16),
            jax.ShapeDtypeStruct((B, B_HEADS, T, LANES), BF16),
            jax.ShapeDtypeStruct((B, B_HEADS, T, LANES), BF16),
            jax.ShapeDtypeStruct((B, T, B_WIDTH), BF16),
        ],
        compiler_params=_cparams(("parallel", "parallel")),
        name="mid",
    )(o_a, z_a, x, cos, sin, *weights)


def _attn_b_kernel(q_ref, k_ref, v_ref, o_ref, *, tq, ahead):
    T = k_ref.shape[2]
    qi = lax.broadcasted_iota(jnp.int32, (tq, tq), 0)
    ki = lax.broadcasted_iota(jnp.int32, (tq, tq), 1)
    causal = ki <= qi
    lane = lax.broadcasted_iota(jnp.int32, (tq, LANES), 1)
    nt = (((1,), (1,)), ((), ()))
    units = [(ii, hh) for ii in range(T // tq) for hh in range(2)]

    def scores(ii, hh):
        lo = ii * tq
        q = q_ref[0, hh, lo:lo + tq, :]
        s_d = lax.dot_general(q, k_ref[0, hh, lo:lo + tq, :], nt, preferred_element_type=F32)
        s_d = jnp.where(causal, s_d, -jnp.inf)
        m = jnp.max(s_d, axis=-1, keepdims=True)
        s_m = None
        if ii:
            s_m = lax.dot_general(q, k_ref[0, hh, 0:lo, :], nt, preferred_element_type=F32)
            m = jnp.maximum(m, jnp.max(s_m, axis=-1, keepdims=True))
        return s_d, s_m, m

    def values(ii, hh, s_d, s_m, m):
        lo = ii * tq
        o = jnp.dot(jnp.exp2(s_d - m).astype(BF16), v_ref[0, hh, lo:lo + tq, :],
                    preferred_element_type=F32)
        if ii:
            o = o + jnp.dot(jnp.exp2(s_m - m).astype(BF16), v_ref[0, hh, 0:lo, :],
                            preferred_element_type=F32)
        l = jnp.sum(jnp.where(lane == 0, o, 0.0), axis=-1, keepdims=True)
        return o / l

    pending = [scores(*units[u]) for u in range(ahead)]
    outs = []
    for u, (ii, hh) in enumerate(units):
        cur = pending.pop(0)
        if u + ahead < len(units):
            pending.append(scores(*units[u + ahead]))
        outs.append(values(ii, hh, *cur))
        if hh:
            lo = ii * tq
            o_ref[0, 0, lo:lo + tq, :] = jnp.where(
                lane < B_VDIM, pltpu.roll(outs[0], B_VDIM, 1), outs[1]).astype(BF16)
            outs = []


def _attn_b(q, k, v, tq=256, ahead=3):
    B, _, T, _ = q.shape
    heads = pl.BlockSpec((1, 2, T, LANES), lambda b, p: (b, p, 0, 0))
    return pl.pallas_call(
        functools.partial(_attn_b_kernel, tq=tq, ahead=ahead),
        grid=(B, B_PAIRS),
        in_specs=[heads, heads, heads],
        out_specs=pl.BlockSpec((1, 1, T, LANES), lambda b, p: (b, p, 0, 0)),
        out_shape=jax.ShapeDtypeStruct((B, B_PAIRS, T, LANES), BF16),
        compiler_params=_cparams(("parallel", "parallel")),
        name="attn_b",
    )(q, k, v)


def _final_kernel(o_ref, z_ref, h1_ref, w_ref, g_ref, out_ref):
    y = jnp.dot(_gated(o_ref, z_ref), w_ref[...], preferred_element_type=F32)
    out_ref[0] = h1_ref[0] + _rms(y, g_ref[...])


def _final(o_b, z_b, h1, w, gain, tm=1024):
    B, T, D = h1.shape
    tok = lambda w_: pl.BlockSpec((1, tm, w_), lambda b, i: (b, i, 0))
    return pl.pallas_call(
        _final_kernel,
        grid=(B, T // tm),
        in_specs=[pl.BlockSpec((1, B_PAIRS, tm, LANES), lambda b, i: (b, 0, i, 0)),
                  tok(B_WIDTH), tok(D),
                  pl.BlockSpec(w.shape, lambda b, i: (0, 0)),
                  pl.BlockSpec((1, D), lambda b, i: (0, 0))],
        out_specs=tok(D),
        out_shape=jax.ShapeDtypeStruct((B, T, D), F32),
        compiler_params=_cparams(("parallel", "parallel")),
        name="final",
    )(o_b, z_b, h1, w, gain.reshape(1, D))


def kernel(x, positions, a_pre_norm, a_w_in, a_w_out, a_post_norm, kv_norm, kv_w_down, kv_latent_norm, kv_w_up, b_pre_norm, b_w_in, b_q_norm, b_w_q_up, b_w_out, b_post_norm):
    B, T, D = x.shape
    assert all(w // d == BAND for w, d in zip(A_WINDOWS, A_DILATIONS))
    assert D == A_WIDTH and T % (BAND * max(A_DILATIONS)) == 0 and positions.shape == (B, T)
    cos, sin = _rope_tables(positions)

    w_in = a_w_in[0].astype(BF16)
    qkv = []
    z_a = None
    for g, dil in enumerate(A_DILATIONS):
        if dil == 1:
            qkv_g, z_a = _proj_a(x, cos, sin, a_pre_norm[0], w_in, g, True)
        else:
            (qkv_g,) = _proj_a(x, cos, sin, a_pre_norm[0], w_in, g, False)
        qkv.append(qkv_g)
    o_a = _attn_a(qkv)

    row = lambda g: g.reshape(1, -1)
    wd = jnp.zeros((D, B_KV_LORA + LANES), F32)
    wd = wd.at[:, :B_KV_LORA].set(kv_w_down[:, :B_KV_LORA])
    wd = wd.at[:, B_KV_LORA + B_NOPE:B_KV_LORA + B_QK_DIM].set(kv_w_down[:, B_KV_LORA:])
    wqup = jnp.pad(b_w_q_up[0].reshape(B_Q_LORA, B_HEADS, B_QK_DIM),
                   ((0, 0), (0, 0), (0, LANES - B_QK_DIM))).reshape(B_Q_LORA, B_HEADS * LANES)
    weights = [
        a_w_out[0].astype(BF16), row(a_post_norm[0]), row(kv_norm), wd.astype(BF16),
        row(kv_latent_norm), kv_w_up.astype(BF16),
        row(b_pre_norm[0]), b_w_in[0][:, :B_Q_LORA].astype(BF16),
        b_w_in[0][:, B_Q_LORA:].astype(BF16), row(b_q_norm[0]), wqup.astype(BF16),
    ]
    h1, k_b, v_b, q_b, z_b = _mid(o_a, z_a, x, cos, sin, weights)

    o_b = _attn_b(q_b, k_b, v_b)
    return _final(o_b, z_b, h1, b_w_out[0].astype(BF16), b_post_norm[0])
```

```python
import functools

import jax
import jax.numpy as jnp
from jax import lax
from jax.experimental import pallas as pl
from jax.experimental.pallas import tpu as pltpu

F32 = jnp.float32
BF16 = jnp.bfloat16

NORM_EPS = 1e-6
LOG2E = 1.4426950408889634
LANES = 128

A_WINDOWS = (128, 512, 2048)
A_DILATIONS = (1, 4, 16)
A_HEADS = 8
A_HEAD_DIM = 128
A_WIDTH = A_HEADS * A_HEAD_DIM
A_ROT_DIM = A_HEAD_DIM // 4
A_ROPE_THETA = 500000.0
BAND = 128
MAX_ROW_STRIDE = 4

B_HEADS = 16
B_NOPE = 64
B_ROPE = 32
B_QK_DIM = B_NOPE + B_ROPE
B_VDIM = 64
B_WIDTH = B_HEADS * B_VDIM
B_Q_LORA = 384
B_KV_LORA = 256
B_ROPE_THETA = 10000.0
B_PAIRS = B_HEADS // 2

VMEM_LIMIT = 56 * 1024 * 1024


def _cparams(sem):
    return pltpu.CompilerParams(dimension_semantics=sem, vmem_limit_bytes=VMEM_LIMIT)


def _rms(x, g):
    ms = jnp.mean(x * x, axis=-1, keepdims=True)
    return x * lax.rsqrt(ms + NORM_EPS) * g


def _rot_half(a, lo, cos, sin):
    lane = lax.broadcasted_iota(jnp.int32, a.shape, 1)
    partner = jnp.where(lane < lo + 16, pltpu.roll(a, LANES - 16, 1), pltpu.roll(a, 16, 1))
    return a * cos + partner * sin


def _rope_table_kernel(pe_ref, po_ref, freq_ref, sign_ref, cos_ref, sin_ref):
    half_t = pe_ref.shape[1]
    lane = lax.broadcasted_iota(jnp.int32, (half_t, LANES), 1)
    pos = jnp.where(lane < LANES // 2, pe_ref[0].astype(F32), po_ref[0].astype(F32))
    ang = pos * freq_ref[...]
    for ref, packed, ident in ((cos_ref, jnp.cos(ang), 1.0),
                               (sin_ref, jnp.sin(ang) * sign_ref[...], 0.0)):
        def unpack(lo0, lo1):
            a = packed if lo0 == 0 else pltpu.roll(packed, LANES - lo0, 1)
            b_ = pltpu.roll(packed, (B_NOPE - lo1) % LANES, 1)
            return jnp.where(lane < A_ROT_DIM, a,
                             jnp.where((lane >= B_NOPE) & (lane < B_QK_DIM), b_, ident))
        ref[0, pl.ds(0, half_t, stride=2), :] = unpack(0, A_ROT_DIM)
        ref[0, pl.ds(1, half_t, stride=2), :] = unpack(LANES // 2, LANES // 2 + A_ROT_DIM)


def _rope_lane_rows():
    half = A_ROT_DIM // 2
    inv_a = 1.0 / (A_ROPE_THETA ** (jnp.arange(half, dtype=F32) * (2.0 / A_ROT_DIM)))
    inv_b = 1.0 / (B_ROPE_THETA ** (jnp.arange(B_ROPE // 2, dtype=F32) * (2.0 / B_ROPE)))
    ones = jnp.ones((half,), F32)
    zeros = lambda n: jnp.zeros((n,), F32)
    row = lambda v: v.reshape(1, LANES)
    packed = (row(jnp.tile(jnp.concatenate([inv_a, inv_a, inv_b, inv_b]), 2)),
              row(jnp.tile(jnp.concatenate([-ones, ones, -ones, ones]), 2)))
    table = (row(jnp.concatenate([inv_a, inv_a, zeros(B_NOPE - A_ROT_DIM), inv_b, inv_b,
                                  zeros(LANES - B_QK_DIM)])),
             row(jnp.concatenate([-ones, ones, zeros(B_NOPE - A_ROT_DIM), -ones, ones,
                                  zeros(LANES - B_QK_DIM)])))
    return packed, table


def _rope_tables_direct(positions):
    B, T = positions.shape
    (freq, sign), _ = _rope_lane_rows()
    tab = jax.ShapeDtypeStruct((B, T, LANES), F32)
    spec = pl.BlockSpec((1, T, LANES), lambda b: (b, 0, 0))
    const = pl.BlockSpec((1, LANES), lambda b: (0, 0))
    pos = pl.BlockSpec((1, T // 2, 1), lambda b: (b, 0, 0))
    return pl.pallas_call(
        _rope_table_kernel,
        grid=(B,),
        in_specs=[pos, pos, const, const],
        out_specs=[spec] * 2,
        out_shape=[tab] * 2,
        compiler_params=_cparams(("parallel",)),
        name="rope_tables",
    )(positions[:, 0::2].reshape(B, T // 2, 1), positions[:, 1::2].reshape(B, T // 2, 1),
      freq, sign)


def _rope_shift_kernel(off_ref, freq_ref, sign_ref, ct_ref, st_ref, cos_ref, sin_ref):
    ang = off_ref[0].astype(F32) * freq_ref[...]
    co, so = jnp.cos(ang), jnp.sin(ang) * sign_ref[...]
    ct, st = ct_ref[0], st_ref[0]
    cos_ref[0] = co * ct - so * st
    sin_ref[0] = so * ct + co * st


def _rope_tables(positions):
    B, T = positions.shape
    steps = jnp.arange(T, dtype=positions.dtype)[None, :]

    def shifted(pos):
        ct, st = _rope_tables_direct(steps)
        _, (freq, sign) = _rope_lane_rows()
        tab = jax.ShapeDtypeStruct((B, T, LANES), F32)
        spec = pl.BlockSpec((1, T, LANES), lambda b: (b, 0, 0))
        base = pl.BlockSpec((1, T, LANES), lambda b: (0, 0, 0))
        const = pl.BlockSpec((1, LANES), lambda b: (0, 0))
        return tuple(pl.pallas_call(
            _rope_shift_kernel,
            grid=(B,),
            in_specs=[pl.BlockSpec((1, 1, 1), lambda b: (b, 0, 0)), const, const, base, base],
            out_specs=[spec] * 2,
            out_shape=[tab] * 2,
            compiler_params=_cparams(("parallel",)),
            name="rope_shift",
        )(pos[:, :1].reshape(B, 1, 1), freq, sign, ct, st))

    contiguous = jnp.all(positions == positions[:, :1] + steps)
    return lax.cond(contiguous, shifted, lambda pos: tuple(_rope_tables_direct(pos)), positions)


def _proj_a_kernel(x_ref, c_ref, s_ref, g_ref, w_ref, *rest, dil, with_z, parts):
    if with_z:
        wz_ref, qkv_ref, z_ref = rest[:3]
    else:
        qkv_ref = rest[0]
    tm_all, D = x_ref.shape[1], x_ref.shape[2]
    tm = tm_all // parts
    tl = tm // dil
    nslab = D // LANES

    def by_class(load, stage):
        if dil <= MAX_ROW_STRIDE:
            return jnp.concatenate([load(pl.ds(r, tl, stride=dil)) for r in range(dil)], axis=0)
        inner = dil // MAX_ROW_STRIDE
        tq = tm // MAX_ROW_STRIDE
        for q0 in range(MAX_ROW_STRIDE):
            stage[q0 * tq:(q0 + 1) * tq, :] = load(pl.ds(q0, tq, stride=MAX_ROW_STRIDE))
        return jnp.concatenate(
            [stage[pl.ds(q0 * tq + q1, tl, stride=inner), :]
             for q1 in range(inner) for q0 in range(MAX_ROW_STRIDE)], axis=0)

    def inputs(i):
        rs = slice(i * tm, (i + 1) * tm)
        if dil == 1:
            return x_ref[0, rs, :], c_ref[0, rs, :], s_ref[0, rs, :]
        slabs, stage = rest[-2].at[i], rest[-1].at[i]
        for c in range(nslab):
            slabs[c] = x_ref[0, rs, c * LANES:(c + 1) * LANES]
        slabs[nslab] = c_ref[0, rs, :]
        slabs[nslab + 1] = s_ref[0, rs, :]
        p = [by_class(lambda idx, c=c: slabs[c, idx, :], stage.at[c]) for c in range(nslab + 2)]
        return jnp.concatenate(p[:nslab], axis=1), p[nslab], p[nslab + 1]

    lane = lax.broadcasted_iota(jnp.int32, (tm, LANES), 1)
    scale = A_HEAD_DIM ** -0.5 * LOG2E

    def project(i, x, cos, sin):
        h = _rms(x, g_ref[...]).astype(BF16)
        cos = jnp.where(lane < A_ROT_DIM, cos, 1.0)
        sin = jnp.where(lane < A_ROT_DIM, sin, 0.0)
        for s in range(3):
            acc = jnp.dot(h, w_ref[:, s * A_WIDTH:(s + 1) * A_WIDTH],
                          preferred_element_type=F32)
            for hd in range(A_HEADS):
                a = acc[:, hd * LANES:(hd + 1) * LANES]
                if s == 0:
                    a = _rot_half(a, 0, cos * scale, sin * scale)
                elif s == 1:
                    a = _rot_half(a, 0, cos, sin)
                a = a.astype(BF16)
                for r in range(dil):
                    qkv_ref[0, s * A_HEADS + hd, r, i * tl:(i + 1) * tl, :] = a[r * tl:(r + 1) * tl]
        if with_z:
            z_ref[0, i * tm:(i + 1) * tm, :] = jnp.dot(
                h, wz_ref[...], preferred_element_type=F32).astype(BF16)

    staged = [inputs(i) for i in range(parts)]
    for i in range(parts):
        project(i, *staged[i])


def _proj_a(x, cos, sin, gain, w, group, with_z, tm=512, parts=2):
    B, T, D = x.shape
    dil = A_DILATIONS[group]
    L = T // dil
    tm = tm * parts
    tl = tm // dil
    tok = lambda n: pl.BlockSpec((1, tm, n), lambda b, i: (b, i, 0))
    out_shape = [jax.ShapeDtypeStruct((B, 3 * A_HEADS, dil, L, LANES), BF16)]
    out_specs = [pl.BlockSpec((1, 3 * A_HEADS, dil, tl, LANES), lambda b, i: (b, 0, 0, i, 0))]
    if with_z:
        out_shape.append(jax.ShapeDtypeStruct((B, T, A_WIDTH), BF16))
        out_specs.append(tok(A_WIDTH))
    return pl.pallas_call(
        functools.partial(_proj_a_kernel, dil=dil, with_z=with_z, parts=parts),
        grid=(B, T // tm),
        in_specs=[tok(D), tok(LANES), tok(LANES),
                  pl.BlockSpec((1, D), lambda b, i: (0, 0)),
                  pl.BlockSpec((D, 3 * A_WIDTH), lambda b, i: (0, group))]
                 + ([pl.BlockSpec((D, A_WIDTH), lambda b, i: (0, 3 * len(A_DILATIONS)))]
                    if with_z else []),
        out_specs=out_specs,
        out_shape=out_shape,
        scratch_shapes=([pltpu.VMEM((parts, D // LANES + 2, tm // parts, LANES), F32)] * 2
                        if dil > 1 else []),
        compiler_params=_cparams(("parallel", "parallel")),
        name=f"proj_a_d{dil}",
    )(x, cos, sin, gain.reshape(1, D), *([w, w] if with_z else [w]))


def _attn_a_kernel(g0_ref, g1_ref, g2_ref, o_ref, onat, mnat, lnat, stage, *, chunk):
    for hd in range(o_ref.shape[1]):
        _attn_a_head(hd, g0_ref, g1_ref, g2_ref, o_ref, onat, mnat, lnat, stage, chunk)


def _attn_a_head(hd, g0_ref, g1_ref, g2_ref, o_ref, onat, mnat, lnat, stage, chunk):
    T = o_ref.shape[2]
    S = MAX_ROW_STRIDE
    seg = BAND // S
    nt = (((1,), (1,)), ((), ()))

    def grids(width):
        row = lax.broadcasted_iota(jnp.int32, (BAND, width), 0)
        return row, (row % seg) * S + row // seg, lax.broadcasted_iota(jnp.int32, (BAND, width), 1)

    pi, qi, ki = grids(2 * BAND)
    band_mask, band_perm = (ki >= pi) & (ki <= pi + BAND), (ki >= qi) & (ki <= qi + BAND)
    pi, qi, ki = grids(BAND)
    first_mask, first_perm = ki <= pi, ki <= qi
    perm = jnp.where(ki == qi, 1.0, 0.0).astype(BF16)
    nblk = T // BAND
    refs = (g0_ref, g1_ref, g2_ref)

    def key_lo(g, j):
        nb = T // A_DILATIONS[g] // BAND
        return (j - 1) * BAND if j % nb else j * BAND

    def scores(g, c0):
        ref, nb = refs[g], T // A_DILATIONS[g] // BAND
        js = range(c0, c0 + chunk)
        q = [ref[0, 0, hd, j * BAND:(j + 1) * BAND, :] for j in js]
        if A_DILATIONS[g] == 1:
            q = [jnp.dot(perm, x, preferred_element_type=F32).astype(BF16) for x in q]
            band, first = band_perm, first_perm
        else:
            band, first = band_mask, first_mask
        s = [lax.dot_general(x, ref[0, 1, hd, key_lo(g, j):(j + 1) * BAND, :], nt,
                             preferred_element_type=F32) for j, x in zip(js, q)]
        s = [jnp.where(band if j % nb else first, x, -jnp.inf) for j, x in zip(js, s)]
        m = [jnp.max(x, axis=-1, keepdims=True) for x in s]
        return s, m

    def class_rows(ref, lead, j):
        if len(ref.shape) == 3:
            return jnp.concatenate([ref[lead, r * (T // S) + j * seg:r * (T // S) + (j + 1) * seg, :]
                                    for r in range(S)], axis=0)
        return jnp.concatenate([ref[lead, r, j * seg:(j + 1) * seg, :] for r in range(S)], axis=0)

    def values(g, c0, s, m):
        ref, dil = refs[g], A_DILATIONS[g]
        nb = T // dil // BAND
        js = range(c0, c0 + chunk)
        p = [jnp.exp2(x - y).astype(BF16) for x, y in zip(s, m)]
        o = [jnp.dot(x, jnp.concatenate([ref[0, 2, hd, key_lo(g, j):(j + 1) * BAND, :],
                                         jnp.ones(((j + 1) * BAND - key_lo(g, j), LANES), BF16)],
                                        axis=1),
                     preferred_element_type=F32) for x, j in zip(p, js)]
        for j, oj, mj in zip(js, o, m):
            r, n = divmod(j, nb)
            mj = jnp.broadcast_to(mj, (BAND, LANES))
            oj, lj = oj[:, :LANES], oj[:, LANES:]
            if dil == 1:
                o1, m1, l1 = (class_rows(a, 0, j) for a in (onat, mnat, lnat))
                o2, m2, l2 = (class_rows(stage, a, j) for a in range(3))
                mx = jnp.maximum(jnp.maximum(mj, m1), m2)
                e0, e1, e2 = jnp.exp2(mj - mx), jnp.exp2(m1 - mx), jnp.exp2(m2 - mx)
                num = e0 * oj + e1 * o1 + e2 * o2
                den = e0 * lj + e1 * l1 + e2 * l2
                merged = num / den
                for c in range(S):
                    onat[1, pl.ds(j * BAND + c, seg, stride=S), :] = merged[c * seg:(c + 1) * seg]
            elif dil == S:
                for a, val in zip((onat, mnat, lnat), (oj, mj, lj)):
                    a[0, j * BAND:(j + 1) * BAND, :] = val
            else:
                inner = dil // S
                r1, r0 = divmod(r, S)
                idx = pl.ds(r1 + n * BAND * inner, BAND, stride=inner)
                stage[0, r0, idx, :] = oj
                stage[1, r0, idx, :] = mj
                stage[2, r0, idx, :] = lj

    order = sorted(range(len(refs)), key=lambda g: -A_DILATIONS[g])
    assert [A_DILATIONS[g] for g in order] == [S * S, S, 1]
    work = [(g, c0) for g in order for c0 in range(0, nblk, chunk)]
    nxt = scores(*work[0])
    for w, (g, c0) in enumerate(work):
        cur = nxt
        if w + 1 < len(work):
            nxt = scores(*work[w + 1])
        values(g, c0, *cur)
    for j in range(nblk):
        o_ref[0, hd, j * BAND:(j + 1) * BAND, :] = onat[1, j * BAND:(j + 1) * BAND, :].astype(BF16)


def _attn_a(qkv, chunk=8, heads=2):
    B, T = qkv[0].shape[0], qkv[0].shape[2] * qkv[0].shape[3]
    views = [a.reshape(B, 3, A_HEADS, T, LANES) for a in qkv]
    spec = pl.BlockSpec((1, 3, heads, T, LANES), lambda b, h: (b, 0, h, 0, 0))
    return pl.pallas_call(
        functools.partial(_attn_a_kernel, chunk=chunk),
        grid=(B, A_HEADS // heads),
        in_specs=[spec] * 3,
        out_specs=pl.BlockSpec((1, heads, T, LANES), lambda b, h: (b, h, 0, 0)),
        out_shape=jax.ShapeDtypeStruct((B, A_HEADS, T, LANES), BF16),
        scratch_shapes=[pltpu.VMEM((len(A_DILATIONS) - 1, T, LANES), F32)] * 3
                       + [pltpu.VMEM((3, MAX_ROW_STRIDE, T // MAX_ROW_STRIDE, LANES), F32)],
        compiler_params=_cparams(("parallel", "parallel")),
        name="attn_a",
    )(*views)


def _gated(o_ref, z_ref):
    o = jnp.concatenate([o_ref[0, h] for h in range(o_ref.shape[1])], axis=1)
    z = z_ref[0].astype(F32)
    return (o.astype(F32) * (z * jax.nn.sigmoid(z))).astype(BF16)


def _mid_kernel(o_ref, z_ref, x_ref, cb_ref, sb_ref,
                wout_ref, gpost_ref, gkv_ref, wd_ref, glat_ref, wup_ref,
                gpre_ref, wcq_ref, wz_ref, gq_ref, wqup_ref,
                h1_ref, k_ref, v_ref, q_ref, zb_ref, *, parts):
    tm = x_ref.shape[1]
    tp = tm // parts
    lane = lax.broadcasted_iota(jnp.int32, (tp, LANES), 1)
    one_col = jnp.where(lane == 0, 1.0, 0.0)
    scale = B_QK_DIM ** -0.5 * LOG2E
    st = [dict() for _ in range(parts)]

    def stage1(i):
        rs = slice(i * tp, (i + 1) * tp)
        o = jnp.concatenate([o_ref[0, h, rs, :] for h in range(o_ref.shape[1])], axis=1)
        z = z_ref[0, rs, :].astype(F32)
        g = (o.astype(F32) * (z * jax.nn.sigmoid(z))).astype(BF16)
        y = jnp.dot(g, wout_ref[...], preferred_element_type=F32)
        h1 = x_ref[0, rs, :] + _rms(y, gpost_ref[...])
        h1_ref[0, rs, :] = h1
        st[i]["h1"] = h1

    def stage2(i):
        h1 = st[i].pop("h1")
        hn = _rms(h1, gkv_ref[...]).astype(BF16)
        hb = _rms(h1, gpre_ref[...]).astype(BF16)
        st[i]["ckr"] = jnp.dot(hn, wd_ref[...], preferred_element_type=F32)
        st[i]["cq"] = jnp.dot(hb, wcq_ref[...], preferred_element_type=F32)
        st[i]["hb"] = hb

    def stage3(i):
        rs = slice(i * tp, (i + 1) * tp)
        ckr = st[i].pop("ckr")
        c_kv = _rms(ckr[:, :B_KV_LORA], glat_ref[...]).astype(BF16)
        c_q = _rms(st[i].pop("cq"), gq_ref[...]).astype(BF16)
        st[i]["kv"] = jnp.dot(c_kv, wup_ref[...], preferred_element_type=F32)
        st[i]["qq"] = jnp.dot(c_q, wqup_ref[...], preferred_element_type=F32)
        zb_ref[0, rs, :] = jnp.dot(st[i].pop("hb"), wz_ref[...],
                                   preferred_element_type=F32).astype(BF16)
        st[i]["kr"] = ckr[:, B_KV_LORA:]

    def stage4(i):
        rs = slice(i * tp, (i + 1) * tp)
        cos = jnp.where(lane >= B_NOPE, cb_ref[0, rs, :], 1.0)
        sin = jnp.where(lane >= B_NOPE, sb_ref[0, rs, :], 0.0)
        k_rope = _rot_half(st[i].pop("kr"), B_NOPE, cos, sin)
        kv, qq = st[i].pop("kv"), st[i].pop("qq")
        cq, sq = cos * scale, sin * scale
        for h in range(B_HEADS):
            blk = kv[:, h * LANES:(h + 1) * LANES]
            k_ref[0, h, rs, :] = jnp.where(lane < B_NOPE, blk, k_rope).astype(BF16)
            v_ref[0, h, rs, :] = jnp.where(lane >= B_NOPE, blk, one_col).astype(BF16)
            q_ref[0, h, rs, :] = _rot_half(qq[:, h * LANES:(h + 1) * LANES],
                                           B_NOPE, cq, sq).astype(BF16)

    for stage in (stage1, stage2, stage3, stage4):
        for i in range(parts):
            stage(i)


def _mid(o_a, z_a, x, cos, sin, weights, tm=512, parts=2):
    B, T, D = x.shape
    tok = lambda w: pl.BlockSpec((1, tm, w), lambda b, i: (b, i, 0))
    heads = lambda n: pl.BlockSpec((1, n, tm, LANES), lambda b, i: (b, 0, i, 0))
    const = lambda a: pl.BlockSpec(a.shape, lambda b, i: (0, 0), pipeline_mode=pl.Buffered(1))
    return pl.pallas_call(
        functools.partial(_mid_kernel, parts=parts),
        grid=(B, T // tm),
        in_specs=[heads(A_HEADS), tok(A_WIDTH), tok(D), tok(LANES), tok(LANES)]
                 + [const(w) for w in weights],
        out_specs=[tok(D), heads(B_HEADS), heads(B_HEADS), heads(B_HEADS), tok(B_WIDTH)],
        out_shape=[
            jax.ShapeDtypeStruct((B, T, D), F32),
            jax.ShapeDtypeStruct((B, B_HEADS, T, LANES), BF16),
            jax.ShapeDtypeStruct((B, B_HEADS, T, LANES), BF16),
            jax.ShapeDtypeStruct((B, B_HEADS, T, LANES), BF16),
            jax.ShapeDtypeStruct((B, T, B_WIDTH), BF16),
        ],
        compiler_params=_cparams(("parallel", "parallel")),
        name="mid",
    )(o_a, z_a, x, cos, sin, *weights)


def _attn_b_kernel(q_ref, k_ref, v_ref, o_ref, *, tq, ahead):
    T = k_ref.shape[2]
    qi = lax.broadcasted_iota(jnp.int32, (tq, tq), 0)
    ki = lax.broadcasted_iota(jnp.int32, (tq, tq), 1)
    causal = ki <= qi
    lane = lax.broadcasted_iota(jnp.int32, (tq, LANES), 1)
    nt = (((1,), (1,)), ((), ()))
    units = [(ii, hh) for ii in range(T // tq) for hh in range(2)]

    def scores(ii, hh):
        lo = ii * tq
        q = q_ref[0, hh, lo:lo + tq, :]
        s_d = lax.dot_general(q, k_ref[0, hh, lo:lo + tq, :], nt, preferred_element_type=F32)
        s_d = jnp.where(causal, s_d, -jnp.inf)
        m = jnp.max(s_d, axis=-1, keepdims=True)
        s_m = None
        if ii:
            s_m = lax.dot_general(q, k_ref[0, hh, 0:lo, :], nt, preferred_element_type=F32)
            m = jnp.maximum(m, jnp.max(s_m, axis=-1, keepdims=True))
        return s_d, s_m, m

    def values(ii, hh, s_d, s_m, m):
        lo = ii * tq
        o = jnp.dot(jnp.exp2(s_d - m).astype(BF16), v_ref[0, hh, lo:lo + tq, :],
                    preferred_element_type=F32)
        if ii:
            o = o + jnp.dot(jnp.exp2(s_m - m).astype(BF16), v_ref[0, hh, 0:lo, :],
                            preferred_element_type=F32)
        l = jnp.sum(jnp.where(lane == 0, o, 0.0), axis=-1, keepdims=True)
        return o / l

    pending = [scores(*units[u]) for u in range(ahead)]
    outs = []
    for u, (ii, hh) in enumerate(units):
        cur = pending.pop(0)
        if u + ahead < len(units):
            pending.append(scores(*units[u + ahead]))
        outs.append(values(ii, hh, *cur))
        if hh:
            lo = ii * tq
            o_ref[0, 0, lo:lo + tq, :] = jnp.where(
                lane < B_VDIM, pltpu.roll(outs[0], B_VDIM, 1), outs[1]).astype(BF16)
            outs = []


def _attn_b(q, k, v, tq=256, ahead=3):
    B, _, T, _ = q.shape
    heads = pl.BlockSpec((1, 2, T, LANES), lambda b, p: (b, p, 0, 0))
    return pl.pallas_call(
        functools.partial(_attn_b_kernel, tq=tq, ahead=ahead),
        grid=(B, B_PAIRS),
        in_specs=[heads, heads, heads],
        out_specs=pl.BlockSpec((1, 1, T, LANES), lambda b, p: (b, p, 0, 0)),
        out_shape=jax.ShapeDtypeStruct((B, B_PAIRS, T, LANES), BF16),
        compiler_params=_cparams(("parallel", "parallel")),
        name="attn_b",
    )(q, k, v)


def _final_kernel(o_ref, z_ref, h1_ref, w_ref, g_ref, out_ref):
    y = jnp.dot(_gated(o_ref, z_ref), w_ref[...], preferred_element_type=F32)
    out_ref[0] = h1_ref[0] + _rms(y, g_ref[...])


def _final(o_b, z_b, h1, w, gain, tm=1024):
    B, T, D = h1.shape
    tok = lambda w_: pl.BlockSpec((1, tm, w_), lambda b, i: (b, i, 0))
    return pl.pallas_call(
        _final_kernel,
        grid=(B, T // tm),
        in_specs=[pl.BlockSpec((1, B_PAIRS, tm, LANES), lambda b, i: (b, 0, i, 0)),
                  tok(B_WIDTH), tok(D),
                  pl.BlockSpec(w.shape, lambda b, i: (0, 0)),
                  pl.BlockSpec((1, D), lambda b, i: (0, 0))],
        out_specs=tok(D),
        out_shape=jax.ShapeDtypeStruct((B, T, D), F32),
        compiler_params=_cparams(("parallel", "parallel")),
        name="final",
    )(o_b, z_b, h1, w, gain.reshape(1, D))


def kernel(x, positions, a_pre_norm, a_w_in, a_w_out, a_post_norm, kv_norm, kv_w_down, kv_latent_norm, kv_w_up, b_pre_norm, b_w_in, b_q_norm, b_w_q_up, b_w_out, b_post_norm):
    B, T, D = x.shape
    assert all(w // d == BAND for w, d in zip(A_WINDOWS, A_DILATIONS))
    assert D == A_WIDTH and T % (BAND * max(A_DILATIONS)) == 0 and positions.shape == (B, T)
    cos, sin = _rope_tables(positions)

    w_in = a_w_in[0].astype(BF16)
    qkv = []
    z_a = None
    for g, dil in enumerate(A_DILATIONS):
        if dil == 1:
            qkv_g, z_a = _proj_a(x, cos, sin, a_pre_norm[0], w_in, g, True)
        else:
            (qkv_g,) = _proj_a(x, cos, sin, a_pre_norm[0], w_in, g, False)
        qkv.append(qkv_g)
    o_a = _attn_a(qkv)

    row = lambda g: g.reshape(1, -1)
    wd = jnp.zeros((D, B_KV_LORA + LANES), F32)
    wd = wd.at[:, :B_KV_LORA].set(kv_w_down[:, :B_KV_LORA])
    wd = wd.at[:, B_KV_LORA + B_NOPE:B_KV_LORA + B_QK_DIM].set(kv_w_down[:, B_KV_LORA:])
    wqup = jnp.pad(b_w_q_up[0].reshape(B_Q_LORA, B_HEADS, B_QK_DIM),
                   ((0, 0), (0, 0), (0, LANES - B_QK_DIM))).reshape(B_Q_LORA, B_HEADS * LANES)
    weights = [
        a_w_out[0].astype(BF16), row(a_post_norm[0]), row(kv_norm), wd.astype(BF16),
        row(kv_latent_norm), kv_w_up.astype(BF16),
        row(b_pre_norm[0]), b_w_in[0][:, :B_Q_LORA].astype(BF16),
        b_w_in[0][:, B_Q_LORA:].astype(BF16), row(b_q_norm[0]), wqup.astype(BF16),
    ]
    h1, k_b, v_b, q_b, z_b = _mid(o_a, z_a, x, cos, sin, weights)

    o_b = _attn_b(q_b, k_b, v_b)
    return _final(o_b, z_b, h1, b_w_out[0].astype(BF16), b_post_norm[0])
```

```python
import functools

import jax
import jax.numpy as jnp
from jax import lax
from jax.experimental import pallas as pl
from jax.experimental.pallas import tpu as pltpu

F32 = jnp.float32
BF16 = jnp.bfloat16

NORM_EPS = 1e-6
LOG2E = 1.4426950408889634
LANES = 128

A_WINDOWS = (128, 512, 2048)
A_DILATIONS = (1, 4, 16)
A_HEADS = 8
A_HEAD_DIM = 128
A_WIDTH = A_HEADS * A_HEAD_DIM
A_ROT_DIM = A_HEAD_DIM // 4
A_ROPE_THETA = 500000.0
BAND = 128
MAX_ROW_STRIDE = 4

B_HEADS = 16
B_NOPE = 64
B_ROPE = 32
B_QK_DIM = B_NOPE + B_ROPE
B_VDIM = 64
B_WIDTH = B_HEADS * B_VDIM
B_Q_LORA = 384
B_KV_LORA = 256
B_ROPE_THETA = 10000.0
B_PAIRS = B_HEADS // 2

VMEM_LIMIT = 56 * 1024 * 1024


def _cparams(sem):
    return pltpu.CompilerParams(dimension_semantics=sem, vmem_limit_bytes=VMEM_LIMIT)


def _rms(x, g):
    ms = jnp.mean(x * x, axis=-1, keepdims=True)
    return x * lax.rsqrt(ms + NORM_EPS) * g


def _rot_half(a, lo, cos, sin):
    lane = lax.broadcasted_iota(jnp.int32, a.shape, 1)
    partner = jnp.where(lane < lo + 16, pltpu.roll(a, LANES - 16, 1), pltpu.roll(a, 16, 1))
    return a * cos + partner * sin


def _rope_table_kernel(pe_ref, po_ref, freq_ref, sign_ref, cos_ref, sin_ref):
    half_t = pe_ref.shape[1]
    lane = lax.broadcasted_iota(jnp.int32, (half_t, LANES), 1)
    pos = jnp.where(lane < LANES // 2, pe_ref[0].astype(F32), po_ref[0].astype(F32))
    ang = pos * freq_ref[...]
    for ref, packed, ident in ((cos_ref, jnp.cos(ang), 1.0),
                               (sin_ref, jnp.sin(ang) * sign_ref[...], 0.0)):
        def unpack(lo0, lo1):
            a = packed if lo0 == 0 else pltpu.roll(packed, LANES - lo0, 1)
            b_ = pltpu.roll(packed, (B_NOPE - lo1) % LANES, 1)
            return jnp.where(lane < A_ROT_DIM, a,
                             jnp.where((lane >= B_NOPE) & (lane < B_QK_DIM), b_, ident))
        ref[0, pl.ds(0, half_t, stride=2), :] = unpack(0, A_ROT_DIM)
        ref[0, pl.ds(1, half_t, stride=2), :] = unpack(LANES // 2, LANES // 2 + A_ROT_DIM)


def _rope_lane_rows():
    half = A_ROT_DIM // 2
    inv_a = 1.0 / (A_ROPE_THETA ** (jnp.arange(half, dtype=F32) * (2.0 / A_ROT_DIM)))
    inv_b = 1.0 / (B_ROPE_THETA ** (jnp.arange(B_ROPE // 2, dtype=F32) * (2.0 / B_ROPE)))
    ones = jnp.ones((half,), F32)
    zeros = lambda n: jnp.zeros((n,), F32)
    row = lambda v: v.reshape(1, LANES)
    packed = (row(jnp.tile(jnp.concatenate([inv_a, inv_a, inv_b, inv_b]), 2)),
              row(jnp.tile(jnp.concatenate([-ones, ones, -ones, ones]), 2)))
    table = (row(jnp.concatenate([inv_a, inv_a, zeros(B_NOPE - A_ROT_DIM), inv_b, inv_b,
                                  zeros(LANES - B_QK_DIM)])),
             row(jnp.concatenate([-ones, ones, zeros(B_NOPE - A_ROT_DIM), -ones, ones,
                                  zeros(LANES - B_QK_DIM)])))
    return packed, table


def _rope_tables_direct(positions):
    B, T = positions.shape
    (freq, sign), _ = _rope_lane_rows()
    tab = jax.ShapeDtypeStruct((B, T, LANES), F32)
    spec = pl.BlockSpec((1, T, LANES), lambda b: (b, 0, 0))
    const = pl.BlockSpec((1, LANES), lambda b: (0, 0))
    pos = pl.BlockSpec((1, T // 2, 1), lambda b: (b, 0, 0))
    return pl.pallas_call(
        _rope_table_kernel,
        grid=(B,),
        in_specs=[pos, pos, const, const],
        out_specs=[spec] * 2,
        out_shape=[tab] * 2,
        compiler_params=_cparams(("parallel",)),
        name="rope_tables",
    )(positions[:, 0::2].reshape(B, T // 2, 1), positions[:, 1::2].reshape(B, T // 2, 1),
      freq, sign)


def _rope_shift_kernel(off_ref, freq_ref, sign_ref, ct_ref, st_ref, cos_ref, sin_ref):
    ang = off_ref[0].astype(F32) * freq_ref[...]
    co, so = jnp.cos(ang), jnp.sin(ang) * sign_ref[...]
    ct, st = ct_ref[0], st_ref[0]
    cos_ref[0] = co * ct - so * st
    sin_ref[0] = so * ct + co * st


def _rope_tables(positions):
    B, T = positions.shape
    steps = jnp.arange(T, dtype=positions.dtype)[None, :]

    def shifted(pos):
        ct, st = _rope_tables_direct(steps)
        _, (freq, sign) = _rope_lane_rows()
        tab = jax.ShapeDtypeStruct((B, T, LANES), F32)
        spec = pl.BlockSpec((1, T, LANES), lambda b: (b, 0, 0))
        base = pl.BlockSpec((1, T, LANES), lambda b: (0, 0, 0))
        const = pl.BlockSpec((1, LANES), lambda b: (0, 0))
        return tuple(pl.pallas_call(
            _rope_shift_kernel,
            grid=(B,),
            in_specs=[pl.BlockSpec((1, 1, 1), lambda b: (b, 0, 0)), const, const, base, base],
            out_specs=[spec] * 2,
            out_shape=[tab] * 2,
            compiler_params=_cparams(("parallel",)),
            name="rope_shift",
        )(pos[:, :1].reshape(B, 1, 1), freq, sign, ct, st))

    contiguous = jnp.all(positions == positions[:, :1] + steps)
    return lax.cond(contiguous, shifted, lambda pos: tuple(_rope_tables_direct(pos)), positions)


def _proj_a_kernel(x_ref, c_ref, s_ref, g_ref, w_ref, *rest, dil, with_z, parts):
    if with_z:
        wz_ref, qkv_ref, z_ref = rest[:3]
    else:
        qkv_ref = rest[0]
    tm_all, D = x_ref.shape[1], x_ref.shape[2]
    tm = tm_all // parts
    tl = tm // dil
    nslab = D // LANES

    def by_class(load, stage):
        if dil <= MAX_ROW_STRIDE:
            return jnp.concatenate([load(pl.ds(r, tl, stride=dil)) for r in range(dil)], axis=0)
        inner = dil // MAX_ROW_STRIDE
        tq = tm // MAX_ROW_STRIDE
        for q0 in range(MAX_ROW_STRIDE):
            stage[q0 * tq:(q0 + 1) * tq, :] = load(pl.ds(q0, tq, stride=MAX_ROW_STRIDE))
        return jnp.concatenate(
            [stage[pl.ds(q0 * tq + q1, tl, stride=inner), :]
             for q1 in range(inner) for q0 in range(MAX_ROW_STRIDE)], axis=0)

    def inputs(i):
        rs = slice(i * tm, (i + 1) * tm)
        if dil == 1:
            return x_ref[0, rs, :], c_ref[0, rs, :], s_ref[0, rs, :]
        slabs, stage = rest[-2].at[i], rest[-1].at[i]
        for c in range(nslab):
            slabs[c] = x_ref[0, rs, c * LANES:(c + 1) * LANES]
        slabs[nslab] = c_ref[0, rs, :]
        slabs[nslab + 1] = s_ref[0, rs, :]
        p = [by_class(lambda idx, c=c: slabs[c, idx, :], stage.at[c]) for c in range(nslab + 2)]
        return jnp.concatenate(p[:nslab], axis=1), p[nslab], p[nslab + 1]

    lane = lax.broadcasted_iota(jnp.int32, (tm, LANES), 1)
    scale = A_HEAD_DIM ** -0.5 * LOG2E

    def project(i, x, cos, sin):
        h = _rms(x, g_ref[...]).astype(BF16)
        cos = jnp.where(lane < A_ROT_DIM, cos, 1.0)
        sin = jnp.where(lane < A_ROT_DIM, sin, 0.0)
        for s in range(3):
            acc = jnp.dot(h, w_ref[:, s * A_WIDTH:(s + 1) * A_WIDTH],
                          preferred_element_type=F32)
            for hd in range(A_HEADS):
                a = acc[:, hd * LANES:(hd + 1) * LANES]
                if s == 0:
                    a = _rot_half(a, 0, cos * scale, sin * scale)
                elif s == 1:
                    a = _rot_half(a, 0, cos, sin)
                a = a.astype(BF16)
                for r in range(dil):
                    qkv_ref[0, s * A_HEADS + hd, r, i * tl:(i + 1) * tl, :] = a[r * tl:(r + 1) * tl]
        if with_z:
            z_ref[0, i * tm:(i + 1) * tm, :] = jnp.dot(
                h, wz_ref[...], preferred_element_type=F32).astype(BF16)

    staged = [inputs(i) for i in range(parts)]
    for i in range(parts):
        project(i, *staged[i])


def _proj_a(x, cos, sin, gain, w, group, with_z, tm=512, parts=2):
    B, T, D = x.shape
    dil = A_DILATIONS[group]
    L = T // dil
    tm = tm * parts
    tl = tm // dil
    tok = lambda n: pl.BlockSpec((1, tm, n), lambda b, i: (b, i, 0))
    out_shape = [jax.ShapeDtypeStruct((B, 3 * A_HEADS, dil, L, LANES), BF16)]
    out_specs = [pl.BlockSpec((1, 3 * A_HEADS, dil, tl, LANES), lambda b, i: (b, 0, 0, i, 0))]
    if with_z:
        out_shape.append(jax.ShapeDtypeStruct((B, T, A_WIDTH), BF16))
        out_specs.append(tok(A_WIDTH))
    return pl.pallas_call(
        functools.partial(_proj_a_kernel, dil=dil, with_z=with_z, parts=parts),
        grid=(B, T // tm),
        in_specs=[tok(D), tok(LANES), tok(LANES),
                  pl.BlockSpec((1, D), lambda b, i: (0, 0)),
                  pl.BlockSpec((D, 3 * A_WIDTH), lambda b, i: (0, group))]
                 + ([pl.BlockSpec((D, A_WIDTH), lambda b, i: (0, 3 * len(A_DILATIONS)))]
                    if with_z else []),
        out_specs=out_specs,
        out_shape=out_shape,
        scratch_shapes=([pltpu.VMEM((parts, D // LANES + 2, tm // parts, LANES), F32)] * 2
                        if dil > 1 else []),
        compiler_params=_cparams(("parallel", "parallel")),
        name=f"proj_a_d{dil}",
    )(x, cos, sin, gain.reshape(1, D), *([w, w] if with_z else [w]))


def _attn_a_kernel(g0_ref, g1_ref, g2_ref, o_ref, onat, mnat, lnat, stage, *, chunk):
    for hd in range(o_ref.shape[1]):
        _attn_a_head(hd, g0_ref, g1_ref, g2_ref, o_ref, onat, mnat, lnat, stage, chunk)


def _attn_a_head(hd, g0_ref, g1_ref, g2_ref, o_ref, onat, mnat, lnat, stage, chunk):
    T = o_ref.shape[2]
    nt = (((1,), (1,)), ((), ()))
    qi = lax.broadcasted_iota(jnp.int32, (BAND, 2 * BAND), 0)
    ki = lax.broadcasted_iota(jnp.int32, (BAND, 2 * BAND), 1)
    band_mask = (ki >= qi) & (ki <= qi + BAND)
    first_mask = (lax.broadcasted_iota(jnp.int32, (BAND, BAND), 1)
                  <= lax.broadcasted_iota(jnp.int32, (BAND, BAND), 0))
    nblk = T // BAND
    refs = (g0_ref, g1_ref, g2_ref)

    def key_lo(g, j):
        nb = T // A_DILATIONS[g] // BAND
        return (j - 1) * BAND if j % nb else j * BAND

    def scores(g, c0):
        ref, nb = refs[g], T // A_DILATIONS[g] // BAND
        js = range(c0, c0 + chunk)
        s = [lax.dot_general(ref[0, 0, hd, j * BAND:(j + 1) * BAND, :],
                             ref[0, 1, hd, key_lo(g, j):(j + 1) * BAND, :], nt,
                             preferred_element_type=F32) for j in js]
        s = [jnp.where(band_mask if j % nb else first_mask, x, -jnp.inf) for j, x in zip(js, s)]
        m = [jnp.max(x, axis=-1, keepdims=True) for x in s]
        return s, m

    def values(g, c0, s, m):
        ref, dil = refs[g], A_DILATIONS[g]
        nb = T // dil // BAND
        js = range(c0, c0 + chunk)
        p = [jnp.exp2(x - y).astype(BF16) for x, y in zip(s, m)]
        o = [jnp.dot(x, jnp.concatenate([ref[0, 2, hd, key_lo(g, j):(j + 1) * BAND, :],
                                         jnp.ones(((j + 1) * BAND - key_lo(g, j), LANES), BF16)],
                                        axis=1),
                     preferred_element_type=F32) for x, j in zip(p, js)]
        for j, oj, mj in zip(js, o, m):
            r, n = divmod(j, nb)
            mj = jnp.broadcast_to(mj, (BAND, LANES))
            oj, lj = oj[:, :LANES], oj[:, LANES:]
            if dil == 1:
                sl = slice(j * BAND, (j + 1) * BAND)
                m1, m2 = mnat[0, sl, :], mnat[1, sl, :]
                mx = jnp.maximum(jnp.maximum(mj, m1), m2)
                e0, e1, e2 = jnp.exp2(mj - mx), jnp.exp2(m1 - mx), jnp.exp2(m2 - mx)
                num = e0 * oj + e1 * onat[0, sl, :] + e2 * onat[1, sl, :]
                den = e0 * lj + e1 * lnat[0, sl, :] + e2 * lnat[1, sl, :]
                o_ref[0, hd, sl, :] = (num / den).astype(BF16)
            elif dil > MAX_ROW_STRIDE:
                inner = dil // MAX_ROW_STRIDE
                r1, r0 = divmod(r, MAX_ROW_STRIDE)
                idx = pl.ds(r1 + n * BAND * inner, BAND, stride=inner)
                stage[0, r0, idx, :] = oj
                stage[1, r0, idx, :] = mj
                stage[2, r0, idx, :] = lj
            else:
                idx = pl.ds(r + n * BAND * dil, BAND, stride=dil)
                onat[g - 1, idx, :] = oj
                mnat[g - 1, idx, :] = mj
                lnat[g - 1, idx, :] = lj

    def second_pass(g):
        for a, nat in enumerate((onat, mnat, lnat)):
            for r0 in range(MAX_ROW_STRIDE):
                nat[g - 1, pl.ds(r0, T // MAX_ROW_STRIDE, stride=MAX_ROW_STRIDE), :] = stage[a, r0]

    order = sorted(range(len(refs)), key=lambda g: -A_DILATIONS[g])
    assert A_DILATIONS[order[-1]] == 1
    work = [(g, c0) for g in order for c0 in range(0, nblk, chunk)]
    nxt = scores(*work[0])
    for w, (g, c0) in enumerate(work):
        cur = nxt
        if w + 1 < len(work):
            nxt = scores(*work[w + 1])
        values(g, c0, *cur)
        last_of_group = w + 1 == len(work) or work[w + 1][0] != g
        if last_of_group and A_DILATIONS[g] > MAX_ROW_STRIDE:
            second_pass(g)


def _attn_a(qkv, chunk=8, heads=2):
    B, T = qkv[0].shape[0], qkv[0].shape[2] * qkv[0].shape[3]
    views = [a.reshape(B, 3, A_HEADS, T, LANES) for a in qkv]
    spec = pl.BlockSpec((1, 3, heads, T, LANES), lambda b, h: (b, 0, h, 0, 0))
    return pl.pallas_call(
        functools.partial(_attn_a_kernel, chunk=chunk),
        grid=(B, A_HEADS // heads),
        in_specs=[spec] * 3,
        out_specs=pl.BlockSpec((1, heads, T, LANES), lambda b, h: (b, h, 0, 0)),
        out_shape=jax.ShapeDtypeStruct((B, A_HEADS, T, LANES), BF16),
        scratch_shapes=[pltpu.VMEM((len(A_DILATIONS) - 1, T, LANES), F32)] * 3
                       + [pltpu.VMEM((3, MAX_ROW_STRIDE, T // MAX_ROW_STRIDE, LANES), F32)],
        compiler_params=_cparams(("parallel", "parallel")),
        name="attn_a",
    )(*views)


def _mid_kernel(o_ref, z_ref, x_ref, cb_ref, sb_ref,
                wout_ref, gpost_ref, gkv_ref, wd_ref, glat_ref, wup_ref,
                gpre_ref, wcq_ref, wz_ref, gq_ref, wqup_ref,
                h1_ref, k_ref, v_ref, q_ref, zb_ref, *, parts):
    tm = x_ref.shape[1]
    tp = tm // parts
    lane = lax.broadcasted_iota(jnp.int32, (tp, LANES), 1)
    one_col = jnp.where(lane == 0, 1.0, 0.0)
    scale = B_QK_DIM ** -0.5 * LOG2E
    st = [dict() for _ in range(parts)]

    def stage1(i):
        rs = slice(i * tp, (i + 1) * tp)
        o = jnp.concatenate([o_ref[0, h, rs, :] for h in range(o_ref.shape[1])], axis=1)
        z = z_ref[0, rs, :].astype(F32)
        g = (o.astype(F32) * (z * jax.nn.sigmoid(z))).astype(BF16)
        y = jnp.dot(g, wout_ref[...], preferred_element_type=F32)
        h1 = x_ref[0, rs, :] + _rms(y, gpost_ref[...])
        h1_ref[0, rs, :] = h1
        st[i]["h1"] = h1

    def stage2(i):
        h1 = st[i].pop("h1")
        hn = _rms(h1, gkv_ref[...]).astype(BF16)
        hb = _rms(h1, gpre_ref[...]).astype(BF16)
        st[i]["ckr"] = jnp.dot(hn, wd_ref[...], preferred_element_type=F32)
        st[i]["cq"] = jnp.dot(hb, wcq_ref[...], preferred_element_type=F32)
        st[i]["hb"] = hb

    def stage3(i):
        rs = slice(i * tp, (i + 1) * tp)
        ckr = st[i].pop("ckr")
        c_kv = _rms(ckr[:, :B_KV_LORA], glat_ref[...]).astype(BF16)
        c_q = _rms(st[i].pop("cq"), gq_ref[...]).astype(BF16)
        st[i]["kv"] = jnp.dot(c_kv, wup_ref[...], preferred_element_type=F32)
        st[i]["qq"] = jnp.dot(c_q, wqup_ref[...], preferred_element_type=F32)
        zb_ref[0, rs, :] = jnp.dot(st[i].pop("hb"), wz_ref[...],
                                   preferred_element_type=F32).astype(BF16)
        st[i]["kr"] = ckr[:, B_KV_LORA:]

    def stage4(i):
        rs = slice(i * tp, (i + 1) * tp)
        cos = jnp.where(lane >= B_NOPE, cb_ref[0, rs, :], 1.0)
        sin = jnp.where(lane >= B_NOPE, sb_ref[0, rs, :], 0.0)
        k_rope = _rot_half(st[i].pop("kr"), B_NOPE, cos, sin)
        kv, qq = st[i].pop("kv"), st[i].pop("qq")
        cq, sq = cos * scale, sin * scale
        for h in range(B_HEADS):
            blk = kv[:, h * LANES:(h + 1) * LANES]
            k_ref[0, h, rs, :] = jnp.where(lane < B_NOPE, blk, k_rope).astype(BF16)
            v_ref[0, h, rs, :] = jnp.where(lane >= B_NOPE, blk, one_col).astype(BF16)
            q_ref[0, h, rs, :] = _rot_half(qq[:, h * LANES:(h + 1) * LANES],
                                           B_NOPE, cq, sq).astype(BF16)

    for stage in (stage1, stage2, stage3, stage4):
        for i in range(parts):
            stage(i)


def _mid(o_a, z_a, x, cos, sin, weights, tm=512, parts=2):
    B, T, D = x.shape
    tok = lambda w: pl.BlockSpec((1, tm, w), lambda b, i: (b, i, 0))
    heads = lambda n: pl.BlockSpec((1, n, tm, LANES), lambda b, i: (b, 0, i, 0))
    const = lambda a: pl.BlockSpec(a.shape, lambda b, i: (0, 0), pipeline_mode=pl.Buffered(1))
    return pl.pallas_call(
        functools.partial(_mid_kernel, parts=parts),
        grid=(B, T // tm),
        in_specs=[heads(A_HEADS), tok(A_WIDTH), tok(D), tok(LANES), tok(LANES)]
                 + [const(w) for w in weights],
        out_specs=[tok(D), heads(B_HEADS), heads(B_HEADS), heads(B_HEADS), tok(B_WIDTH)],
        out_shape=[
            jax.ShapeDtypeStruct((B, T, D), F32),
            jax.ShapeDtypeStruct((B, B_HEADS, T, LANES), BF16),
            jax.ShapeDtypeStruct((B, B_HEADS, T, LANES), BF16),
            jax.ShapeDtypeStruct((B, B_HEADS, T, LANES), BF16),
            jax.ShapeDtypeStruct((B, T, B_WIDTH), BF16),
        ],
        compiler_params=_cparams(("parallel", "parallel")),
        name="mid",
    )(o_a, z_a, x, cos, sin, *weights)


def _attn_b_kernel(q_ref, k_ref, v_ref, o_ref, *, tq, ahead):
    T = k_ref.shape[2]
    qi = lax.broadcasted_iota(jnp.int32, (tq, tq), 0)
    ki = lax.broadcasted_iota(jnp.int32, (tq, tq), 1)
    causal = ki <= qi
    lane = lax.broadcasted_iota(jnp.int32, (tq, LANES), 1)
    nt = (((1,), (1,)), ((), ()))
    units = [(ii, hh) for ii in range(T // tq) for hh in range(2)]

    def scores(ii, hh):
        lo = ii * tq
        q = q_ref[0, hh, lo:lo + tq, :]
        s_d = lax.dot_general(q, k_ref[0, hh, lo:lo + tq, :], nt, preferred_element_type=F32)
        s_d = jnp.where(causal, s_d, -jnp.inf)
        m = jnp.max(s_d, axis=-1, keepdims=True)
        s_m = None
        if ii:
            s_m = lax.dot_general(q, k_ref[0, hh, 0:lo, :], nt, preferred_element_type=F32)
            m = jnp.maximum(m, jnp.max(s_m, axis=-1, keepdims=True))
        return s_d, s_m, m

    def values(ii, hh, s_d, s_m, m):
        lo = ii * tq
        o = jnp.dot(jnp.exp2(s_d - m).astype(BF16), v_ref[0, hh, lo:lo + tq, :],
                    preferred_element_type=F32)
        if ii:
            o = o + jnp.dot(jnp.exp2(s_m - m).astype(BF16), v_ref[0, hh, 0:lo, :],
                            preferred_element_type=F32)
        l = jnp.sum(jnp.where(lane == 0, o, 0.0), axis=-1, keepdims=True)
        return o / l

    pending = [scores(*units[u]) for u in range(ahead)]
    outs = []
    for u, (ii, hh) in enumerate(units):
        cur = pending.pop(0)
        if u + ahead < len(units):
            pending.append(scores(*units[u + ahead]))
        outs.append(values(ii, hh, *cur))
        if hh:
            lo = ii * tq
            o_ref[0, 0, lo:lo + tq, :] = jnp.where(
                lane < B_VDIM, pltpu.roll(outs[0], B_VDIM, 1), outs[1]).astype(BF16)
            outs = []


def _attn_b(q, k, v, tq=256, ahead=3):
    B, _, T, _ = q.shape
    heads = pl.BlockSpec((1, 2, T, LANES), lambda b, p: (b, p, 0, 0))
    return pl.pallas_call(
        functools.partial(_attn_b_kernel, tq=tq, ahead=ahead),
        grid=(B, B_PAIRS),
        in_specs=[heads, heads, heads],
        out_specs=pl.BlockSpec((1, 1, T, LANES), lambda b, p: (b, p, 0, 0)),
        out_shape=jax.ShapeDtypeStruct((B, B_PAIRS, T, LANES), BF16),
        compiler_params=_cparams(("parallel", "parallel")),
        name="attn_b",
    )(q, k, v)


FINAL_BUFFERS = 3


def _final_kernel(o_hbm, z_hbm, h_hbm, w_ref, g_ref, out_hbm,
                  obuf, zbuf, hbuf, outbuf, in_sem, out_sem, *, tm):
    nt = h_hbm.shape[1] // tm
    n = h_hbm.shape[0] * nt
    ahead = FINAL_BUFFERS - 1

    def tile(i):
        return i // nt, pl.ds(pl.multiple_of((i % nt) * tm, tm), tm)

    def loads(i, slot):
        b, rows = tile(i)
        return (pltpu.make_async_copy(o_hbm.at[b, :, rows, :], obuf.at[slot], in_sem.at[0, slot]),
                pltpu.make_async_copy(z_hbm.at[b, rows, :], zbuf.at[slot], in_sem.at[1, slot]),
                pltpu.make_async_copy(h_hbm.at[b, rows, :], hbuf.at[slot], in_sem.at[2, slot]))

    def store(i, slot):
        b, rows = tile(i)
        return pltpu.make_async_copy(outbuf.at[slot], out_hbm.at[b, rows, :], out_sem.at[slot])

    for i in range(ahead):
        for c in loads(i, i):
            c.start()

    def step(i, carry):
        slot, oslot = i % FINAL_BUFFERS, i % 2

        @pl.when(i + ahead < n)
        def _():
            for c in loads(i + ahead, (i + ahead) % FINAL_BUFFERS):
                c.start()

        for c in loads(i, slot):
            c.wait()

        @pl.when(i >= 2)
        def _():
            store(i - 2, oslot).wait()

        o = jnp.concatenate([obuf[slot, p] for p in range(obuf.shape[1])], axis=1)
        z = zbuf[slot].astype(F32)
        y = (o.astype(F32) * (z * jax.nn.sigmoid(z))).astype(BF16)
        outbuf[oslot] = hbuf[slot] + _rms(
            jnp.dot(y, w_ref[...], preferred_element_type=F32), g_ref[...])
        store(i, oslot).start()
        return carry

    lax.fori_loop(0, n, step, 0)
    store(n - 2, (n - 2) % 2).wait()
    store(n - 1, (n - 1) % 2).wait()


def _final(o_b, z_b, h1, w, gain, tm=1024):
    B, T, D = h1.shape
    assert T % tm == 0 and B * (T // tm) >= FINAL_BUFFERS
    hbm = pl.BlockSpec(memory_space=pl.ANY)
    vmem = pl.BlockSpec(memory_space=pltpu.VMEM)
    return pl.pallas_call(
        functools.partial(_final_kernel, tm=tm),
        in_specs=[hbm, hbm, hbm, vmem, vmem],
        out_specs=hbm,
        out_shape=jax.ShapeDtypeStruct((B, T, D), F32),
        scratch_shapes=[
            pltpu.VMEM((FINAL_BUFFERS, B_PAIRS, tm, LANES), BF16),
            pltpu.VMEM((FINAL_BUFFERS, tm, B_WIDTH), BF16),
            pltpu.VMEM((FINAL_BUFFERS, tm, D), F32),
            pltpu.VMEM((2, tm, D), F32),
            pltpu.SemaphoreType.DMA((3, FINAL_BUFFERS)),
            pltpu.SemaphoreType.DMA((2,)),
        ],
        compiler_params=pltpu.CompilerParams(vmem_limit_bytes=VMEM_LIMIT),
        name="final",
    )(o_b, z_b, h1, w, gain.reshape(1, D))


def kernel(x, positions, a_pre_norm, a_w_in, a_w_out, a_post_norm, kv_norm, kv_w_down, kv_latent_norm, kv_w_up, b_pre_norm, b_w_in, b_q_norm, b_w_q_up, b_w_out, b_post_norm):
    B, T, D = x.shape
    assert all(w // d == BAND for w, d in zip(A_WINDOWS, A_DILATIONS))
    assert D == A_WIDTH and T % (BAND * max(A_DILATIONS)) == 0 and positions.shape == (B, T)
    cos, sin = _rope_tables(positions)

    w_in = a_w_in[0].astype(BF16)
    qkv = []
    z_a = None
    for g, dil in enumerate(A_DILATIONS):
        if dil == 1:
            qkv_g, z_a = _proj_a(x, cos, sin, a_pre_norm[0], w_in, g, True)
        else:
            (qkv_g,) = _proj_a(x, cos, sin, a_pre_norm[0], w_in, g, False)
        qkv.append(qkv_g)
    o_a = _attn_a(qkv)

    row = lambda g: g.reshape(1, -1)
    wd = jnp.zeros((D, B_KV_LORA + LANES), F32)
    wd = wd.at[:, :B_KV_LORA].set(kv_w_down[:, :B_KV_LORA])
    wd = wd.at[:, B_KV_LORA + B_NOPE:B_KV_LORA + B_QK_DIM].set(kv_w_down[:, B_KV_LORA:])
    wqup = jnp.pad(b_w_q_up[0].reshape(B_Q_LORA, B_HEADS, B_QK_DIM),
                   ((0, 0), (0, 0), (0, LANES - B_QK_DIM))).reshape(B_Q_LORA, B_HEADS * LANES)
    weights = [
        a_w_out[0].astype(BF16), row(a_post_norm[0]), row(kv_norm), wd.astype(BF16),
        row(kv_latent_norm), kv_w_up.astype(BF16),
        row(b_pre_norm[0]), b_w_in[0][:, :B_Q_LORA].astype(BF16),
        b_w_in[0][:, B_Q_LORA:].astype(BF16), row(b_q_norm[0]), wqup.astype(BF16),
    ]
    h1, k_b, v_b, q_b, z_b = _mid(o_a, z_a, x, cos, sin, weights)

    o_b = _attn_b(q_b, k_b, v_b)
    return _final(o_b, z_b, h1, b_w_out[0].astype(BF16), b_post_norm[0])
```

```python
import functools

import jax
import jax.numpy as jnp
from jax import lax
from jax.experimental import pallas as pl
from jax.experimental.pallas import tpu as pltpu

F32 = jnp.float32
BF16 = jnp.bfloat16

NORM_EPS = 1e-6
LOG2E = 1.4426950408889634
LANES = 128

A_WINDOWS = (128, 512, 2048)
A_DILATIONS = (1, 4, 16)
A_HEADS = 8
A_HEAD_DIM = 128
A_WIDTH = A_HEADS * A_HEAD_DIM
A_ROT_DIM = A_HEAD_DIM // 4
A_ROPE_THETA = 500000.0
BAND = 128
MAX_ROW_STRIDE = 4

B_HEADS = 16
B_NOPE = 64
B_ROPE = 32
B_QK_DIM = B_NOPE + B_ROPE
B_VDIM = 64
B_WIDTH = B_HEADS * B_VDIM
B_Q_LORA = 384
B_KV_LORA = 256
B_ROPE_THETA = 10000.0
B_PAIRS = B_HEADS // 2

VMEM_LIMIT = 56 * 1024 * 1024


def _cparams(sem):
    return pltpu.CompilerParams(dimension_semantics=sem, vmem_limit_bytes=VMEM_LIMIT)


def _rms(x, g):
    ms = jnp.mean(x * x, axis=-1, keepdims=True)
    return x * lax.rsqrt(ms + NORM_EPS) * g


def _rot_half(a, lo, cos, sin):
    lane = lax.broadcasted_iota(jnp.int32, a.shape, 1)
    partner = jnp.where(lane < lo + 16, pltpu.roll(a, LANES - 16, 1), pltpu.roll(a, 16, 1))
    return a * cos + partner * sin


def _rope_table_kernel(pe_ref, po_ref, freq_ref, sign_ref, cos_ref, sin_ref):
    half_t = pe_ref.shape[1]
    lane = lax.broadcasted_iota(jnp.int32, (half_t, LANES), 1)
    pos = jnp.where(lane < LANES // 2, pe_ref[0].astype(F32), po_ref[0].astype(F32))
    ang = pos * freq_ref[...]
    for ref, packed, ident in ((cos_ref, jnp.cos(ang), 1.0),
                               (sin_ref, jnp.sin(ang) * sign_ref[...], 0.0)):
        def unpack(lo0, lo1):
            a = packed if lo0 == 0 else pltpu.roll(packed, LANES - lo0, 1)
            b_ = pltpu.roll(packed, (B_NOPE - lo1) % LANES, 1)
            return jnp.where(lane < A_ROT_DIM, a,
                             jnp.where((lane >= B_NOPE) & (lane < B_QK_DIM), b_, ident))
        ref[0, pl.ds(0, half_t, stride=2), :] = unpack(0, A_ROT_DIM)
        ref[0, pl.ds(1, half_t, stride=2), :] = unpack(LANES // 2, LANES // 2 + A_ROT_DIM)


def _rope_lane_rows():
    half = A_ROT_DIM // 2
    inv_a = 1.0 / (A_ROPE_THETA ** (jnp.arange(half, dtype=F32) * (2.0 / A_ROT_DIM)))
    inv_b = 1.0 / (B_ROPE_THETA ** (jnp.arange(B_ROPE // 2, dtype=F32) * (2.0 / B_ROPE)))
    ones = jnp.ones((half,), F32)
    zeros = lambda n: jnp.zeros((n,), F32)
    row = lambda v: v.reshape(1, LANES)
    packed = (row(jnp.tile(jnp.concatenate([inv_a, inv_a, inv_b, inv_b]), 2)),
              row(jnp.tile(jnp.concatenate([-ones, ones, -ones, ones]), 2)))
    table = (row(jnp.concatenate([inv_a, inv_a, zeros(B_NOPE - A_ROT_DIM), inv_b, inv_b,
                                  zeros(LANES - B_QK_DIM)])),
             row(jnp.concatenate([-ones, ones, zeros(B_NOPE - A_ROT_DIM), -ones, ones,
                                  zeros(LANES - B_QK_DIM)])))
    return packed, table


def _rope_tables_direct(positions):
    B, T = positions.shape
    (freq, sign), _ = _rope_lane_rows()
    tab = jax.ShapeDtypeStruct((B, T, LANES), F32)
    spec = pl.BlockSpec((1, T, LANES), lambda b: (b, 0, 0))
    const = pl.BlockSpec((1, LANES), lambda b: (0, 0))
    pos = pl.BlockSpec((1, T // 2, 1), lambda b: (b, 0, 0))
    return pl.pallas_call(
        _rope_table_kernel,
        grid=(B,),
        in_specs=[pos, pos, const, const],
        out_specs=[spec] * 2,
        out_shape=[tab] * 2,
        compiler_params=_cparams(("parallel",)),
        name="rope_tables",
    )(positions[:, 0::2].reshape(B, T // 2, 1), positions[:, 1::2].reshape(B, T // 2, 1),
      freq, sign)


def _rope_shift_kernel(off_ref, freq_ref, sign_ref, ct_ref, st_ref, cos_ref, sin_ref):
    ang = off_ref[0].astype(F32) * freq_ref[...]
    co, so = jnp.cos(ang), jnp.sin(ang) * sign_ref[...]
    ct, st = ct_ref[0], st_ref[0]
    cos_ref[0] = co * ct - so * st
    sin_ref[0] = so * ct + co * st


def _rope_tables(positions):
    B, T = positions.shape
    steps = jnp.arange(T, dtype=positions.dtype)[None, :]

    def shifted(pos):
        ct, st = _rope_tables_direct(steps)
        _, (freq, sign) = _rope_lane_rows()
        tab = jax.ShapeDtypeStruct((B, T, LANES), F32)
        spec = pl.BlockSpec((1, T, LANES), lambda b: (b, 0, 0))
        base = pl.BlockSpec((1, T, LANES), lambda b: (0, 0, 0))
        const = pl.BlockSpec((1, LANES), lambda b: (0, 0))
        return tuple(pl.pallas_call(
            _rope_shift_kernel,
            grid=(B,),
            in_specs=[pl.BlockSpec((1, 1, 1), lambda b: (b, 0, 0)), const, const, base, base],
            out_specs=[spec] * 2,
            out_shape=[tab] * 2,
            compiler_params=_cparams(("parallel",)),
            name="rope_shift",
        )(pos[:, :1].reshape(B, 1, 1), freq, sign, ct, st))

    contiguous = jnp.all(positions == positions[:, :1] + steps)
    return lax.cond(contiguous, shifted, lambda pos: tuple(_rope_tables_direct(pos)), positions)


def _proj_a_kernel(x_ref, c_ref, s_ref, g_ref, w_ref, *rest, dil, with_z, parts):
    if with_z:
        wz_ref, qkv_ref, z_ref = rest[:3]
    else:
        qkv_ref = rest[0]
    tm_all, D = x_ref.shape[1], x_ref.shape[2]
    tm = tm_all // parts
    tl = tm // dil
    nslab = D // LANES

    def by_class(load, stage):
        if dil <= MAX_ROW_STRIDE:
            return jnp.concatenate([load(pl.ds(r, tl, stride=dil)) for r in range(dil)], axis=0)
        inner = dil // MAX_ROW_STRIDE
        tq = tm // MAX_ROW_STRIDE
        for q0 in range(MAX_ROW_STRIDE):
            stage[q0 * tq:(q0 + 1) * tq, :] = load(pl.ds(q0, tq, stride=MAX_ROW_STRIDE))
        return jnp.concatenate(
            [stage[pl.ds(q0 * tq + q1, tl, stride=inner), :]
             for q1 in range(inner) for q0 in range(MAX_ROW_STRIDE)], axis=0)

    def inputs(i):
        rs = slice(i * tm, (i + 1) * tm)
        if dil == 1:
            return x_ref[0, rs, :], c_ref[0, rs, :], s_ref[0, rs, :]
        slabs, stage = rest[-2].at[i], rest[-1].at[i]
        for c in range(nslab):
            slabs[c] = x_ref[0, rs, c * LANES:(c + 1) * LANES]
        slabs[nslab] = c_ref[0, rs, :]
        slabs[nslab + 1] = s_ref[0, rs, :]
        p = [by_class(lambda idx, c=c: slabs[c, idx, :], stage.at[c]) for c in range(nslab + 2)]
        return jnp.concatenate(p[:nslab], axis=1), p[nslab], p[nslab + 1]

    lane = lax.broadcasted_iota(jnp.int32, (tm, LANES), 1)
    scale = A_HEAD_DIM ** -0.5 * LOG2E

    def project(i, x, cos, sin):
        h = _rms(x, g_ref[...]).astype(BF16)
        cos = jnp.where(lane < A_ROT_DIM, cos, 1.0)
        sin = jnp.where(lane < A_ROT_DIM, sin, 0.0)
        for s in range(3):
            acc = jnp.dot(h, w_ref[:, s * A_WIDTH:(s + 1) * A_WIDTH],
                          preferred_element_type=F32)
            for hd in range(A_HEADS):
                a = acc[:, hd * LANES:(hd + 1) * LANES]
                if s == 0:
                    a = _rot_half(a, 0, cos * scale, sin * scale)
                elif s == 1:
                    a = _rot_half(a, 0, cos, sin)
                a = a.astype(BF16)
                for r in range(dil):
                    qkv_ref[0, s * A_HEADS + hd, r, i * tl:(i + 1) * tl, :] = a[r * tl:(r + 1) * tl]
        if with_z:
            z_ref[0, i * tm:(i + 1) * tm, :] = jnp.dot(
                h, wz_ref[...], preferred_element_type=F32).astype(BF16)

    staged = [inputs(i) for i in range(parts)]
    for i in range(parts):
        project(i, *staged[i])


def _proj_a(x, cos, sin, gain, w, group, with_z, tm=512, parts=2):
    B, T, D = x.shape
    dil = A_DILATIONS[group]
    L = T // dil
    tm = tm * parts
    tl = tm // dil
    tok = lambda n: pl.BlockSpec((1, tm, n), lambda b, i: (b, i, 0))
    out_shape = [jax.ShapeDtypeStruct((B, 3 * A_HEADS, dil, L, LANES), BF16)]
    out_specs = [pl.BlockSpec((1, 3 * A_HEADS, dil, tl, LANES), lambda b, i: (b, 0, 0, i, 0))]
    if with_z:
        out_shape.append(jax.ShapeDtypeStruct((B, T, A_WIDTH), BF16))
        out_specs.append(tok(A_WIDTH))
    return pl.pallas_call(
        functools.partial(_proj_a_kernel, dil=dil, with_z=with_z, parts=parts),
        grid=(B, T // tm),
        in_specs=[tok(D), tok(LANES), tok(LANES),
                  pl.BlockSpec((1, D), lambda b, i: (0, 0)),
                  pl.BlockSpec((D, 3 * A_WIDTH), lambda b, i: (0, group))]
                 + ([pl.BlockSpec((D, A_WIDTH), lambda b, i: (0, 3 * len(A_DILATIONS)))]
                    if with_z else []),
        out_specs=out_specs,
        out_shape=out_shape,
        scratch_shapes=([pltpu.VMEM((parts, D // LANES + 2, tm // parts, LANES), F32)] * 2
                        if dil > 1 else []),
        compiler_params=_cparams(("parallel", "parallel")),
        name=f"proj_a_d{dil}",
    )(x, cos, sin, gain.reshape(1, D), *([w, w] if with_z else [w]))


def _attn_a_kernel(g0_ref, g1_ref, g2_ref, o_ref, onat, mnat, lnat, stage, *, chunk):
    for hd in range(o_ref.shape[1]):
        _attn_a_head(hd, g0_ref, g1_ref, g2_ref, o_ref, onat, mnat, lnat, stage, chunk)


def _attn_a_head(hd, g0_ref, g1_ref, g2_ref, o_ref, onat, mnat, lnat, stage, chunk):
    T = o_ref.shape[2]
    nt = (((1,), (1,)), ((), ()))
    qi = lax.broadcasted_iota(jnp.int32, (BAND, 2 * BAND), 0)
    ki = lax.broadcasted_iota(jnp.int32, (BAND, 2 * BAND), 1)
    band_mask = (ki >= qi) & (ki <= qi + BAND)
    first_mask = (lax.broadcasted_iota(jnp.int32, (BAND, BAND), 1)
                  <= lax.broadcasted_iota(jnp.int32, (BAND, BAND), 0))
    nblk = T // BAND
    refs = (g0_ref, g1_ref, g2_ref)

    def key_lo(g, j):
        nb = T // A_DILATIONS[g] // BAND
        return (j - 1) * BAND if j % nb else j * BAND

    def scores(g, c0):
        ref, nb = refs[g], T // A_DILATIONS[g] // BAND
        js = range(c0, c0 + chunk)
        s = [lax.dot_general(ref[0, 0, hd, j * BAND:(j + 1) * BAND, :],
                             ref[0, 1, hd, key_lo(g, j):(j + 1) * BAND, :], nt,
                             preferred_element_type=F32) for j in js]
        s = [jnp.where(band_mask if j % nb else first_mask, x, -jnp.inf) for j, x in zip(js, s)]
        m = [jnp.max(x, axis=-1, keepdims=True) for x in s]
        return s, m

    def values(g, c0, s, m):
        ref, dil = refs[g], A_DILATIONS[g]
        nb = T // dil // BAND
        js = range(c0, c0 + chunk)
        p = [jnp.exp2(x - y).astype(BF16) for x, y in zip(s, m)]
        o = [jnp.dot(x, jnp.concatenate([ref[0, 2, hd, key_lo(g, j):(j + 1) * BAND, :],
                                         jnp.ones(((j + 1) * BAND - key_lo(g, j), LANES), BF16)],
                                        axis=1),
                     preferred_element_type=F32) for x, j in zip(p, js)]
        for j, oj, mj in zip(js, o, m):
            r, n = divmod(j, nb)
            mj = jnp.broadcast_to(mj, (BAND, LANES))
            oj, lj = oj[:, :LANES], oj[:, LANES:]
            if dil == 1:
                sl = slice(j * BAND, (j + 1) * BAND)
                m1, m2 = mnat[0, sl, :], mnat[1, sl, :]
                mx = jnp.maximum(jnp.maximum(mj, m1), m2)
                e0, e1, e2 = jnp.exp2(mj - mx), jnp.exp2(m1 - mx), jnp.exp2(m2 - mx)
                num = e0 * oj + e1 * onat[0, sl, :] + e2 * onat[1, sl, :]
                den = e0 * lj + e1 * lnat[0, sl, :] + e2 * lnat[1, sl, :]
                o_ref[0, hd, sl, :] = (num / den).astype(BF16)
            elif dil > MAX_ROW_STRIDE:
                inner = dil // MAX_ROW_STRIDE
                r1, r0 = divmod(r, MAX_ROW_STRIDE)
                idx = pl.ds(r1 + n * BAND * inner, BAND, stride=inner)
                stage[0, r0, idx, :] = oj
                stage[1, r0, idx, :] = mj
                stage[2, r0, idx, :] = lj
            else:
                idx = pl.ds(r + n * BAND * dil, BAND, stride=dil)
                onat[g - 1, idx, :] = oj
                mnat[g - 1, idx, :] = mj
                lnat[g - 1, idx, :] = lj

    def second_pass(g):
        for a, nat in enumerate((onat, mnat, lnat)):
            for r0 in range(MAX_ROW_STRIDE):
                nat[g - 1, pl.ds(r0, T // MAX_ROW_STRIDE, stride=MAX_ROW_STRIDE), :] = stage[a, r0]

    order = sorted(range(len(refs)), key=lambda g: -A_DILATIONS[g])
    assert A_DILATIONS[order[-1]] == 1
    work = [(g, c0) for g in order for c0 in range(0, nblk, chunk)]
    nxt = scores(*work[0])
    for w, (g, c0) in enumerate(work):
        cur = nxt
        if w + 1 < len(work):
            nxt = scores(*work[w + 1])
        values(g, c0, *cur)
        last_of_group = w + 1 == len(work) or work[w + 1][0] != g
        if last_of_group and A_DILATIONS[g] > MAX_ROW_STRIDE:
            second_pass(g)


def _attn_a(qkv, chunk=8, heads=2):
    B, T = qkv[0].shape[0], qkv[0].shape[2] * qkv[0].shape[3]
    views = [a.reshape(B, 3, A_HEADS, T, LANES) for a in qkv]
    spec = pl.BlockSpec((1, 3, heads, T, LANES), lambda b, h: (b, 0, h, 0, 0))
    return pl.pallas_call(
        functools.partial(_attn_a_kernel, chunk=chunk),
        grid=(B, A_HEADS // heads),
        in_specs=[spec] * 3,
        out_specs=pl.BlockSpec((1, heads, T, LANES), lambda b, h: (b, h, 0, 0)),
        out_shape=jax.ShapeDtypeStruct((B, A_HEADS, T, LANES), BF16),
        scratch_shapes=[pltpu.VMEM((len(A_DILATIONS) - 1, T, LANES), F32)] * 3
                       + [pltpu.VMEM((3, MAX_ROW_STRIDE, T // MAX_ROW_STRIDE, LANES), F32)],
        compiler_params=_cparams(("parallel", "parallel")),
        name="attn_a",
    )(*views)


def _mid_kernel(o_ref, z_ref, x_ref, cb_ref, sb_ref,
                wout_ref, gpost_ref, gkv_ref, wd_ref, glat_ref, wup_ref,
                gpre_ref, wcq_ref, wz_ref, gq_ref, wqup_ref,
                h1_ref, k_ref, v_ref, q_ref, zb_ref, *, parts):
    tm = x_ref.shape[1]
    tp = tm // parts
    lane = lax.broadcasted_iota(jnp.int32, (tp, LANES), 1)
    one_col = jnp.where(lane == 0, 1.0, 0.0)
    scale = B_QK_DIM ** -0.5 * LOG2E
    st = [dict() for _ in range(parts)]

    def stage1(i):
        rs = slice(i * tp, (i + 1) * tp)
        o = jnp.concatenate([o_ref[0, h, rs, :] for h in range(o_ref.shape[1])], axis=1)
        z = z_ref[0, rs, :].astype(F32)
        g = (o.astype(F32) * (z * jax.nn.sigmoid(z))).astype(BF16)
        y = jnp.dot(g, wout_ref[...], preferred_element_type=F32)
        h1 = x_ref[0, rs, :] + _rms(y, gpost_ref[...])
        h1_ref[0, rs, :] = h1
        st[i]["h1"] = h1

    def stage2(i):
        h1 = st[i].pop("h1")
        hn = _rms(h1, gkv_ref[...]).astype(BF16)
        hb = _rms(h1, gpre_ref[...]).astype(BF16)
        st[i]["ckr"] = jnp.dot(hn, wd_ref[...], preferred_element_type=F32)
        st[i]["cq"] = jnp.dot(hb, wcq_ref[...], preferred_element_type=F32)
        st[i]["hb"] = hb

    def stage3(i):
        rs = slice(i * tp, (i + 1) * tp)
        ckr = st[i].pop("ckr")
        c_kv = _rms(ckr[:, :B_KV_LORA], glat_ref[...]).astype(BF16)
        c_q = _rms(st[i].pop("cq"), gq_ref[...]).astype(BF16)
        st[i]["kv"] = jnp.dot(c_kv, wup_ref[...], preferred_element_type=F32)
        st[i]["qq"] = jnp.dot(c_q, wqup_ref[...], preferred_element_type=F32)
        zb_ref[0, rs, :] = jnp.dot(st[i].pop("hb"), wz_ref[...],
                                   preferred_element_type=F32).astype(BF16)
        st[i]["kr"] = ckr[:, B_KV_LORA:]

    def stage4(i):
        rs = slice(i * tp, (i + 1) * tp)
        cos = jnp.where(lane >= B_NOPE, cb_ref[0, rs, :], 1.0)
        sin = jnp.where(lane >= B_NOPE, sb_ref[0, rs, :], 0.0)
        k_rope = _rot_half(st[i].pop("kr"), B_NOPE, cos, sin)
        kv, qq = st[i].pop("kv"), st[i].pop("qq")
        cq, sq = cos * scale, sin * scale
        for h in range(B_HEADS):
            blk = kv[:, h * LANES:(h + 1) * LANES]
            k_ref[0, h, rs, :] = jnp.where(lane < B_NOPE, blk, k_rope).astype(BF16)
            v_ref[0, h, rs, :] = jnp.where(lane >= B_NOPE, blk, one_col).astype(BF16)
            q_ref[0, h, rs, :] = _rot_half(qq[:, h * LANES:(h + 1) * LANES],
                                           B_NOPE, cq, sq).astype(BF16)

    for stage in (stage1, stage2, stage3, stage4):
        for i in range(parts):
            stage(i)


def _mid(o_a, z_a, x, cos, sin, weights, tm=512, parts=2):
    B, T, D = x.shape
    tok = lambda w: pl.BlockSpec((1, tm, w), lambda b, i: (b, i, 0))
    heads = lambda n: pl.BlockSpec((1, n, tm, LANES), lambda b, i: (b, 0, i, 0))
    const = lambda a: pl.BlockSpec(a.shape, lambda b, i: (0, 0), pipeline_mode=pl.Buffered(1))
    return pl.pallas_call(
        functools.partial(_mid_kernel, parts=parts),
        grid=(B, T // tm),
        in_specs=[heads(A_HEADS), tok(A_WIDTH), tok(D), tok(LANES), tok(LANES)]
                 + [const(w) for w in weights],
        out_specs=[tok(D), heads(B_HEADS), heads(B_HEADS), heads(B_HEADS), tok(B_WIDTH)],
        out_shape=[
            jax.ShapeDtypeStruct((B, T, D), F32),
            jax.ShapeDtypeStruct((B, B_HEADS, T, LANES), BF16),
            jax.ShapeDtypeStruct((B, B_HEADS, T, LANES), BF16),
            jax.ShapeDtypeStruct((B, B_HEADS, T, LANES), BF16),
            jax.ShapeDtypeStruct((B, T, B_WIDTH), BF16),
        ],
        compiler_params=_cparams(("parallel", "parallel")),
        name="mid",
    )(o_a, z_a, x, cos, sin, *weights)


def _attn_b_kernel(q_ref, k_ref, v_ref, o_ref, *, tq, ahead):
    T = k_ref.shape[2]
    qi = lax.broadcasted_iota(jnp.int32, (tq, tq), 0)
    ki = lax.broadcasted_iota(jnp.int32, (tq, tq), 1)
    causal = ki <= qi
    lane = lax.broadcasted_iota(jnp.int32, (tq, LANES), 1)
    nt = (((1,), (1,)), ((), ()))
    units = [(ii, hh) for ii in range(T // tq) for hh in range(2)]

    def scores(ii, hh):
        lo = ii * tq
        q = q_ref[0, hh, lo:lo + tq, :]
        s_d = lax.dot_general(q, k_ref[0, hh, lo:lo + tq, :], nt, preferred_element_type=F32)
        s_d = jnp.where(causal, s_d, -jnp.inf)
        m = jnp.max(s_d, axis=-1, keepdims=True)
        s_m = None
        if ii:
            s_m = lax.dot_general(q, k_ref[0, hh, 0:lo, :], nt, preferred_element_type=F32)
            m = jnp.maximum(m, jnp.max(s_m, axis=-1, keepdims=True))
        return s_d, s_m, m

    def values(ii, hh, s_d, s_m, m):
        lo = ii * tq
        o = jnp.dot(jnp.exp2(s_d - m).astype(BF16), v_ref[0, hh, lo:lo + tq, :],
                    preferred_element_type=F32)
        if ii:
            o = o + jnp.dot(jnp.exp2(s_m - m).astype(BF16), v_ref[0, hh, 0:lo, :],
                            preferred_element_type=F32)
        l = jnp.sum(jnp.where(lane == 0, o, 0.0), axis=-1, keepdims=True)
        return o / l

    pending = [scores(*units[u]) for u in range(ahead)]
    outs = []
    for u, (ii, hh) in enumerate(units):
        cur = pending.pop(0)
        if u + ahead < len(units):
            pending.append(scores(*units[u + ahead]))
        outs.append(values(ii, hh, *cur))
        if hh:
            lo = ii * tq
            o_ref[0, 0, lo:lo + tq, :] = jnp.where(
                lane < B_VDIM, pltpu.roll(outs[0], B_VDIM, 1), outs[1]).astype(BF16)
            outs = []


def _attn_b(q, k, v, tq=256, ahead=3):
    B, _, T, _ = q.shape
    heads = pl.BlockSpec((1, 2, T, LANES), lambda b, p: (b, p, 0, 0))
    return pl.pallas_call(
        functools.partial(_attn_b_kernel, tq=tq, ahead=ahead),
        grid=(B, B_PAIRS),
        in_specs=[heads, heads, heads],
        out_specs=pl.BlockSpec((1, 1, T, LANES), lambda b, p: (b, p, 0, 0)),
        out_shape=jax.ShapeDtypeStruct((B, B_PAIRS, T, LANES), BF16),
        compiler_params=_cparams(("parallel", "parallel")),
        name="attn_b",
    )(q, k, v)


FINAL_BUFFERS = 3


def _final_kernel(o_hbm, z_hbm, h_hbm, w_ref, g_ref, out_hbm,
                  obuf, zbuf, hbuf, outbuf, in_sem, out_sem, *, tm):
    nt = h_hbm.shape[1] // tm
    n = h_hbm.shape[0] * nt
    ahead = FINAL_BUFFERS - 1

    def tile(i):
        return i // nt, pl.ds(pl.multiple_of((i % nt) * tm, tm), tm)

    def loads(i, slot):
        b, rows = tile(i)
        return (pltpu.make_async_copy(o_hbm.at[b, :, rows, :], obuf.at[slot], in_sem.at[0, slot]),
                pltpu.make_async_copy(z_hbm.at[b, rows, :], zbuf.at[slot], in_sem.at[1, slot]),
                pltpu.make_async_copy(h_hbm.at[b, rows, :], hbuf.at[slot], in_sem.at[2, slot]))

    def store(i, slot):
        b, rows = tile(i)
        return pltpu.make_async_copy(outbuf.at[slot], out_hbm.at[b, rows, :], out_sem.at[slot])

    for i in range(ahead):
        for c in loads(i, i):
            c.start()

    def step(i, carry):
        slot, oslot = i % FINAL_BUFFERS, i % 2

        @pl.when(i + ahead < n)
        def _():
            for c in loads(i + ahead, (i + ahead) % FINAL_BUFFERS):
                c.start()

        for c in loads(i, slot):
            c.wait()

        @pl.when(i >= 2)
        def _():
            store(i - 2, oslot).wait()

        o = jnp.concatenate([obuf[slot, p] for p in range(obuf.shape[1])], axis=1)
        z = zbuf[slot].astype(F32)
        y = (o.astype(F32) * (z * jax.nn.sigmoid(z))).astype(BF16)
        outbuf[oslot] = hbuf[slot] + _rms(
            jnp.dot(y, w_ref[...], preferred_element_type=F32), g_ref[...])
        store(i, oslot).start(priority=1)
        return carry

    lax.fori_loop(0, n, step, 0)
    store(n - 2, (n - 2) % 2).wait()
    store(n - 1, (n - 1) % 2).wait()


def _final(o_b, z_b, h1, w, gain, tm=1024):
    B, T, D = h1.shape
    assert T % tm == 0 and B * (T // tm) >= FINAL_BUFFERS
    hbm = pl.BlockSpec(memory_space=pl.ANY)
    vmem = pl.BlockSpec(memory_space=pltpu.VMEM)
    return pl.pallas_call(
        functools.partial(_final_kernel, tm=tm),
        in_specs=[hbm, hbm, hbm, vmem, vmem],
        out_specs=hbm,
        out_shape=jax.ShapeDtypeStruct((B, T, D), F32),
        scratch_shapes=[
            pltpu.VMEM((FINAL_BUFFERS, B_PAIRS, tm, LANES), BF16),
            pltpu.VMEM((FINAL_BUFFERS, tm, B_WIDTH), BF16),
            pltpu.VMEM((FINAL_BUFFERS, tm, D), F32),
            pltpu.VMEM((2, tm, D), F32),
            pltpu.SemaphoreType.DMA((3, FINAL_BUFFERS)),
            pltpu.SemaphoreType.DMA((2,)),
        ],
        compiler_params=pltpu.CompilerParams(vmem_limit_bytes=VMEM_LIMIT),
        name="final",
    )(o_b, z_b, h1, w, gain.reshape(1, D))


def kernel(x, positions, a_pre_norm, a_w_in, a_w_out, a_post_norm, kv_norm, kv_w_down, kv_latent_norm, kv_w_up, b_pre_norm, b_w_in, b_q_norm, b_w_q_up, b_w_out, b_post_norm):
    B, T, D = x.shape
    assert all(w // d == BAND for w, d in zip(A_WINDOWS, A_DILATIONS))
    assert D == A_WIDTH and T % (BAND * max(A_DILATIONS)) == 0 and positions.shape == (B, T)
    cos, sin = _rope_tables(positions)

    w_in = a_w_in[0].astype(BF16)
    qkv = []
    z_a = None
    for g, dil in enumerate(A_DILATIONS):
        if dil == 1:
            qkv_g, z_a = _proj_a(x, cos, sin, a_pre_norm[0], w_in, g, True)
        else:
            (qkv_g,) = _proj_a(x, cos, sin, a_pre_norm[0], w_in, g, False)
        qkv.append(qkv_g)
    o_a = _attn_a(qkv)

    row = lambda g: g.reshape(1, -1)
    wd = jnp.zeros((D, B_KV_LORA + LANES), F32)
    wd = wd.at[:, :B_KV_LORA].set(kv_w_down[:, :B_KV_LORA])
    wd = wd.at[:, B_KV_LORA + B_NOPE:B_KV_LORA + B_QK_DIM].set(kv_w_down[:, B_KV_LORA:])
    wqup = jnp.pad(b_w_q_up[0].reshape(B_Q_LORA, B_HEADS, B_QK_DIM),
                   ((0, 0), (0, 0), (0, LANES - B_QK_DIM))).reshape(B_Q_LORA, B_HEADS * LANES)
    weights = [
        a_w_out[0].astype(BF16), row(a_post_norm[0]), row(kv_norm), wd.astype(BF16),
        row(kv_latent_norm), kv_w_up.astype(BF16),
        row(b_pre_norm[0]), b_w_in[0][:, :B_Q_LORA].astype(BF16),
        b_w_in[0][:, B_Q_LORA:].astype(BF16), row(b_q_norm[0]), wqup.astype(BF16),
    ]
    h1, k_b, v_b, q_b, z_b = _mid(o_a, z_a, x, cos, sin, weights)

    o_b = _attn_b(q_b, k_b, v_b)
    return _final(o_b, z_b, h1, b_w_out[0].astype(BF16), b_post_norm[0])
```
